```python
import math
import jax
import jax.numpy as jnp
from jax import lax
import numpy as np

D_MODEL = 1024
BATCH = 8
SEQ = 2048
DEPTH = 2
DEC_BATCH = 128
DEC_SEQ = 1
PAST_LEN = 2048
PAGE_SIZE = 128

HEAD_DIM = 64
D_SB = D_MODEL // 2
D_RWKV = D_MODEL // 2
N_HEADS_SB = D_SB // HEAD_DIM
N_HEADS_RWKV = D_RWKV // HEAD_DIM
RANK_DECAY = D_MODEL // 16
RANK_ICLR = D_MODEL // 16
RANK_GATE = D_MODEL // 8
SB_COLS = 3 * D_SB
RWKV_SPLITS = [D_RWKV, 2 * D_RWKV, 3 * D_RWKV, 3 * D_RWKV + RANK_DECAY, 3 * D_RWKV + RANK_DECAY + RANK_ICLR]
RWKV_COLS = 3 * D_RWKV + RANK_DECAY + RANK_ICLR + RANK_GATE
IN_COLS = SB_COLS + RWKV_COLS
Q_BLOCK = 128
SB_BIAS_LO = -7.0
SB_BIAS_HI = -4.0
S5_GROUP = 16
S5_GROUPS = D_MODEL // S5_GROUP
S5_STATE = 64
D_FF = ((8 * D_MODEL // 3 + 127) // 128) * 128
N_EVEN = (DEPTH + 1) // 2
N_ODD = DEPTH // 2
DN_ALPHA = (2.0 * DEPTH) ** 0.25
DN_BETA = (8.0 * DEPTH) ** -0.25
LN_EPS = 1e-5
GN_EPS = 64e-5

kernel_name = 'hybrid_sb_rwkv7_s5_macaron_deepnorm_step'


def layer_norm(x, g, b):
    xf = x.astype(jnp.float32)
    mu = jnp.mean(xf, axis=-1, keepdims=True)
    var = jnp.mean(jnp.square(xf - mu), axis=-1, keepdims=True)
    return ((xf - mu) * lax.rsqrt(var + LN_EPS) * g + b).astype(x.dtype)


def swiglu(x, wg, wu, wd):
    return (jax.nn.silu(x @ wg) * (x @ wu)) @ wd


def heads(t):
    return t.reshape(t.shape[:-1] + (-1, HEAD_DIM))


def stick_breaking(q, k, v, bias, q_offset):
    tq = q.shape[1]
    scale = HEAD_DIM ** -0.5
    bias = bias.astype(jnp.float32)[None, :, None, None]
    outs = []
    for qs in range(0, tq, Q_BLOCK):
        qe = min(qs + Q_BLOCK, tq)
        kend = q_offset + qe
        z = jnp.einsum('bqhd,bkhd->bhqk', q[:, qs:qe], k[:, :kend],
                       preferred_element_type=jnp.float32) * scale + bias
        q_pos = q_offset + jnp.arange(qs, qe)
        k_pos = jnp.arange(kend)
        mask = k_pos[None, :] < q_pos[:, None]
        log1m = jnp.where(mask, jax.nn.log_sigmoid(-z), 0.0)
        later = lax.cumsum(log1m, axis=3, reverse=True) - log1m
        w = jnp.where(mask, jnp.exp(jax.nn.log_sigmoid(z) + later), 0.0)
        outs.append(jnp.einsum('bhqk,bkhd->bqhd', w.astype(v.dtype), v[:, :kend]))
    return jnp.concatenate(outs, axis=1)


def rwkv7_recurrence(r, w, k, v, kk, a, s0):
    def step(s, inp):
        r_t, w_t, k_t, v_t, kk_t, a_t = inp
        sa = jnp.einsum('bhij,bhj->bhi', s, kk_t)
        s = (s * w_t[:, :, None, :] - sa[..., None] * (kk_t * a_t)[:, :, None, :]
             + v_t[..., None] * k_t[:, :, None, :])
        return s, jnp.einsum('bhij,bhj->bhi', s, r_t)
    xs = tuple(jnp.swapaxes(t, 0, 1) for t in (r, w, k, v, kk, a))
    s_final, y = lax.scan(step, s0, xs)
    return jnp.swapaxes(y, 0, 1), s_final


def even_mixer(h, past_k, past_v, wkv0, shift0, prm, i):
    f32 = jnp.float32
    bsz, t, _ = h.shape
    proj = h @ prm['w_in_even'][i]
    q, k, v = (heads(c) for c in jnp.split(proj[..., :SB_COLS], 3, axis=-1))
    if past_k is None:
        o_sb = stick_breaking(q, k, v, prm['sb_bias'][i], 0)
    else:
        k_all = jnp.concatenate([past_k.astype(k.dtype), k], axis=1)
        v_all = jnp.concatenate([past_v.astype(v.dtype), v], axis=1)
        o_sb = stick_breaking(q, k_all, v_all, prm['sb_bias'][i], past_k.shape[1])
    o_sb = o_sb.reshape(bsz, t, D_SB).astype(h.dtype)

    pb = proj[..., SB_COLS:]
    prev = jnp.concatenate([shift0[:, None, :].astype(pb.dtype), pb[:, :-1]], axis=1)
    pm = pb + prm['mu_shift'][i] * (prev - pb)
    new_shift = pb[:, -1]
    r, kr, vr, wd, ad, gd = jnp.split(pm, RWKV_SPLITS, axis=-1)
    z_w = (prm['w0'][i] + jnp.tanh(wd) @ prm['w_w2'][i]).astype(f32)
    decay = jnp.exp(-jnp.exp(-jax.nn.softplus(-z_w) - 0.5))
    iclr = jax.nn.sigmoid((prm['a0'][i] + ad @ prm['w_a2'][i]).astype(f32))
    gate = (jax.nn.sigmoid(gd) @ prm['w_g2'][i]).astype(f32)
    kk = heads(kr.astype(f32) * prm['k_k'][i].astype(f32))
    kk = kk * lax.rsqrt(jnp.maximum(jnp.sum(kk * kk, axis=-1, keepdims=True), 1e-24))
    kf = heads(kr.astype(f32) * (1.0 + (iclr - 1.0) * prm['k_a'][i].astype(f32)))
    rf = heads(r.astype(f32))
    vf = heads(vr.astype(f32))
    y, wkv_final = rwkv7_recurrence(rf, heads(decay), kf, vf, kk, heads(iclr), wkv0.astype(f32))
    mu = jnp.mean(y, axis=-1, keepdims=True)
    var = jnp.mean(jnp.square(y - mu), axis=-1, keepdims=True)
    yn = ((y - mu) * lax.rsqrt(var + GN_EPS)).reshape(bsz, t, D_RWKV)
    yn = yn * prm['gn_g'][i].astype(f32) + prm['gn_b'][i].astype(f32)
    bonus = jnp.sum(rf * kf * prm['r_k'][i].astype(f32), axis=-1, keepdims=True) * vf
    o_rwkv = ((yn + bonus.reshape(bsz, t, D_RWKV)) * gate).astype(h.dtype)

    out = jnp.concatenate([o_sb, o_rwkv], axis=-1) @ prm['w_out_even'][i]
    return out, k, v, wkv_final, new_shift


def s5_mixer(h, s0, prm, i):
    f32 = jnp.float32
    bsz, t, _ = h.shape
    u = h.astype(f32).reshape(bsz, t, S5_GROUPS, S5_GROUP)
    dt = jnp.exp(prm['log_dt'][i].astype(f32))[:, None]
    lr = prm['lam_re'][i].astype(f32)
    li = prm['lam_im'][i].astype(f32)
    mag = jnp.exp(lr * dt)
    ar = mag * jnp.cos(li * dt)
    ai = mag * jnp.sin(li * dt)
    den = lr * lr + li * li
    cr = ((ar - 1.0) * lr + ai * li) / den
    ci = (ai * lr - (ar - 1.0) * li) / den
    br = prm['b_re'][i].astype(f32)
    bi = prm['b_im'][i].astype(f32)
    bbr = cr[..., None] * br - ci[..., None] * bi
    bbi = cr[..., None] * bi + ci[..., None] * br
    bu_r = jnp.einsum('gpc,btgc->btgp', bbr, u)
    bu_i = jnp.einsum('gpc,btgc->btgp', bbi, u)
    a_r = jnp.broadcast_to(ar, (1, t, S5_GROUPS, S5_STATE))
    a_i = jnp.broadcast_to(ai, (1, t, S5_GROUPS, S5_STATE))

    def combine(e1, e2):
        a1r, a1i, b1r, b1i = e1
        a2r, a2i, b2r, b2i = e2
        return (a1r * a2r - a1i * a2i, a1r * a2i + a1i * a2r,
                a2r * b1r - a2i * b1i + b2r, a2r * b1i + a2i * b1r + b2i)

    pr, pi, sr, si = lax.associative_scan(combine, (a_r, a_i, bu_r, bu_i), axis=1)
    s0r = s0[..., 0].astype(f32)[:, None]
    s0i = s0[..., 1].astype(f32)[:, None]
    sr = sr + pr * s0r - pi * s0i
    si = si + pr * s0i + pi * s0r
    y = (jnp.einsum('gcp,btgp->btgc', prm['c_re'][i].astype(f32), sr)
         - jnp.einsum('gcp,btgp->btgc', prm['c_im'][i].astype(f32), si)
         + prm['d_skip'][i].astype(f32).reshape(S5_GROUPS, S5_GROUP) * u)
    zg = jax.nn.gelu(y.reshape(bsz, t, D_MODEL)).astype(h.dtype)
    out = (zg @ prm['w_glu_out'][i]) * jax.nn.sigmoid(zg @ prm['w_glu_gate'][i])
    s_new = jnp.stack([sr[:, -1], si[:, -1]], axis=-1)
    return out, s_new


def trunk(x, cache_k, cache_v, page_table, wkv0, shift0, s50, prm):
    out_k, out_v, out_wkv, out_shift, out_s5 = [], [], [], [], []
    for layer in range(DEPTH):
        i = layer // 2
        g = prm['ln_g'][layer]
        b = prm['ln_b'][layer]
        f = swiglu(x, prm['ffn1_wg'][layer], prm['ffn1_wu'][layer], prm['ffn1_wd'][layer])
        x = layer_norm(DN_ALPHA * x + 0.5 * f, g[0], b[0])
        if layer % 2 == 0:
            if cache_k is None:
                past_k = None
                past_v = None
            else:
                nb = page_table.shape[0]
                past_k = cache_k[i][page_table].reshape(nb, -1, N_HEADS_SB, HEAD_DIM)
                past_v = cache_v[i][page_table].reshape(nb, -1, N_HEADS_SB, HEAD_DIM)
            mix, k, v, wkv, shift = even_mixer(x, past_k, past_v, wkv0[i], shift0[i], prm, i)
            out_k.append(k)
            out_v.append(v)
            out_wkv.append(wkv)
            out_shift.append(shift)
        else:
            mix, s5 = s5_mixer(x, s50[i], prm, i)
            out_s5.append(s5)
        x = layer_norm(DN_ALPHA * x + mix, g[1], b[1])
        f = swiglu(x, prm['ffn2_wg'][layer], prm['ffn2_wu'][layer], prm['ffn2_wd'][layer])
        x = layer_norm(DN_ALPHA * x + 0.5 * f, g[2], b[2])
    return (x, jnp.stack(out_k), jnp.stack(out_v), jnp.stack(out_wkv),
            jnp.stack(out_shift), jnp.stack(out_s5))


def setup_inputs(seed: int = 0) -> dict:
    key = jax.random.key(seed)
    ks = iter(jax.random.split(key, 64))
    f32 = jnp.float32

    def nrm(shape, s=1.0):
        return s * jax.random.normal(next(ks), shape, f32)

    def uni(shape, lo, hi):
        return jax.random.uniform(next(ks), shape, f32, lo, hi)

    n_pages = PAST_LEN // PAGE_SIZE
    n_pool = (5 * DEC_BATCH * n_pages + 3) // 4
    x_prompt = nrm((BATCH, SEQ, D_MODEL))
    x_sample = nrm((DEC_BATCH, DEC_SEQ, D_MODEL))
    cache_k_sb = nrm((N_EVEN, n_pool, PAGE_SIZE, N_HEADS_SB, HEAD_DIM))
    cache_v_sb = nrm((N_EVEN, n_pool, PAGE_SIZE, N_HEADS_SB, HEAD_DIM))
    page_table = jax.random.permutation(next(ks), n_pool)[:DEC_BATCH * n_pages]
    page_table = page_table.reshape(DEC_BATCH, n_pages).astype(jnp.int32)
    state_wkv = nrm((N_EVEN, DEC_BATCH, N_HEADS_RWKV, HEAD_DIM, HEAD_DIM), 0.3)
    state_shift = nrm((N_EVEN, DEC_BATCH, RWKV_COLS))
    state_s5 = nrm((N_ODD, DEC_BATCH, S5_GROUPS, S5_STATE, 2), 0.3)
    ln_g = 1.0 + nrm((DEPTH, 3, D_MODEL), 0.02)
    ln_b = nrm((DEPTH, 3, D_MODEL), 0.02)
    ffn1_wg = nrm((DEPTH, D_MODEL, D_FF), D_MODEL ** -0.5)
    ffn1_wu = nrm((DEPTH, D_MODEL, D_FF), D_MODEL ** -0.5)
    ffn1_wd = nrm((DEPTH, D_FF, D_MODEL), DN_BETA * D_FF ** -0.5)
    ffn2_wg = nrm((DEPTH, D_MODEL, D_FF), D_MODEL ** -0.5)
    ffn2_wu = nrm((DEPTH, D_MODEL, D_FF), D_MODEL ** -0.5)
    ffn2_wd = nrm((DEPTH, D_FF, D_MODEL), DN_BETA * D_FF ** -0.5)
    w_in_even = nrm((N_EVEN, D_MODEL, IN_COLS), D_MODEL ** -0.5)
    w_out_even = nrm((N_EVEN, D_SB + D_RWKV, D_MODEL), DN_BETA * (D_SB + D_RWKV) ** -0.5)
    sb_bias = (jnp.linspace(SB_BIAS_LO, SB_BIAS_HI, N_HEADS_SB, dtype=f32)[None, :]
               + nrm((N_EVEN, N_HEADS_SB), 0.05))
    mu_shift = uni((N_EVEN, RWKV_COLS), 0.0, 1.0)
    w0 = uni((N_EVEN, D_RWKV), -6.0, -1.0)
    w_w2 = nrm((N_EVEN, RANK_DECAY, D_RWKV), 0.5 * RANK_DECAY ** -0.5)
    a0 = nrm((N_EVEN, D_RWKV), 0.1)
    w_a2 = nrm((N_EVEN, RANK_ICLR, D_RWKV), 0.5 * RANK_ICLR ** -0.5)
    w_g2 = nrm((N_EVEN, RANK_GATE, D_RWKV), RANK_GATE ** -0.5)
    k_k = 0.85 + nrm((N_EVEN, D_RWKV), 0.02)
    k_a = 1.0 + nrm((N_EVEN, D_RWKV), 0.02)
    r_k = nrm((N_EVEN, N_HEADS_RWKV, HEAD_DIM), 0.1)
    gn_g = 1.0 + nrm((N_EVEN, D_RWKV), 0.02)
    gn_b = nrm((N_EVEN, D_RWKV), 0.02)
    lam_re = -0.5 + nrm((N_ODD, S5_GROUPS, S5_STATE), 0.01)
    lam_im = math.pi * jnp.arange(S5_STATE, dtype=f32) + nrm((N_ODD, S5_GROUPS, S5_STATE), 0.01)
    log_dt = uni((N_ODD, S5_GROUPS), math.log(1e-3), math.log(1e-1))
    b_re = nrm((N_ODD, S5_GROUPS, S5_STATE, S5_GROUP), (2.0 * S5_GROUP) ** -0.5)
    b_im = nrm((N_ODD, S5_GROUPS, S5_STATE, S5_GROUP), (2.0 * S5_GROUP) ** -0.5)
    c_re = nrm((N_ODD, S5_GROUPS, S5_GROUP, S5_STATE), S5_STATE ** -0.5)
    c_im = nrm((N_ODD, S5_GROUPS, S5_GROUP, S5_STATE), S5_STATE ** -0.5)
    d_skip = nrm((N_ODD, D_MODEL))
    w_glu_out = nrm((N_ODD, D_MODEL, D_MODEL), DN_BETA * D_MODEL ** -0.5)
    w_glu_gate = nrm((N_ODD, D_MODEL, D_MODEL), D_MODEL ** -0.5)
    return {'x_prompt': x_prompt, 'x_sample': x_sample, 'cache_k_sb': cache_k_sb,
            'cache_v_sb': cache_v_sb, 'page_table': page_table, 'state_wkv': state_wkv,
            'state_shift': state_shift, 'state_s5': state_s5, 'ln_g': ln_g, 'ln_b': ln_b,
            'ffn1_wg': ffn1_wg, 'ffn1_wu': ffn1_wu, 'ffn1_wd': ffn1_wd,
            'ffn2_wg': ffn2_wg, 'ffn2_wu': ffn2_wu, 'ffn2_wd': ffn2_wd,
            'w_in_even': w_in_even, 'w_out_even': w_out_even, 'sb_bias': sb_bias,
            'mu_shift': mu_shift,
            'w0': w0, 'w_w2': w_w2, 'a0': a0, 'w_a2': w_a2, 'w_g2': w_g2,
            'k_k': k_k, 'k_a': k_a, 'r_k': r_k, 'gn_g': gn_g, 'gn_b': gn_b,
            'lam_re': lam_re, 'lam_im': lam_im, 'log_dt': log_dt, 'b_re': b_re, 'b_im': b_im,
            'c_re': c_re, 'c_im': c_im, 'd_skip': d_skip, 'w_glu_out': w_glu_out,
            'w_glu_gate': w_glu_gate}


def reference(x_prompt, x_sample, cache_k_sb, cache_v_sb, page_table, state_wkv, state_shift,
              state_s5, ln_g, ln_b, ffn1_wg, ffn1_wu, ffn1_wd, ffn2_wg, ffn2_wu, ffn2_wd,
              w_in_even, w_out_even, sb_bias, mu_shift, w0, w_w2, a0, w_a2, w_g2, k_k, k_a, r_k,
              gn_g, gn_b, lam_re, lam_im, log_dt, b_re, b_im, c_re, c_im, d_skip,
              w_glu_out, w_glu_gate):
    prm = dict(ln_g=ln_g, ln_b=ln_b, ffn1_wg=ffn1_wg, ffn1_wu=ffn1_wu, ffn1_wd=ffn1_wd,
               ffn2_wg=ffn2_wg, ffn2_wu=ffn2_wu, ffn2_wd=ffn2_wd, w_in_even=w_in_even,
               w_out_even=w_out_even, sb_bias=sb_bias, mu_shift=mu_shift, w0=w0, w_w2=w_w2,
               a0=a0, w_a2=w_a2,
               w_g2=w_g2, k_k=k_k, k_a=k_a, r_k=r_k, gn_g=gn_g, gn_b=gn_b, lam_re=lam_re,
               lam_im=lam_im, log_dt=log_dt, b_re=b_re, b_im=b_im, c_re=c_re, c_im=c_im,
               d_skip=d_skip, w_glu_out=w_glu_out, w_glu_gate=w_glu_gate)
    nb = x_prompt.shape[0]
    wkv0 = jnp.zeros((N_EVEN, nb, N_HEADS_RWKV, HEAD_DIM, HEAD_DIM), jnp.float32)
    shift0 = jnp.zeros((N_EVEN, nb, RWKV_COLS), x_prompt.dtype)
    s50 = jnp.zeros((N_ODD, nb, S5_GROUPS, S5_STATE, 2), jnp.float32)
    y_prompt, p_k, p_v, p_wkv, p_shift, p_s5 = trunk(
        x_prompt, None, None, None, wkv0, shift0, s50, prm)
    y_sample, s_k, s_v, s_wkv, s_shift, s_s5 = trunk(
        x_sample, cache_k_sb, cache_v_sb, page_table, state_wkv, state_shift, state_s5, prm)
    return (y_prompt, y_sample, p_k, p_v, p_wkv, p_shift, p_s5, s_k, s_v, s_wkv, s_shift, s_s5)
```

```python
import functools
import math

import jax
import jax.numpy as jnp
from jax import lax
from jax.experimental import pallas as pl
from jax.experimental.pallas import tpu as pltpu

F32 = jnp.float32
BF16 = jnp.bfloat16

HEAD_DIM = 64
LN_EPS = 1e-5
GN_EPS = 64e-5
S5_GROUP = 16
S5_STATE = 64
LANES = 128
SUBLANES = 8
VMEM_LIMIT_BYTES = 56 * 1024 * 1024

ROW_TILE = 512
FF_CHUNKS = 2
SB_TILE = 256
SB_PAGES_PER_STEP = 4
RWKV_CHUNK = 64
RWKV_SOLVE_BLOCK = 16
RWKV_PREP_TILE = 256
S5_TIME_CHUNK = 32
S5_SLAB = 1024


def _cparams(sem):
    return pltpu.CompilerParams(dimension_semantics=sem, vmem_limit_bytes=VMEM_LIMIT_BYTES)


def _const_spec(shape):
    nd = len(shape)
    return pl.BlockSpec(shape, lambda *_: (0,) * nd, pipeline_mode=pl.Buffered(1))


def _dot(a, b):
    return jnp.dot(a.astype(BF16), b.astype(BF16), preferred_element_type=F32)


def _dot_nt(a, b):
    return lax.dot_general(a.astype(BF16), b.astype(BF16), (((1,), (1,)), ((), ())),
                           preferred_element_type=F32)


def _dot_tn(a, b):
    return lax.dot_general(a.astype(BF16), b.astype(BF16), (((0,), (0,)), ((), ())),
                           preferred_element_type=F32)


def _split_bf16(a):
    hi = a.astype(BF16)
    lo = (a - hi.astype(F32)).astype(BF16)
    return hi, lo


def _dot_sel_rhs(a, sel):
    hi, lo = _split_bf16(a)
    return (jnp.dot(hi, sel, preferred_element_type=F32) + jnp.dot(lo, sel, preferred_element_type=F32))


def _dot_sel_lhs(sel, a):
    hi, lo = _split_bf16(a)
    return (jnp.dot(sel, hi, preferred_element_type=F32) + jnp.dot(sel, lo, preferred_element_type=F32))


def _layer_norm(y, g, b):
    mu = jnp.mean(y, axis=-1, keepdims=True)
    d = y - mu
    var = jnp.mean(d * d, axis=-1, keepdims=True)
    return d * lax.rsqrt(var + LN_EPS) * g + b


def _softplus(z):
    return jnp.maximum(z, 0.0) + jnp.log1p(jnp.exp(-jnp.abs(z)))


def _gelu_tanh(y):
    c = math.sqrt(2.0 / math.pi)
    return 0.5 * y * (1.0 + jnp.tanh(c * (y + 0.044715 * (y * y * y))))


def _stage_kernel(*refs, alpha, has_mix, proj_splits):
    it = iter(refs)
    x_ref = next(it)
    if has_mix:
        a_ref, b_ref, wa_ref, wb_ref, gm_ref, bm_ref = (next(it) for _ in range(6))
    wg_ref, wu_ref, wd_ref, g_ref, b2_ref = (next(it) for _ in range(5))
    if proj_splits:
        win_ref = next(it)
    o_ref = next(it)
    proj_refs = [next(it) for _ in proj_splits[1:]] if proj_splits else []

    x = x_ref[...]
    if has_mix:
        mix = _dot(a_ref[...], wa_ref[...]) + _dot(b_ref[...], wb_ref[...])
        x = _layer_norm(alpha * x + mix, gm_ref[...], bm_ref[...])
    xb = x.astype(BF16)
    ff = wg_ref.shape[1]
    fc = ff // FF_CHUNKS
    acc = None
    for c in range(0, ff, fc):
        gate = jnp.dot(xb, wg_ref[:, c:c + fc], preferred_element_type=F32)
        up = jnp.dot(xb, wu_ref[:, c:c + fc], preferred_element_type=F32)
        h = (gate * jax.nn.sigmoid(gate) * up).astype(BF16)
        part = jnp.dot(h, wd_ref[c:c + fc, :], preferred_element_type=F32)
        acc = part if acc is None else acc + part
    y = _layer_norm(alpha * x + 0.5 * acc, g_ref[...], b2_ref[...])
    o_ref[...] = y
    if proj_splits:
        proj = jnp.dot(y.astype(BF16), win_ref[...], preferred_element_type=F32)
        for r, lo, hi in zip(proj_refs, proj_splits[:-1], proj_splits[1:]):
            r[...] = proj[:, lo:hi]


def _stage(x, ffn, ln_ffn, alpha, mix=None, w_in=None, proj_splits=()):
    m, d = x.shape
    tm = min(ROW_TILE, m)
    assert m % tm == 0
    row = lambda w: pl.BlockSpec((tm, w), lambda i: (i, 0))
    args, specs = [x], [row(d)]
    if mix is not None:
        a, b, wa, wb, gm, bm = mix
        args += [a, b, wa, wb, gm, bm]
        specs += [row(a.shape[1]), row(b.shape[1]), _const_spec(wa.shape), _const_spec(wb.shape),
                  _const_spec(gm.shape), _const_spec(bm.shape)]
    args += [*ffn, *ln_ffn]
    specs += [_const_spec(w.shape) for w in (*ffn, *ln_ffn)]
    out_shape = [jax.ShapeDtypeStruct((m, d), F32)]
    out_specs = [row(d)]
    if proj_splits:
        args.append(w_in)
        specs.append(_const_spec(w_in.shape))
        for lo, hi in zip(proj_splits[:-1], proj_splits[1:]):
            out_shape.append(jax.ShapeDtypeStruct((m, hi - lo), F32))
            out_specs.append(row(hi - lo))
    kern = functools.partial(_stage_kernel, alpha=alpha, has_mix=mix is not None,
                             proj_splits=tuple(proj_splits))
    outs = pl.pallas_call(
        kern, grid=(m // tm,), in_specs=specs, out_specs=out_specs, out_shape=out_shape,
        compiler_params=_cparams(("parallel",)), name="stage")(*args)
    return outs


def _sb_prompt_kernel(bias_ref, q_ref, k_ref, v_ref, o_ref, *, tile, scale):
    hp = pl.program_id(1)
    qi = pl.program_id(2)
    q = q_ref[0] * scale
    lane = lax.broadcasted_iota(jnp.int32, q.shape, 1)
    rr = lax.broadcasted_iota(jnp.int32, (tile, tile), 0)
    cc = lax.broadcasted_iota(jnp.int32, (tile, tile), 1)
    after = (rr > cc).astype(BF16)
    causal = cc < rr

    outs = []
    for hh in range(2):
        in_head = (lane >= hh * HEAD_DIM) & (lane < (hh + 1) * HEAD_DIM)
        qh = jnp.where(in_head, q, 0.0).astype(BF16)
        bias = bias_ref[2 * hp + hh]

        def tile_step(kb, carry, masked):
            acc, run = carry
            start = pl.multiple_of(kb * tile, tile)
            ks = k_ref[0, pl.ds(start, tile), :].astype(BF16)
            z = lax.dot_general(qh, ks, (((1,), (1,)), ((), ())), preferred_element_type=F32) + bias
            sp = _softplus(z)
            l1m = jnp.where(causal, -sp, 0.0) if masked else -sp
            later = _dot_sel_rhs(l1m, after)
            w = jnp.exp(z - sp + later + run)
            if masked:
                w = jnp.where(causal, w, 0.0)
            vs = v_ref[0, pl.ds(start, tile), :].astype(BF16)
            acc = acc + jnp.dot(w.astype(BF16), vs, preferred_element_type=F32)
            run = run + later[:, 0:1] + l1m[:, 0:1]
            return acc, run

        carry = (jnp.zeros((tile, LANES), F32), jnp.zeros((tile, 1), F32))
        carry = tile_step(qi, carry, True)
        carry = lax.fori_loop(0, qi, lambda i, cr: tile_step(qi - 1 - i, cr, False), carry)
        outs.append(carry[0])
    o_ref[0] = jnp.where(lane < HEAD_DIM, outs[0], outs[1])


def _sb_prompt(q, k, v, bias):
    b, t, hd = q.shape
    tile = min(SB_TILE, t)
    assert t % tile == 0 and hd % LANES == 0
    kern = functools.partial(_sb_prompt_kernel, tile=tile, scale=HEAD_DIM ** -0.5)
    qspec = pl.BlockSpec((1, tile, LANES), lambda bi, hp, qi: (bi, qi, hp))
    kvspec = pl.BlockSpec((1, t, LANES), lambda bi, hp, qi: (bi, 0, hp))
    return pl.pallas_call(
        kern, grid=(b, hd // LANES, t // tile),
        in_specs=[pl.BlockSpec(memory_space=pltpu.SMEM), qspec, kvspec, kvspec],
        out_specs=qspec, out_shape=jax.ShapeDtypeStruct((b, t, hd), F32),
        compiler_params=_cparams(("parallel", "parallel", "arbitrary")), name="sb_prompt")(bias, q, k, v)


def _sb_sample_kernel(pt_ref, q_ref, bias_ref, *refs, pages, n_heads, scale):
    k_refs = refs[:pages]
    v_refs = refs[pages:2 * pages]
    o_ref = refs[2 * pages]
    acc_ref, run_ref = refs[2 * pages + 1:]
    g = pl.program_id(1)

    @pl.when(g == 0)
    def _():
        acc_ref[...] = jnp.zeros_like(acc_ref)
        run_ref[...] = jnp.zeros_like(run_ref)

    page = k_refs[0].shape[0]
    flat = page * n_heads
    qb = (q_ref[...] * scale).astype(BF16)
    sub = lax.broadcasted_iota(jnp.int32, (n_heads, flat), 0)
    lane = lax.broadcasted_iota(jnp.int32, (n_heads, flat), 1)
    own = (lane % n_heads) == sub
    lane1 = lax.broadcasted_iota(jnp.int32, (1, flat), 1)
    bias = bias_ref[...]

    acc = acc_ref[...]
    run = run_ref[...]
    for u in range(pages):
        k2 = k_refs[u][...].reshape(flat, HEAD_DIM).astype(BF16)
        zall = lax.dot_general(qb, k2, (((1,), (1,)), ((), ())), preferred_element_type=F32)
        z = jnp.sum(jnp.where(own, zall, 0.0), axis=0, keepdims=True) + bias
        sp = _softplus(z)
        l1m = -sp
        incl = l1m
        total = l1m
        s = n_heads
        while s < flat:
            shifted = pltpu.roll(incl, flat - s, axis=1)
            incl = incl + jnp.where(lane1 + s < flat, shifted, 0.0)
            total = total + pltpu.roll(total, s, axis=1)
            s *= 2
        w = jnp.exp(z - sp + (incl - l1m) + run)
        run = run + total
        wd = jnp.where(own, jnp.broadcast_to(w, (n_heads, flat)), 0.0).astype(BF16)
        v2 = v_refs[u][...].reshape(flat, HEAD_DIM).astype(BF16)
        acc = acc + jnp.dot(wd, v2, preferred_element_type=F32)
    acc_ref[...] = acc
    run_ref[...] = run

    @pl.when(g == pl.num_programs(1) - 1)
    def _():
        o_ref[...] = acc


def _sb_sample(q, cache_k, cache_v, page_table, bias):
    nb, n_heads, _ = q.shape
    n_pages = page_table.shape[1]
    page = cache_k.shape[1]
    pages = min(SB_PAGES_PER_STEP, n_pages)
    assert n_pages % pages == 0
    flat = page * n_heads
    bias_flat = jnp.tile(bias.astype(F32), page).reshape(1, flat)

    def page_spec(u):
        def imap(b, g, pt):
            return (pt[b, n_pages - 1 - (g * pages + u)], 0, 0, 0)
        return pl.BlockSpec((None, page, n_heads, HEAD_DIM), imap)

    qspec = pl.BlockSpec((None, n_heads, HEAD_DIM), lambda b, g, pt: (b, 0, 0))
    grid_spec = pltpu.PrefetchScalarGridSpec(
        num_scalar_prefetch=1, grid=(nb, n_pages // pages),
        in_specs=[qspec, pl.BlockSpec((1, flat), lambda b, g, pt: (0, 0))]
                 + [page_spec(u) for u in range(pages)] * 2,
        out_specs=qspec,
        scratch_shapes=[pltpu.VMEM((n_heads, HEAD_DIM), F32), pltpu.VMEM((1, flat), F32)])
    kern = functools.partial(_sb_sample_kernel, pages=pages, n_heads=n_heads, scale=HEAD_DIM ** -0.5)
    return pl.pallas_call(
        kern, grid_spec=grid_spec, out_shape=jax.ShapeDtypeStruct((nb, n_heads, HEAD_DIM), F32),
        compiler_params=_cparams(("parallel", "arbitrary")), name="sb_sample")(
            page_table, q, bias_flat, *([cache_k] * pages), *([cache_v] * pages))


def _rwkv_prep_kernel(pb_ref, sh_ref, mu_ref, w0_ref, a0_ref, kk_ref, ka_ref, ww2_ref, wa2_ref, wg2_ref,
                      ones_ref, r_o, k_o, v_o, lw_o, kk_o, bb_o, g_o, carry_ref, *, shifted, d_rwkv):
    pb = pb_ref[0]
    if shifted:
        @pl.when(pl.program_id(1) == 0)
        def _():
            carry_ref[...] = sh_ref[0]
        rows = pb.shape[0]
        rolled = pltpu.roll(pb, 1, axis=0)
        row = lax.broadcasted_iota(jnp.int32, pb.shape, 0)
        prev = jnp.where(row == 0, carry_ref[...], rolled)
        carry_ref[...] = pb[rows - 1:rows, :]
    else:
        prev = sh_ref[0]
    pm = pb + mu_ref[...] * (prev - pb)
    d = d_rwkv
    r = pm[:, 0:d]
    kr = pm[:, d:2 * d]
    vr = pm[:, 2 * d:3 * d]
    wad = pm[:, 3 * d:3 * d + LANES]
    gd = pm[:, 3 * d + LANES:]
    z_w = w0_ref[...] + _dot(jnp.tanh(wad), ww2_ref[...])
    log_decay = -math.exp(-0.5) * jax.nn.sigmoid(z_w)
    iclr = jax.nn.sigmoid(a0_ref[...] + _dot(wad, wa2_ref[...]))
    gate = _dot(jax.nn.sigmoid(gd), wg2_ref[...])
    kkr = kr * kk_ref[...]
    ss = _dot_sel_rhs(kkr * kkr, ones_ref[...])
    kk = kkr * lax.rsqrt(jnp.maximum(ss, 1e-24))
    kf = kr * (1.0 + (iclr - 1.0) * ka_ref[...])
    r_o[0] = r
    k_o[0] = kf
    v_o[0] = vr
    lw_o[0] = log_decay
    kk_o[0] = kk
    bb_o[0] = kk * iclr
    g_o[0] = gate


def _rwkv_prep(pb, shift, p, shifted):
    b, t, c = pb.shape
    d = p["w0"].shape[1]
    tt = min(RWKV_PREP_TILE, t)
    assert t % tt == 0
    blk = lambda w: pl.BlockSpec((1, tt, w), lambda bi, ti: (bi, ti, 0))
    sh_spec = pl.BlockSpec((1, 1, c), lambda bi, ti: (bi, 0, 0)) if shifted else blk(c)
    consts = [p["mu"], p["w0"], p["a0"], p["k_k"], p["k_a"], p["w_w2"], p["w_a2"], p["w_g2"], p["ones"]]
    kern = functools.partial(_rwkv_prep_kernel, shifted=shifted, d_rwkv=d)
    return pl.pallas_call(
        kern, grid=(b, t // tt),
        in_specs=[blk(c), sh_spec] + [_const_spec(a.shape) for a in consts],
        out_specs=[blk(d)] * 7, out_shape=[jax.ShapeDtypeStruct((b, t, d), F32)] * 7,
        scratch_shapes=[pltpu.VMEM((1, c), F32)],
        compiler_params=_cparams(("parallel", "arbitrary")), name="rwkv_prep")(pb, shift, *consts)


def _rwkv_chunk_kernel(r_ref, k_ref, v_ref, lw_ref, kk_ref, bb_ref, g_ref, s0_ref, rk_ref, gng_ref,
                       gnb_ref, tri_ref, ones_ref, avg_ref, o_ref, sout_ref, st_ref, *, chunk, n_heads):
    L = chunk
    c = pl.program_id(1)

    @pl.when(c == 0)
    def _():
        st_ref[...] = s0_ref[0]

    r = r_ref[0]
    kf = k_ref[0]
    v = v_ref[0]
    lw = lw_ref[0]
    kk = kk_ref[0]
    bb = bb_ref[0]
    cs = _dot_sel_lhs(tri_ref[...], lw)
    cs_end = cs[L - 1:L, :]
    p_in = jnp.exp(cs)
    p_inv = jnp.exp(-cs)
    p_end = jnp.exp(cs_end - cs)
    rt = r * p_in
    kt = kk * jnp.exp(cs - lw)
    bt = bb * p_inv
    kft = kf * p_inv
    kh = kf * p_end
    bh = bb * p_end
    p_last = jnp.exp(cs_end)

    row = lax.broadcasted_iota(jnp.int32, (2 * L, 2 * L), 0)
    col = lax.broadcasted_iota(jnp.int32, (2 * L, 2 * L), 1)
    t_idx = row & (L - 1)
    s_idx = col & (L - 1)
    tril = s_idx < t_idx + row // L
    row1 = lax.broadcasted_iota(jnp.int32, (L, L), 0)
    col1 = lax.broadcasted_iota(jnp.int32, (L, L), 1)
    same_blk = (row1 // RWKV_SOLVE_BLOCK) == (col1 // RWKV_SOLVE_BLOCK)

    ys = []
    for h in range(n_heads):
        sl = slice(h * HEAD_DIM, (h + 1) * HEAD_DIM)
        x = jnp.concatenate([kt[:, sl], rt[:, sl]], axis=0)
        yk = jnp.concatenate([bt[:, sl], kft[:, sl]], axis=0)
        gm = jnp.where(tril, _dot_nt(x, yk), 0.0)
        s0 = st_ref[h]
        xm = _dot_nt(x, s0)
        vh = v[:, sl]
        a = gm[:L, :L]
        rhs = -(xm[:L] + _dot(gm[:L, L:], vh))
        ad = jnp.where(same_blk, a, 0.0)
        z = jnp.concatenate([a - ad, rhs], axis=1)
        z = z - _dot(ad, z)
        apow = ad
        span = 2
        while span < RWKV_SOLVE_BLOCK:
            apow = _dot(apow, apow)
            z = z + _dot(apow, z)
            span *= 2
        n = z[:, :L]
        w2 = _dot(n, z)
        x1 = z[:, L:] - w2[:, L:]
        u = x1 + _dot(w2[:, :L], x1)
        uv = jnp.concatenate([u, vh], axis=0)
        ys.append(xm[L:] + _dot(gm[L:], uv))
        bk = jnp.concatenate([bh[:, sl], kh[:, sl]], axis=0)
        st_ref[h] = s0 * p_last[:, sl] + _dot_tn(uv, bk)
    y = jnp.concatenate(ys, axis=1)
    mean = _dot_sel_rhs(y, avg_ref[...])
    d = y - mean
    var = _dot_sel_rhs(d * d, avg_ref[...])
    yn = d * lax.rsqrt(var + GN_EPS) * gng_ref[...] + gnb_ref[...]
    bonus = _dot_sel_rhs(r * kf * rk_ref[...], ones_ref[...]) * v
    o_ref[0] = (yn + bonus) * g_ref[0]

    @pl.when(c == pl.num_programs(1) - 1)
    def _():
        sout_ref[0] = st_ref[...]


def _rwkv_chunked(prep, s0, p):
    r = prep[0]
    b, t, d = r.shape
    n_heads = d // HEAD_DIM
    L = RWKV_CHUNK
    assert t % L == 0 and L // RWKV_SOLVE_BLOCK == 4
    blk = pl.BlockSpec((1, L, d), lambda bi, ci: (bi, ci, 0))
    sspec = pl.BlockSpec((1, n_heads, HEAD_DIM, HEAD_DIM), lambda bi, ci: (bi, 0, 0, 0))
    tri = jnp.tril(jnp.ones((L, L), F32)).astype(BF16)
    consts = [p["r_k"], p["gn_g"], p["gn_b"], tri, p["ones"], p["avg"]]
    kern = functools.partial(_rwkv_chunk_kernel, chunk=L, n_heads=n_heads)
    return pl.pallas_call(
        kern, grid=(b, t // L),
        in_specs=[blk] * 7 + [sspec] + [_const_spec(a.shape) for a in consts],
        out_specs=[blk, sspec],
        out_shape=[jax.ShapeDtypeStruct((b, t, d), F32),
                   jax.ShapeDtypeStruct((b, n_heads, HEAD_DIM, HEAD_DIM), F32)],
        scratch_shapes=[pltpu.VMEM((n_heads, HEAD_DIM, HEAD_DIM), F32)],
        compiler_params=_cparams(("parallel", "arbitrary")), name="rwkv_chunk")(*prep, s0, *consts)


def _rwkv_step_kernel(s_ref, r_ref, k_ref, v_ref, lw_ref, kk_ref, bb_ref, g_ref, rk_ref, gng_ref, gnb_ref,
                      so_ref, o_ref):
    r = r_ref[0]
    k = k_ref[0]
    v = v_ref[0]
    w = jnp.exp(lw_ref[0])
    kk = kk_ref[0]
    bb = bb_ref[0]
    lane = lax.broadcasted_iota(jnp.int32, r.shape, 1)
    y = jnp.zeros_like(r)
    for i in range(HEAD_DIM):
        s = s_ref[0, i]
        sa = jnp.sum(s * kk, axis=1, keepdims=True)
        s2 = s * w - sa * bb + v[:, i:i + 1] * k
        so_ref[0, i] = s2
        y = jnp.where(lane == i, jnp.sum(s2 * r, axis=1, keepdims=True), y)
    mu = jnp.mean(y, axis=1, keepdims=True)
    d = y - mu
    var = jnp.mean(d * d, axis=1, keepdims=True)
    yn = d * lax.rsqrt(var + GN_EPS) * gng_ref[0] + gnb_ref[0]
    bonus = jnp.sum(r * k * rk_ref[0], axis=1, keepdims=True) * v
    o_ref[0] = (yn + bonus) * g_ref[0]


def _rwkv_step(prep, state, p):
    nb, d = prep[0].shape
    n_heads = d // HEAD_DIM
    to_heads = lambda a: a.reshape(-1, n_heads, HEAD_DIM).transpose(1, 0, 2)
    vecs = [to_heads(a) for a in prep]
    params = [to_heads(p[n]) for n in ("r_k", "gn_g", "gn_b")]
    st = state.transpose(1, 2, 0, 3)
    sspec = pl.BlockSpec((1, HEAD_DIM, nb, HEAD_DIM), lambda h: (h, 0, 0, 0))
    vspec = pl.BlockSpec((1, nb, HEAD_DIM), lambda h: (h, 0, 0))
    pspec = pl.BlockSpec((1, 1, HEAD_DIM), lambda h: (h, 0, 0))
    st_new, o = pl.pallas_call(
        _rwkv_step_kernel, grid=(n_heads,),
        in_specs=[sspec] + [vspec] * 7 + [pspec] * 3, out_specs=[sspec, vspec],
        out_shape=[jax.ShapeDtypeStruct(st.shape, F32), jax.ShapeDtypeStruct((n_heads, nb, HEAD_DIM), F32)],
        compiler_params=_cparams(("parallel",)), name="rwkv_step")(st, *vecs, *params)
    return o.transpose(1, 0, 2).reshape(nb, d), st_new.transpose(2, 0, 1, 3)


def _s5_disc_kernel(lr_ref, li_ref, ldt_ref, ar_ref, ai_ref, cr_ref, ci_ref):
    lr = lr_ref[...]
    li = li_ref[...]
    dt = jnp.exp(ldt_ref[...])
    mag = jnp.exp(lr * dt)
    ar = mag * jnp.cos(li * dt)
    ai = mag * jnp.sin(li * dt)
    den = lr * lr + li * li
    ar_ref[...] = ar
    ai_ref[...] = ai
    cr_ref[...] = ((ar - 1.0) * lr + ai * li) / den
    ci_ref[...] = (ai * lr - (ar - 1.0) * li) / den


def _s5_discretize(lam_re, lam_im, log_dt):
    g, p = lam_re.shape
    ldt = jnp.broadcast_to(log_dt.reshape(g, 1), (g, p))
    return pl.pallas_call(_s5_disc_kernel, out_shape=[jax.ShapeDtypeStruct((g, p), F32)] * 4,
                          name="s5_disc")(lam_re, lam_im, ldt)


def _s5_mix_tail(x, y_state, dsk_ref, wo_ref, wgt_ref, g_ref, b_ref, alpha):
    y = y_state + dsk_ref[...] * x
    zg = _gelu_tanh(y).astype(BF16)
    out = (jnp.dot(zg, wo_ref[...], preferred_element_type=F32)
           * jax.nn.sigmoid(jnp.dot(zg, wgt_ref[...], preferred_element_type=F32)))
    return _layer_norm(alpha * x + out, g_ref[...], b_ref[...])


def _s5_scan_kernel(x_ref, wb_ref, wcr_ref, wci_ref, ar_ref, ai_ref, dsk_ref, wo_ref, wgt_ref, g_ref, b_ref,
                    s0r_ref, s0i_ref, o_ref, sr_out, si_out, bur, bui, sr_st, si_st, *, tc, nb, alpha):
    ti = pl.program_id(0)

    @pl.when(ti == 0)
    def _():
        sr_st[...] = s0r_ref[...]
        si_st[...] = s0i_ref[...]

    x = x_ref[...]
    xb = x.astype(BF16)
    n_blk = wb_ref.shape[0]
    half = wb_ref.shape[2] // 2
    for j in range(n_blk):
        pj = jnp.dot(xb[:, j * LANES:(j + 1) * LANES], wb_ref[j], preferred_element_type=F32)
        bur[:, j * half:(j + 1) * half] = pj[:, :half]
        bui[:, j * half:(j + 1) * half] = pj[:, half:]
    n_state = bur.shape[1]
    for s in range(n_state // S5_SLAB):
        cols = slice(s * S5_SLAB, (s + 1) * S5_SLAB)
        ar = jnp.broadcast_to(ar_ref[:, cols], (nb, S5_SLAB))
        ai = jnp.broadcast_to(ai_ref[:, cols], (nb, S5_SLAB))

        def step(t, carry):
            sr, si = carry
            rows = pl.ds(pl.multiple_of(t * nb, nb), nb)
            nsr = ar * sr - ai * si + bur[rows, cols]
            nsi = ar * si + ai * sr + bui[rows, cols]
            bur[rows, cols] = nsr
            bui[rows, cols] = nsi
            return nsr, nsi

        sr, si = lax.fori_loop(0, tc, step, (sr_st[:, cols], si_st[:, cols]))
        sr_st[:, cols] = sr
        si_st[:, cols] = si
    ys = []
    for j in range(n_blk):
        ys.append(_dot(bur[:, j * half:(j + 1) * half], wcr_ref[j])
                  + _dot(bui[:, j * half:(j + 1) * half], wci_ref[j]))
    y_state = jnp.concatenate(ys, axis=1)
    o_ref[...] = _s5_mix_tail(x, y_state, dsk_ref, wo_ref, wgt_ref, g_ref, b_ref, alpha)

    @pl.when(ti == pl.num_programs(0) - 1)
    def _():
        sr_out[...] = sr_st[...]
        si_out[...] = si_st[...]


def _s5_scan(x_tb, s0r, s0i, sp, ln, alpha, nb):
    m, d = x_tb.shape
    t = m // nb
    tc = min(S5_TIME_CHUNK, t)
    assert t % tc == 0 and nb == SUBLANES
    n_state = s0r.shape[1]
    consts = [sp["wb"], sp["wcr"], sp["wci"], sp["ar"], sp["ai"], sp["d_skip"], sp["w_out"], sp["w_gate"],
              ln[0], ln[1], s0r, s0i]
    row = pl.BlockSpec((tc * nb, d), lambda i: (i, 0))
    st_spec = pl.BlockSpec((nb, n_state), lambda i: (0, 0))
    kern = functools.partial(_s5_scan_kernel, tc=tc, nb=nb, alpha=alpha)
    return pl.pallas_call(
        kern, grid=(t // tc,),
        in_specs=[row] + [_const_spec(a.shape) for a in consts],
        out_specs=[row, st_spec, st_spec],
        out_shape=[jax.ShapeDtypeStruct((m, d), F32), jax.ShapeDtypeStruct((nb, n_state), F32),
                   jax.ShapeDtypeStruct((nb, n_state), F32)],
        scratch_shapes=[pltpu.VMEM((tc * nb, n_state), F32), pltpu.VMEM((tc * nb, n_state), F32),
                        pltpu.VMEM((nb, n_state), F32), pltpu.VMEM((nb, n_state), F32)],
        compiler_params=_cparams(("arbitrary",)), name="s5_scan")(x_tb, *consts)


def _s5_step_kernel(x_ref, wb_ref, wcr_ref, wci_ref, ar_ref, ai_ref, dsk_ref, wo_ref, wgt_ref, g_ref, b_ref,
                    s0r_ref, s0i_ref, o_ref, sr_out, si_out, *, alpha):
    x = x_ref[...]
    xb = x.astype(BF16)
    n_blk = wb_ref.shape[0]
    half = wb_ref.shape[2] // 2
    ys = []
    for j in range(n_blk):
        cols = slice(j * half, (j + 1) * half)
        pj = jnp.dot(xb[:, j * LANES:(j + 1) * LANES], wb_ref[j], preferred_element_type=F32)
        ar = ar_ref[:, cols]
        ai = ai_ref[:, cols]
        sr = s0r_ref[:, cols]
        si = s0i_ref[:, cols]
        nsr = ar * sr - ai * si + pj[:, :half]
        nsi = ar * si + ai * sr + pj[:, half:]
        sr_out[:, cols] = nsr
        si_out[:, cols] = nsi
        ys.append(_dot(nsr, wcr_ref[j]) + _dot(nsi, wci_ref[j]))
    y_state = jnp.concatenate(ys, axis=1)
    o_ref[...] = _s5_mix_tail(x, y_state, dsk_ref, wo_ref, wgt_ref, g_ref, b_ref, alpha)


def _s5_step(x, s0r, s0i, sp, ln, alpha):
    m, d = x.shape
    n_state = s0r.shape[1]
    args = [x, sp["wb"], sp["wcr"], sp["wci"], sp["ar"], sp["ai"], sp["d_skip"], sp["w_out"], sp["w_gate"],
            ln[0], ln[1], s0r, s0i]
    kern = functools.partial(_s5_step_kernel, alpha=alpha)
    return pl.pallas_call(
        kern,
        out_shape=[jax.ShapeDtypeStruct((m, d), F32), jax.ShapeDtypeStruct((m, n_state), F32),
                   jax.ShapeDtypeStruct((m, n_state), F32)],
        compiler_params=pltpu.CompilerParams(vmem_limit_bytes=VMEM_LIMIT_BYTES), name="s5_step")(*args)


def _s5_weights(prm, i):
    ar, ai, cr, ci = _s5_discretize(prm["lam_re"][i], prm["lam_im"][i], prm["log_dt"][i])
    b_re, b_im = prm["b_re"][i], prm["b_im"][i]
    bbr = cr[..., None] * b_re - ci[..., None] * b_im
    bbi = cr[..., None] * b_im + ci[..., None] * b_re
    n_groups, n_state, n_ch = bbr.shape
    gpb = LANES // n_ch
    n_blk = n_groups // gpb
    eye = jnp.eye(gpb, dtype=F32)

    def blk_in(bb):
        w = jnp.einsum("jgpc,gh->jgchp", bb.reshape(n_blk, gpb, n_state, n_ch), eye)
        return w.reshape(n_blk, LANES, gpb * n_state)

    def blk_out(cc):
        w = jnp.einsum("jgcp,gh->jhpgc", cc.reshape(n_blk, gpb, n_ch, n_state), eye)
        return w.reshape(n_blk, gpb * n_state, LANES)

    return dict(
        wb=jnp.concatenate([blk_in(bbr), blk_in(bbi)], axis=-1).astype(BF16),
        wcr=blk_out(prm["c_re"][i]).astype(BF16), wci=(-blk_out(prm["c_im"][i])).astype(BF16),
        ar=ar.reshape(1, -1), ai=ai.reshape(1, -1), d_skip=prm["d_skip"][i].reshape(1, -1),
        w_out=prm["w_glu_out"][i].astype(BF16), w_gate=prm["w_glu_gate"][i].astype(BF16))


def _rwkv_weights(prm, i, d_rwkv):
    rank_d = prm["w_w2"].shape[1]
    rank_a = prm["w_a2"].shape[1]
    assert rank_d + rank_a == LANES
    zeros_d = jnp.zeros((rank_d, d_rwkv), F32)
    zeros_a = jnp.zeros((rank_a, d_rwkv), F32)
    head = jnp.arange(d_rwkv) // HEAD_DIM
    ones = (head[:, None] == head[None, :]).astype(F32)
    row = lambda a: a.reshape(1, -1).astype(F32)
    return dict(
        mu=row(prm["mu_shift"][i]), w0=row(prm["w0"][i]), a0=row(prm["a0"][i]), k_k=row(prm["k_k"][i]),
        k_a=row(prm["k_a"][i]), r_k=row(prm["r_k"][i]), gn_g=row(prm["gn_g"][i]), gn_b=row(prm["gn_b"][i]),
        w_w2=jnp.concatenate([prm["w_w2"][i], zeros_a], axis=0).astype(BF16),
        w_a2=jnp.concatenate([zeros_d, prm["w_a2"][i]], axis=0).astype(BF16),
        w_g2=prm["w_g2"][i].astype(BF16), ones=ones.astype(BF16), avg=(ones / HEAD_DIM).astype(BF16))


def _trunk(x, cache_k, cache_v, page_table, wkv0, shift0, s50, prm, w):
    bsz, t, d = x.shape
    depth = prm["ln_g"].shape[0]
    alpha = (2.0 * depth) ** 0.25
    d_sb = prm["sb_bias"].shape[1] * HEAD_DIM
    d_rwkv = prm["w0"].shape[1]
    splits = (0, d_sb, 2 * d_sb, 3 * d_sb, prm["w_in_even"].shape[2])
    ln = lambda layer, j: (prm["ln_g"][layer, j].reshape(1, d), prm["ln_b"][layer, j].reshape(1, d))
    prompt = cache_k is None
    time_major = False
    rows = x.reshape(bsz * t, d)
    out_k, out_v, out_wkv, out_shift, out_s5 = [], [], [], [], []
    pending_mix = None
    for layer in range(depth):
        i = layer // 2
        ffn1 = (w["ffn1_wg"][layer], w["ffn1_wu"][layer], w["ffn1_wd"][layer])
        ffn2 = (w["ffn2_wg"][layer], w["ffn2_wu"][layer], w["ffn2_wd"][layer])
        if layer % 2 == 0:
            if time_major:
                rows = rows.reshape(t, bsz, d).transpose(1, 0, 2).reshape(bsz * t, d)
                time_major = False
            x1, q, k, v, pb = _stage(rows, ffn1, ln(layer, 0), alpha, w_in=w["w_in_even"][i],
                                     proj_splits=splits)
            rp = _rwkv_weights(prm, i, d_rwkv)
            pb3 = pb.reshape(bsz, t, -1)
            if prompt:
                o_sb = _sb_prompt(q.reshape(bsz, t, d_sb), k.reshape(bsz, t, d_sb), v.reshape(bsz, t, d_sb),
                                  prm["sb_bias"][i]).reshape(bsz * t, d_sb)
                prep = _rwkv_prep(pb3, shift0[i][:, None, :], rp, shifted=True)
                o_rwkv, wkv = _rwkv_chunked(prep, wkv0[i], rp)
                o_rwkv = o_rwkv.reshape(bsz * t, d_rwkv)
            else:
                assert t == 1
                n_heads = d_sb // HEAD_DIM
                o_sb = _sb_sample(q.reshape(bsz, n_heads, HEAD_DIM), cache_k[i], cache_v[i], page_table,
                                  prm["sb_bias"][i]).reshape(bsz, d_sb)
                prep = _rwkv_prep(pb[None], shift0[i][None], rp, shifted=False)
                o_rwkv, wkv = _rwkv_step([a[0] for a in prep], wkv0[i], rp)
            out_k.append(k.reshape(bsz, t, -1, HEAD_DIM))
            out_v.append(v.reshape(bsz, t, -1, HEAD_DIM))
            out_wkv.append(wkv)
            out_shift.append(pb3[:, -1])
            w_out = w["w_out_even"][i]
            mix = (o_sb, o_rwkv, w_out[:d_sb], w_out[d_sb:], *ln(layer, 1))
            (rows,) = _stage(x1, ffn2, ln(layer, 2), alpha, mix=mix)
        else:
            sp = _s5_weights(prm, i)
            s0r = s50[i][..., 0].reshape(bsz, -1)
            s0i = s50[i][..., 1].reshape(bsz, -1)
            if t > 1 and not time_major:
                rows = rows.reshape(bsz, t, d).transpose(1, 0, 2).reshape(bsz * t, d)
                time_major = True
            (x1,) = _stage(rows, ffn1, ln(layer, 0), alpha)
            if t > 1:
                x2, sr, si = _s5_scan(x1, s0r, s0i, sp, ln(layer, 1), alpha, bsz)
            else:
                x2, sr, si = _s5_step(x1, s0r, s0i, sp, ln(layer, 1), alpha)
            n_groups = prm["lam_re"].shape[1]
            out_s5.append(jnp.stack([sr.reshape(bsz, n_groups, -1), si.reshape(bsz, n_groups, -1)], axis=-1))
            (rows,) = _stage(x2, ffn2, ln(layer, 2), alpha)
    if time_major:
        rows = rows.reshape(t, bsz, d).transpose(1, 0, 2)
    y = rows.reshape(bsz, t, d)
    return (y, jnp.stack(out_k), jnp.stack(out_v), jnp.stack(out_wkv), jnp.stack(out_shift),
            jnp.stack(out_s5))


def kernel(x_prompt, x_sample, cache_k_sb, cache_v_sb, page_table, state_wkv, state_shift, state_s5, ln_g, ln_b, ffn1_wg, ffn1_wu, ffn1_wd, ffn2_wg, ffn2_wu, ffn2_wd, w_in_even, w_out_even, sb_bias, mu_shift, w0, w_w2, a0, w_a2, w_g2, k_k, k_a, r_k, gn_g, gn_b, lam_re, lam_im, log_dt, b_re, b_im, c_re, c_im, d_skip, w_glu_out, w_glu_gate):
    prm = dict(ln_g=ln_g, ln_b=ln_b, sb_bias=sb_bias, mu_shift=mu_shift, w0=w0, w_w2=w_w2, a0=a0, w_a2=w_a2,
               w_g2=w_g2, k_k=k_k, k_a=k_a, r_k=r_k.reshape(r_k.shape[0], -1), gn_g=gn_g, gn_b=gn_b,
               lam_re=lam_re, lam_im=lam_im, log_dt=log_dt, b_re=b_re, b_im=b_im, c_re=c_re, c_im=c_im,
               d_skip=d_skip, w_glu_out=w_glu_out, w_glu_gate=w_glu_gate, w_in_even=w_in_even)
    w = dict(ffn1_wg=ffn1_wg, ffn1_wu=ffn1_wu, ffn1_wd=ffn1_wd, ffn2_wg=ffn2_wg, ffn2_wu=ffn2_wu,
             ffn2_wd=ffn2_wd, w_in_even=w_in_even, w_out_even=w_out_even)
    w = {name: a.astype(BF16) for name, a in w.items()}
    nb = x_prompt.shape[0]
    n_even, n_odd = state_wkv.shape[0], state_s5.shape[0]
    wkv0 = jnp.zeros((n_even, nb) + state_wkv.shape[2:], F32)
    shift0 = jnp.zeros((n_even, nb, state_shift.shape[2]), F32)
    s50 = jnp.zeros((n_odd, nb) + state_s5.shape[2:], F32)
    y_p, p_k, p_v, p_wkv, p_shift, p_s5 = _trunk(x_prompt, None, None, None, wkv0, shift0, s50, prm, w)
    y_s, s_k, s_v, s_wkv, s_shift, s_s5 = _trunk(x_sample, cache_k_sb, cache_v_sb, page_table, state_wkv,
                                                 state_shift, state_s5, prm, w)
    return (y_p, y_s, p_k, p_v, p_wkv, p_shift, p_s5, s_k, s_v, s_wkv, s_shift, s_s5)
```

```python
import functools
import math

import jax
import jax.numpy as jnp
from jax import lax
from jax.experimental import pallas as pl
from jax.experimental.pallas import tpu as pltpu

F32 = jnp.float32
BF16 = jnp.bfloat16

HEAD_DIM = 64
LN_EPS = 1e-5
GN_EPS = 64e-5
S5_GROUP = 16
S5_STATE = 64
LANES = 128
SUBLANES = 8
VMEM_LIMIT_BYTES = 56 * 1024 * 1024

ROW_TILE = 512
FF_CHUNKS = 2
SB_TILE = 256
SB_PAGES_PER_STEP = 8
RWKV_CHUNK = 64
RWKV_SOLVE_BLOCK = 16
RWKV_PREP_TILE = 256
S5_TIME_CHUNK = 32
S5_SLAB = 1024


def _cparams(sem):
    return pltpu.CompilerParams(dimension_semantics=sem, vmem_limit_bytes=VMEM_LIMIT_BYTES)


def _const_spec(shape):
    nd = len(shape)
    return pl.BlockSpec(shape, lambda *_: (0,) * nd, pipeline_mode=pl.Buffered(1))


def _dot(a, b):
    return jnp.dot(a.astype(BF16), b.astype(BF16), preferred_element_type=F32)


def _dot_nt(a, b):
    return lax.dot_general(a.astype(BF16), b.astype(BF16), (((1,), (1,)), ((), ())),
                           preferred_element_type=F32)


def _dot_tn(a, b):
    return lax.dot_general(a.astype(BF16), b.astype(BF16), (((0,), (0,)), ((), ())),
                           preferred_element_type=F32)


def _split_bf16(a):
    hi = a.astype(BF16)
    lo = (a - hi.astype(F32)).astype(BF16)
    return hi, lo


def _dot_sel_rhs(a, sel):
    hi, lo = _split_bf16(a)
    return (jnp.dot(hi, sel, preferred_element_type=F32) + jnp.dot(lo, sel, preferred_element_type=F32))


def _dot_sel_lhs(sel, a):
    hi, lo = _split_bf16(a)
    return (jnp.dot(sel, hi, preferred_element_type=F32) + jnp.dot(sel, lo, preferred_element_type=F32))


def _layer_norm(y, g, b):
    mu = jnp.mean(y, axis=-1, keepdims=True)
    d = y - mu
    var = jnp.mean(d * d, axis=-1, keepdims=True)
    return d * lax.rsqrt(var + LN_EPS) * g + b


def _softplus(z):
    return jnp.maximum(z, 0.0) + jnp.log1p(jnp.exp(-jnp.abs(z)))


def _gelu_tanh(y):
    c = math.sqrt(2.0 / math.pi)
    return 0.5 * y * (1.0 + jnp.tanh(c * (y + 0.044715 * (y * y * y))))


def _stage_kernel(*refs, alpha, has_mix, proj_splits):
    it = iter(refs)
    x_ref = next(it)
    if has_mix:
        a_ref, b_ref, wa_ref, wb_ref, gm_ref, bm_ref = (next(it) for _ in range(6))
    wg_ref, wu_ref, wd_ref, g_ref, b2_ref = (next(it) for _ in range(5))
    if proj_splits:
        win_ref = next(it)
    o_ref = next(it)
    proj_refs = [next(it) for _ in proj_splits[1:]] if proj_splits else []

    x = x_ref[...]
    if has_mix:
        mix = _dot(a_ref[...], wa_ref[...]) + _dot(b_ref[...], wb_ref[...])
        x = _layer_norm(alpha * x + mix, gm_ref[...], bm_ref[...])
    xb = x.astype(BF16)
    ff = wg_ref.shape[1]
    fc = ff // FF_CHUNKS
    acc = None
    for c in range(0, ff, fc):
        gate = jnp.dot(xb, wg_ref[:, c:c + fc], preferred_element_type=F32)
        up = jnp.dot(xb, wu_ref[:, c:c + fc], preferred_element_type=F32)
        h = (gate * jax.nn.sigmoid(gate) * up).astype(BF16)
        part = jnp.dot(h, wd_ref[c:c + fc, :], preferred_element_type=F32)
        acc = part if acc is None else acc + part
    y = _layer_norm(alpha * x + 0.5 * acc, g_ref[...], b2_ref[...])
    o_ref[...] = y
    if proj_splits:
        proj = jnp.dot(y.astype(BF16), win_ref[...], preferred_element_type=F32)
        for r, lo, hi in zip(proj_refs, proj_splits[:-1], proj_splits[1:]):
            r[...] = proj[:, lo:hi]


def _stage(x, ffn, ln_ffn, alpha, mix=None, w_in=None, proj_splits=()):
    m, d = x.shape
    tm = min(ROW_TILE, m)
    assert m % tm == 0
    row = lambda w: pl.BlockSpec((tm, w), lambda i: (i, 0))
    args, specs = [x], [row(d)]
    if mix is not None:
        a, b, wa, wb, gm, bm = mix
        args += [a, b, wa, wb, gm, bm]
        specs += [row(a.shape[1]), row(b.shape[1]), _const_spec(wa.shape), _const_spec(wb.shape),
                  _const_spec(gm.shape), _const_spec(bm.shape)]
    args += [*ffn, *ln_ffn]
    specs += [_const_spec(w.shape) for w in (*ffn, *ln_ffn)]
    out_shape = [jax.ShapeDtypeStruct((m, d), F32)]
    out_specs = [row(d)]
    if proj_splits:
        args.append(w_in)
        specs.append(_const_spec(w_in.shape))
        for lo, hi in zip(proj_splits[:-1], proj_splits[1:]):
            out_shape.append(jax.ShapeDtypeStruct((m, hi - lo), F32))
            out_specs.append(row(hi - lo))
    kern = functools.partial(_stage_kernel, alpha=alpha, has_mix=mix is not None,
                             proj_splits=tuple(proj_splits))
    outs = pl.pallas_call(
        kern, grid=(m // tm,), in_specs=specs, out_specs=out_specs, out_shape=out_shape,
        compiler_params=_cparams(("parallel",)), name="stage")(*args)
    return outs


def _sb_prompt_kernel(bias_ref, q_ref, k_ref, v_ref, o_ref, *, tile, scale):
    hp = pl.program_id(1)
    qi = pl.program_id(2)
    q = q_ref[0] * scale
    lane = lax.broadcasted_iota(jnp.int32, q.shape, 1)
    rr = lax.broadcasted_iota(jnp.int32, (tile, tile), 0)
    cc = lax.broadcasted_iota(jnp.int32, (tile, tile), 1)
    after = (rr > cc).astype(BF16)
    causal = cc < rr
    heads = range(LANES // HEAD_DIM)
    qh = [jnp.where((lane >= h * HEAD_DIM) & (lane < (h + 1) * HEAD_DIM), q, 0.0).astype(BF16) for h in heads]
    bias = [bias_ref[2 * hp + h] for h in heads]

    def tile_step(kb, carry, masked):
        start = pl.multiple_of(kb * tile, tile)
        ks = k_ref[0, pl.ds(start, tile), :].astype(BF16)
        vs = v_ref[0, pl.ds(start, tile), :].astype(BF16)
        out = []
        for h in heads:
            acc, run = carry[2 * h], carry[2 * h + 1]
            z = lax.dot_general(qh[h], ks, (((1,), (1,)), ((), ())), preferred_element_type=F32) + bias[h]
            sp = _softplus(z)
            l1m = jnp.where(causal, -sp, 0.0) if masked else -sp
            later = _dot_sel_rhs(l1m, after)
            w = jnp.exp(z - sp + later + run)
            if masked:
                w = jnp.where(causal, w, 0.0)
            acc = acc + jnp.dot(w.astype(BF16), vs, preferred_element_type=F32)
            run = run + later[:, 0:1] + l1m[:, 0:1]
            out += [acc, run]
        return tuple(out)

    carry = (jnp.zeros((tile, LANES), F32), jnp.zeros((tile, 1), F32)) * len(heads)
    carry = tile_step(qi, carry, True)
    carry = lax.fori_loop(0, qi, lambda i, cr: tile_step(qi - 1 - i, cr, False), carry)
    o_ref[0] = jnp.where(lane < HEAD_DIM, carry[0], carry[2])


def _sb_prompt(q, k, v, bias):
    b, t, hd = q.shape
    tile = min(SB_TILE, t)
    assert t % tile == 0 and hd % LANES == 0
    kern = functools.partial(_sb_prompt_kernel, tile=tile, scale=HEAD_DIM ** -0.5)
    qspec = pl.BlockSpec((1, tile, LANES), lambda bi, hp, qi: (bi, qi, hp))
    kvspec = pl.BlockSpec((1, t, LANES), lambda bi, hp, qi: (bi, 0, hp))
    return pl.pallas_call(
        kern, grid=(b, hd // LANES, t // tile),
        in_specs=[pl.BlockSpec(memory_space=pltpu.SMEM), qspec, kvspec, kvspec],
        out_specs=qspec, out_shape=jax.ShapeDtypeStruct((b, t, hd), F32),
        compiler_params=_cparams(("parallel", "parallel", "arbitrary")), name="sb_prompt")(bias, q, k, v)


def _sb_sample_kernel(pt_ref, q_ref, bias_ref, after_ref, *refs, pages, n_heads, scale):
    k_refs = refs[:pages]
    v_refs = refs[pages:2 * pages]
    o_ref = refs[2 * pages]
    qt_ref, acc_ref, run_ref = refs[2 * pages + 1:]
    g = pl.program_id(1)
    page = k_refs[0].shape[2]

    @pl.when(g == 0)
    def _():
        q = q_ref[...] * scale
        sub = lax.broadcasted_iota(jnp.int32, q.shape, 0)
        ones = jnp.ones((n_heads, page), BF16)
        for h in range(n_heads):
            qt_ref[h] = _dot_tn(jnp.where(sub == h, q, 0.0), ones)
        acc_ref[...] = jnp.zeros_like(acc_ref)
        run_ref[...] = jnp.zeros_like(run_ref)

    bias = bias_ref[...]
    after = after_ref[...]
    run = run_ref[...]
    for u in range(pages):
        rows = [jnp.sum(k_refs[u][h] * qt_ref[h], axis=0, keepdims=True) for h in range(n_heads)]
        z = jnp.concatenate(rows, axis=0) + bias
        sp = _softplus(z)
        later = _dot_sel_rhs(-sp, after)
        w = jnp.exp(z - sp + later + run)
        run = run - jnp.sum(sp, axis=1, keepdims=True)
        for h in range(n_heads):
            acc_ref[h] += v_refs[u][h] * w[h:h + 1, :]
    run_ref[...] = run

    @pl.when(g == pl.num_programs(1) - 1)
    def _():
        ones = jnp.ones((SUBLANES, page), BF16)
        rows = []
        for h in range(n_heads):
            hi, lo = _split_bf16(acc_ref[h])
            rows.append((_dot_nt(ones, hi) + _dot_nt(ones, lo))[0:1])
        o_ref[...] = jnp.concatenate(rows, axis=0)


def _sb_sample(q, cache_k, cache_v, page_table, bias):
    nb, n_heads, _ = q.shape
    n_pages = page_table.shape[1]
    page = cache_k.shape[3]
    pages = min(SB_PAGES_PER_STEP, n_pages)
    assert n_pages % pages == 0
    bias_b = jnp.broadcast_to(bias.astype(F32)[:, None], (n_heads, page))
    key = jnp.arange(page)
    after = (key[:, None] > key[None, :]).astype(BF16)

    def page_spec(u):
        def imap(b, g, pt):
            return (pt[b, n_pages - 1 - (g * pages + u)], 0, 0, 0)
        return pl.BlockSpec((None, n_heads, HEAD_DIM, page), imap)

    qspec = pl.BlockSpec((None, n_heads, HEAD_DIM), lambda b, g, pt: (b, 0, 0))
    const = lambda shape: pl.BlockSpec(shape, lambda b, g, pt: (0, 0))
    grid_spec = pltpu.PrefetchScalarGridSpec(
        num_scalar_prefetch=1, grid=(nb, n_pages // pages),
        in_specs=[qspec, const((n_heads, page)), const((page, page))] + [page_spec(u) for u in range(pages)] * 2,
        out_specs=qspec,
        scratch_shapes=[pltpu.VMEM((n_heads, HEAD_DIM, page), F32), pltpu.VMEM((n_heads, HEAD_DIM, page), F32),
                        pltpu.VMEM((n_heads, page), F32)])
    kern = functools.partial(_sb_sample_kernel, pages=pages, n_heads=n_heads, scale=HEAD_DIM ** -0.5)
    return pl.pallas_call(
        kern, grid_spec=grid_spec, out_shape=jax.ShapeDtypeStruct((nb, n_heads, HEAD_DIM), F32),
        compiler_params=_cparams(("parallel", "arbitrary")), name="sb_sample")(
            page_table, q, bias_b, after, *([cache_k] * pages), *([cache_v] * pages))


def _rwkv_prep_kernel(pb_ref, sh_ref, mu_ref, w0_ref, a0_ref, kk_ref, ka_ref, ww2_ref, wa2_ref, wg2_ref,
                      ones_ref, r_o, k_o, v_o, lw_o, kk_o, bb_o, g_o, carry_ref, *, shifted, d_rwkv):
    pb = pb_ref[0]
    if shifted:
        @pl.when(pl.program_id(1) == 0)
        def _():
            carry_ref[...] = sh_ref[0]
        rows = pb.shape[0]
        rolled = pltpu.roll(pb, 1, axis=0)
        row = lax.broadcasted_iota(jnp.int32, pb.shape, 0)
        prev = jnp.where(row == 0, carry_ref[...], rolled)
        carry_ref[...] = pb[rows - 1:rows, :]
    else:
        prev = sh_ref[0]
    pm = pb + mu_ref[...] * (prev - pb)
    d = d_rwkv
    r = pm[:, 0:d]
    kr = pm[:, d:2 * d]
    vr = pm[:, 2 * d:3 * d]
    wad = pm[:, 3 * d:3 * d + LANES]
    gd = pm[:, 3 * d + LANES:]
    z_w = w0_ref[...] + _dot(jnp.tanh(wad), ww2_ref[...])
    log_decay = -math.exp(-0.5) * jax.nn.sigmoid(z_w)
    iclr = jax.nn.sigmoid(a0_ref[...] + _dot(wad, wa2_ref[...]))
    gate = _dot(jax.nn.sigmoid(gd), wg2_ref[...])
    kkr = kr * kk_ref[...]
    ss = _dot_sel_rhs(kkr * kkr, ones_ref[...])
    kk = kkr * lax.rsqrt(jnp.maximum(ss, 1e-24))
    kf = kr * (1.0 + (iclr - 1.0) * ka_ref[...])
    r_o[0] = r
    k_o[0] = kf
    v_o[0] = vr
    lw_o[0] = log_decay
    kk_o[0] = kk
    bb_o[0] = kk * iclr
    g_o[0] = gate


def _rwkv_prep(pb, shift, p, shifted):
    b, t, c = pb.shape
    d = p["w0"].shape[1]
    tt = min(RWKV_PREP_TILE, t)
    assert t % tt == 0
    blk = lambda w: pl.BlockSpec((1, tt, w), lambda bi, ti: (bi, ti, 0))
    sh_spec = pl.BlockSpec((1, 1, c), lambda bi, ti: (bi, 0, 0)) if shifted else blk(c)
    consts = [p["mu"], p["w0"], p["a0"], p["k_k"], p["k_a"], p["w_w2"], p["w_a2"], p["w_g2"], p["ones"]]
    kern = functools.partial(_rwkv_prep_kernel, shifted=shifted, d_rwkv=d)
    return pl.pallas_call(
        kern, grid=(b, t // tt),
        in_specs=[blk(c), sh_spec] + [_const_spec(a.shape) for a in consts],
        out_specs=[blk(d)] * 7, out_shape=[jax.ShapeDtypeStruct((b, t, d), F32)] * 7,
        scratch_shapes=[pltpu.VMEM((1, c), F32)],
        compiler_params=_cparams(("parallel", "arbitrary")), name="rwkv_prep")(pb, shift, *consts)


def _rwkv_chunk_kernel(r_ref, k_ref, v_ref, lw_ref, kk_ref, bb_ref, g_ref, s0_ref, rk_ref, gng_ref,
                       gnb_ref, tri_ref, ones_ref, avg_ref, o_ref, sout_ref, st_ref, *, chunk, n_heads):
    L = chunk
    c = pl.program_id(1)

    @pl.when(c == 0)
    def _():
        st_ref[...] = s0_ref[0]

    r = r_ref[0]
    kf = k_ref[0]
    v = v_ref[0]
    lw = lw_ref[0]
    kk = kk_ref[0]
    bb = bb_ref[0]
    cs = _dot_sel_lhs(tri_ref[...], lw)
    cs_end = cs[L - 1:L, :]
    p_in = jnp.exp(cs)
    p_inv = jnp.exp(-cs)
    p_end = jnp.exp(cs_end - cs)
    rt = r * p_in
    kt = kk * jnp.exp(cs - lw)
    bt = bb * p_inv
    kft = kf * p_inv
    kh = kf * p_end
    bh = bb * p_end
    p_last = jnp.exp(cs_end)

    row = lax.broadcasted_iota(jnp.int32, (2 * L, 2 * L), 0)
    col = lax.broadcasted_iota(jnp.int32, (2 * L, 2 * L), 1)
    t_idx = row & (L - 1)
    s_idx = col & (L - 1)
    tril = s_idx < t_idx + row // L
    row1 = lax.broadcasted_iota(jnp.int32, (L, L), 0)
    col1 = lax.broadcasted_iota(jnp.int32, (L, L), 1)
    same_blk = (row1 // RWKV_SOLVE_BLOCK) == (col1 // RWKV_SOLVE_BLOCK)

    hs = range(n_heads)
    sl = [slice(h * HEAD_DIM, (h + 1) * HEAD_DIM) for h in hs]
    s0 = [st_ref[h] for h in hs]
    vh = [v[:, sl[h]] for h in hs]
    x = [jnp.concatenate([kt[:, sl[h]], rt[:, sl[h]]], axis=0) for h in hs]
    yk = [jnp.concatenate([bt[:, sl[h]], kft[:, sl[h]]], axis=0) for h in hs]
    gm = [jnp.where(tril, _dot_nt(x[h], yk[h]), 0.0) for h in hs]
    xm = [_dot_nt(x[h], s0[h]) for h in hs]
    rhs = [-(xm[h][:L] + _dot(gm[h][:L, L:], vh[h])) for h in hs]
    ad = [jnp.where(same_blk, gm[h][:L, :L], 0.0) for h in hs]
    z = [jnp.concatenate([gm[h][:L, :L] - ad[h], rhs[h]], axis=1) for h in hs]
    z = [z[h] - _dot(ad[h], z[h]) for h in hs]
    apow = ad
    span = 2
    while span < RWKV_SOLVE_BLOCK:
        apow = [_dot(apow[h], apow[h]) for h in hs]
        z = [z[h] + _dot(apow[h], z[h]) for h in hs]
        span *= 2
    w2 = [_dot(z[h][:, :L], z[h]) for h in hs]
    x1 = [z[h][:, L:] - w2[h][:, L:] for h in hs]
    u = [x1[h] + _dot(w2[h][:, :L], x1[h]) for h in hs]
    uv = [jnp.concatenate([u[h], vh[h]], axis=0) for h in hs]
    ys = [xm[h][L:] + _dot(gm[h][L:], uv[h]) for h in hs]
    for h in hs:
        bk = jnp.concatenate([bh[:, sl[h]], kh[:, sl[h]]], axis=0)
        st_ref[h] = s0[h] * p_last[:, sl[h]] + _dot_tn(uv[h], bk)
    y = jnp.concatenate(ys, axis=1)
    mean = _dot_sel_rhs(y, avg_ref[...])
    d = y - mean
    var = _dot_sel_rhs(d * d, avg_ref[...])
    yn = d * lax.rsqrt(var + GN_EPS) * gng_ref[...] + gnb_ref[...]
    bonus = _dot_sel_rhs(r * kf * rk_ref[...], ones_ref[...]) * v
    o_ref[0] = (yn + bonus) * g_ref[0]

    @pl.when(c == pl.num_programs(1) - 1)
    def _():
        sout_ref[0] = st_ref[...]


def _rwkv_chunked(prep, s0, p):
    r = prep[0]
    b, t, d = r.shape
    n_heads = d // HEAD_DIM
    L = RWKV_CHUNK
    assert t % L == 0 and L // RWKV_SOLVE_BLOCK == 4
    blk = pl.BlockSpec((1, L, d), lambda bi, ci: (bi, ci, 0))
    sspec = pl.BlockSpec((1, n_heads, HEAD_DIM, HEAD_DIM), lambda bi, ci: (bi, 0, 0, 0))
    tri = jnp.tril(jnp.ones((L, L), F32)).astype(BF16)
    consts = [p["r_k"], p["gn_g"], p["gn_b"], tri, p["ones"], p["avg"]]
    kern = functools.partial(_rwkv_chunk_kernel, chunk=L, n_heads=n_heads)
    return pl.pallas_call(
        kern, grid=(b, t // L),
        in_specs=[blk] * 7 + [sspec] + [_const_spec(a.shape) for a in consts],
        out_specs=[blk, sspec],
        out_shape=[jax.ShapeDtypeStruct((b, t, d), F32),
                   jax.ShapeDtypeStruct((b, n_heads, HEAD_DIM, HEAD_DIM), F32)],
        scratch_shapes=[pltpu.VMEM((n_heads, HEAD_DIM, HEAD_DIM), F32)],
        compiler_params=_cparams(("parallel", "arbitrary")), name="rwkv_chunk")(*prep, s0, *consts)


def _rwkv_step_kernel(s_ref, r_ref, k_ref, v_ref, lw_ref, kk_ref, bb_ref, g_ref, rk_ref, gng_ref, gnb_ref,
                      so_ref, o_ref, y_ref):
    r = r_ref[0]
    k = k_ref[0]
    v = v_ref[0]
    w = jnp.exp(lw_ref[0])
    kk = kk_ref[0]
    bb = bb_ref[0]
    for i in range(HEAD_DIM):
        s = s_ref[0, i]
        sa = jnp.sum(s * kk, axis=0, keepdims=True)
        s2 = s * w - sa * bb + v[i:i + 1, :] * k
        so_ref[0, i] = s2
        y_ref[i:i + 1, :] = jnp.sum(s2 * r, axis=0, keepdims=True)
    y = y_ref[...]
    mu = jnp.mean(y, axis=0, keepdims=True)
    d = y - mu
    var = jnp.mean(d * d, axis=0, keepdims=True)
    yn = d * lax.rsqrt(var + GN_EPS) * gng_ref[0] + gnb_ref[0]
    bonus = jnp.sum(r * k * rk_ref[0], axis=0, keepdims=True) * v
    o_ref[0] = (yn + bonus) * g_ref[0]


def _rwkv_step(prep, state, p):
    nb, d = prep[0].shape
    n_heads = d // HEAD_DIM
    to_heads = lambda a: a.reshape(nb, n_heads, HEAD_DIM).transpose(1, 2, 0)
    vecs = [to_heads(a) for a in prep]
    params = [jnp.broadcast_to(p[n].reshape(n_heads, HEAD_DIM, 1), (n_heads, HEAD_DIM, nb))
              for n in ("r_k", "gn_g", "gn_b")]
    st = state.transpose(1, 2, 3, 0)
    sspec = pl.BlockSpec((1, HEAD_DIM, HEAD_DIM, nb), lambda h: (h, 0, 0, 0))
    vspec = pl.BlockSpec((1, HEAD_DIM, nb), lambda h: (h, 0, 0))
    st_new, o = pl.pallas_call(
        _rwkv_step_kernel, grid=(n_heads,),
        in_specs=[sspec] + [vspec] * 10, out_specs=[sspec, vspec],
        out_shape=[jax.ShapeDtypeStruct(st.shape, F32), jax.ShapeDtypeStruct((n_heads, HEAD_DIM, nb), F32)],
        scratch_shapes=[pltpu.VMEM((HEAD_DIM, nb), F32)],
        compiler_params=_cparams(("parallel",)), name="rwkv_step")(st, *vecs, *params)
    return o.transpose(2, 0, 1).reshape(nb, d), st_new.transpose(3, 0, 1, 2)


def _s5_disc_kernel(lr_ref, li_ref, ldt_ref, ar_ref, ai_ref, cr_ref, ci_ref):
    lr = lr_ref[...]
    li = li_ref[...]
    dt = jnp.exp(ldt_ref[...])
    mag = jnp.exp(lr * dt)
    ar = mag * jnp.cos(li * dt)
    ai = mag * jnp.sin(li * dt)
    den = lr * lr + li * li
    ar_ref[...] = ar
    ai_ref[...] = ai
    cr_ref[...] = ((ar - 1.0) * lr + ai * li) / den
    ci_ref[...] = (ai * lr - (ar - 1.0) * li) / den


def _s5_discretize(lam_re, lam_im, log_dt):
    g, p = lam_re.shape
    ldt = jnp.broadcast_to(log_dt.reshape(g, 1), (g, p))
    return pl.pallas_call(_s5_disc_kernel, out_shape=[jax.ShapeDtypeStruct((g, p), F32)] * 4,
                          name="s5_disc")(lam_re, lam_im, ldt)


def _s5_mix_tail(x, y_state, dsk_ref, wo_ref, wgt_ref, g_ref, b_ref, alpha):
    y = y_state + dsk_ref[...] * x
    zg = _gelu_tanh(y).astype(BF16)
    out = (jnp.dot(zg, wo_ref[...], preferred_element_type=F32)
           * jax.nn.sigmoid(jnp.dot(zg, wgt_ref[...], preferred_element_type=F32)))
    return _layer_norm(alpha * x + out, g_ref[...], b_ref[...])


def _s5_scan_kernel(x_ref, wb_ref, wcr_ref, wci_ref, ar_ref, ai_ref, dsk_ref, wo_ref, wgt_ref, g_ref, b_ref,
                    s0r_ref, s0i_ref, o_ref, sr_out, si_out, bur, bui, sr_st, si_st, *, tc, nb, alpha):
    ti = pl.program_id(0)

    @pl.when(ti == 0)
    def _():
        sr_st[...] = s0r_ref[...]
        si_st[...] = s0i_ref[...]

    x = x_ref[...]
    xb = x.astype(BF16)
    n_blk = wb_ref.shape[0]
    half = wb_ref.shape[2] // 2
    for j in range(n_blk):
        pj = jnp.dot(xb[:, j * LANES:(j + 1) * LANES], wb_ref[j], preferred_element_type=F32)
        bur[:, j * half:(j + 1) * half] = pj[:, :half]
        bui[:, j * half:(j + 1) * half] = pj[:, half:]
    n_state = bur.shape[1]
    for s in range(n_state // S5_SLAB):
        cols = slice(s * S5_SLAB, (s + 1) * S5_SLAB)
        ar = jnp.broadcast_to(ar_ref[:, cols], (nb, S5_SLAB))
        ai = jnp.broadcast_to(ai_ref[:, cols], (nb, S5_SLAB))

        def step(t, carry):
            sr, si = carry
            rows = pl.ds(pl.multiple_of(t * nb, nb), nb)
            nsr = ar * sr - ai * si + bur[rows, cols]
            nsi = ar * si + ai * sr + bui[rows, cols]
            bur[rows, cols] = nsr
            bui[rows, cols] = nsi
            return nsr, nsi

        sr, si = lax.fori_loop(0, tc, step, (sr_st[:, cols], si_st[:, cols]))
        sr_st[:, cols] = sr
        si_st[:, cols] = si
    ys = []
    for j in range(n_blk):
        ys.append(_dot(bur[:, j * half:(j + 1) * half], wcr_ref[j])
                  + _dot(bui[:, j * half:(j + 1) * half], wci_ref[j]))
    y_state = jnp.concatenate(ys, axis=1)
    o_ref[...] = _s5_mix_tail(x, y_state, dsk_ref, wo_ref, wgt_ref, g_ref, b_ref, alpha)

    @pl.when(ti == pl.num_programs(0) - 1)
    def _():
        sr_out[...] = sr_st[...]
        si_out[...] = si_st[...]


def _s5_scan(x_tb, s0r, s0i, sp, ln, alpha, nb):
    m, d = x_tb.shape
    t = m // nb
    tc = min(S5_TIME_CHUNK, t)
    assert t % tc == 0 and nb == SUBLANES
    n_state = s0r.shape[1]
    consts = [sp["wb"], sp["wcr"], sp["wci"], sp["ar"], sp["ai"], sp["d_skip"], sp["w_out"], sp["w_gate"],
              ln[0], ln[1], s0r, s0i]
    row = pl.BlockSpec((tc * nb, d), lambda i: (i, 0))
    st_spec = pl.BlockSpec((nb, n_state), lambda i: (0, 0))
    kern = functools.partial(_s5_scan_kernel, tc=tc, nb=nb, alpha=alpha)
    return pl.pallas_call(
        kern, grid=(t // tc,),
        in_specs=[row] + [_const_spec(a.shape) for a in consts],
        out_specs=[row, st_spec, st_spec],
        out_shape=[jax.ShapeDtypeStruct((m, d), F32), jax.ShapeDtypeStruct((nb, n_state), F32),
                   jax.ShapeDtypeStruct((nb, n_state), F32)],
        scratch_shapes=[pltpu.VMEM((tc * nb, n_state), F32), pltpu.VMEM((tc * nb, n_state), F32),
                        pltpu.VMEM((nb, n_state), F32), pltpu.VMEM((nb, n_state), F32)],
        compiler_params=_cparams(("arbitrary",)), name="s5_scan")(x_tb, *consts)


def _s5_step_kernel(x_ref, wb_ref, wcr_ref, wci_ref, ar_ref, ai_ref, dsk_ref, wo_ref, wgt_ref, g_ref, b_ref,
                    s0r_ref, s0i_ref, o_ref, sr_out, si_out, *, alpha):
    x = x_ref[...]
    xb = x.astype(BF16)
    n_blk = wb_ref.shape[0]
    half = wb_ref.shape[2] // 2
    ys = []
    for j in range(n_blk):
        cols = slice(j * half, (j + 1) * half)
        pj = jnp.dot(xb[:, j * LANES:(j + 1) * LANES], wb_ref[j], preferred_element_type=F32)
        ar = ar_ref[:, cols]
        ai = ai_ref[:, cols]
        sr = s0r_ref[:, cols]
        si = s0i_ref[:, cols]
        nsr = ar * sr - ai * si + pj[:, :half]
        nsi = ar * si + ai * sr + pj[:, half:]
        sr_out[:, cols] = nsr
        si_out[:, cols] = nsi
        ys.append(_dot(nsr, wcr_ref[j]) + _dot(nsi, wci_ref[j]))
    y_state = jnp.concatenate(ys, axis=1)
    o_ref[...] = _s5_mix_tail(x, y_state, dsk_ref, wo_ref, wgt_ref, g_ref, b_ref, alpha)


def _s5_step(x, s0r, s0i, sp, ln, alpha):
    m, d = x.shape
    n_state = s0r.shape[1]
    args = [x, sp["wb"], sp["wcr"], sp["wci"], sp["ar"], sp["ai"], sp["d_skip"], sp["w_out"], sp["w_gate"],
            ln[0], ln[1], s0r, s0i]
    kern = functools.partial(_s5_step_kernel, alpha=alpha)
    return pl.pallas_call(
        kern,
        out_shape=[jax.ShapeDtypeStruct((m, d), F32), jax.ShapeDtypeStruct((m, n_state), F32),
                   jax.ShapeDtypeStruct((m, n_state), F32)],
        compiler_params=pltpu.CompilerParams(vmem_limit_bytes=VMEM_LIMIT_BYTES), name="s5_step")(*args)


def _s5_weights(prm, i):
    ar, ai, cr, ci = _s5_discretize(prm["lam_re"][i], prm["lam_im"][i], prm["log_dt"][i])
    b_re, b_im = prm["b_re"][i], prm["b_im"][i]
    bbr = cr[..., None] * b_re - ci[..., None] * b_im
    bbi = cr[..., None] * b_im + ci[..., None] * b_re
    n_groups, n_state, n_ch = bbr.shape
    gpb = LANES // n_ch
    n_blk = n_groups // gpb
    eye = jnp.eye(gpb, dtype=F32)

    def blk_in(bb):
        w = jnp.einsum("jgpc,gh->jgchp", bb.reshape(n_blk, gpb, n_state, n_ch), eye)
        return w.reshape(n_blk, LANES, gpb * n_state)

    def blk_out(cc):
        w = jnp.einsum("jgcp,gh->jhpgc", cc.reshape(n_blk, gpb, n_ch, n_state), eye)
        return w.reshape(n_blk, gpb * n_state, LANES)

    return dict(
        wb=jnp.concatenate([blk_in(bbr), blk_in(bbi)], axis=-1).astype(BF16),
        wcr=blk_out(prm["c_re"][i]).astype(BF16), wci=(-blk_out(prm["c_im"][i])).astype(BF16),
        ar=ar.reshape(1, -1), ai=ai.reshape(1, -1), d_skip=prm["d_skip"][i].reshape(1, -1),
        w_out=prm["w_glu_out"][i].astype(BF16), w_gate=prm["w_glu_gate"][i].astype(BF16))


def _rwkv_weights(prm, i, d_rwkv):
    rank_d = prm["w_w2"].shape[1]
    rank_a = prm["w_a2"].shape[1]
    assert rank_d + rank_a == LANES
    zeros_d = jnp.zeros((rank_d, d_rwkv), F32)
    zeros_a = jnp.zeros((rank_a, d_rwkv), F32)
    head = jnp.arange(d_rwkv) // HEAD_DIM
    ones = (head[:, None] == head[None, :]).astype(F32)
    row = lambda a: a.reshape(1, -1).astype(F32)
    return dict(
        mu=row(prm["mu_shift"][i]), w0=row(prm["w0"][i]), a0=row(prm["a0"][i]), k_k=row(prm["k_k"][i]),
        k_a=row(prm["k_a"][i]), r_k=row(prm["r_k"][i]), gn_g=row(prm["gn_g"][i]), gn_b=row(prm["gn_b"][i]),
        w_w2=jnp.concatenate([prm["w_w2"][i], zeros_a], axis=0).astype(BF16),
        w_a2=jnp.concatenate([zeros_d, prm["w_a2"][i]], axis=0).astype(BF16),
        w_g2=prm["w_g2"][i].astype(BF16), ones=ones.astype(BF16), avg=(ones / HEAD_DIM).astype(BF16))


def _trunk(x, cache_k, cache_v, page_table, wkv0, shift0, s50, prm, w):
    bsz, t, d = x.shape
    depth = prm["ln_g"].shape[0]
    alpha = (2.0 * depth) ** 0.25
    d_sb = prm["sb_bias"].shape[1] * HEAD_DIM
    d_rwkv = prm["w0"].shape[1]
    splits = (0, d_sb, 2 * d_sb, 3 * d_sb, prm["w_in_even"].shape[2])
    ln = lambda layer, j: (prm["ln_g"][layer, j].reshape(1, d), prm["ln_b"][layer, j].reshape(1, d))
    prompt = cache_k is None
    time_major = False
    rows = x.reshape(bsz * t, d)
    out_k, out_v, out_wkv, out_shift, out_s5 = [], [], [], [], []
    pending_mix = None
    for layer in range(depth):
        i = layer // 2
        ffn1 = (w["ffn1_wg"][layer], w["ffn1_wu"][layer], w["ffn1_wd"][layer])
        ffn2 = (w["ffn2_wg"][layer], w["ffn2_wu"][layer], w["ffn2_wd"][layer])
        if layer % 2 == 0:
            if time_major:
                rows = rows.reshape(t, bsz, d).transpose(1, 0, 2).reshape(bsz * t, d)
                time_major = False
            x1, q, k, v, pb = _stage(rows, ffn1, ln(layer, 0), alpha, w_in=w["w_in_even"][i],
                                     proj_splits=splits)
            rp = _rwkv_weights(prm, i, d_rwkv)
            pb3 = pb.reshape(bsz, t, -1)
            if prompt:
                o_sb = _sb_prompt(q.reshape(bsz, t, d_sb), k.reshape(bsz, t, d_sb), v.reshape(bsz, t, d_sb),
                                  prm["sb_bias"][i]).reshape(bsz * t, d_sb)
                prep = _rwkv_prep(pb3, shift0[i][:, None, :], rp, shifted=True)
                o_rwkv, wkv = _rwkv_chunked(prep, wkv0[i], rp)
                o_rwkv = o_rwkv.reshape(bsz * t, d_rwkv)
            else:
                assert t == 1
                n_heads = d_sb // HEAD_DIM
                o_sb = _sb_sample(q.reshape(bsz, n_heads, HEAD_DIM), cache_k[i], cache_v[i], page_table,
                                  prm["sb_bias"][i]).reshape(bsz, d_sb)
                prep = _rwkv_prep(pb[None], shift0[i][None], rp, shifted=False)
                o_rwkv, wkv = _rwkv_step([a[0] for a in prep], wkv0[i], rp)
            out_k.append(k.reshape(bsz, t, -1, HEAD_DIM))
            out_v.append(v.reshape(bsz, t, -1, HEAD_DIM))
            out_wkv.append(wkv)
            out_shift.append(pb3[:, -1])
            w_out = w["w_out_even"][i]
            mix = (o_sb, o_rwkv, w_out[:d_sb], w_out[d_sb:], *ln(layer, 1))
            (rows,) = _stage(x1, ffn2, ln(layer, 2), alpha, mix=mix)
        else:
            sp = _s5_weights(prm, i)
            s0r = s50[i][..., 0].reshape(bsz, -1)
            s0i = s50[i][..., 1].reshape(bsz, -1)
            if t > 1 and not time_major:
                rows = rows.reshape(bsz, t, d).transpose(1, 0, 2).reshape(bsz * t, d)
                time_major = True
            (x1,) = _stage(rows, ffn1, ln(layer, 0), alpha)
            if t > 1:
                x2, sr, si = _s5_scan(x1, s0r, s0i, sp, ln(layer, 1), alpha, bsz)
            else:
                x2, sr, si = _s5_step(x1, s0r, s0i, sp, ln(layer, 1), alpha)
            n_groups = prm["lam_re"].shape[1]
            out_s5.append(jnp.stack([sr.reshape(bsz, n_groups, -1), si.reshape(bsz, n_groups, -1)], axis=-1))
            (rows,) = _stage(x2, ffn2, ln(layer, 2), alpha)
    if time_major:
        rows = rows.reshape(t, bsz, d).transpose(1, 0, 2)
    y = rows.reshape(bsz, t, d)
    return (y, jnp.stack(out_k), jnp.stack(out_v), jnp.stack(out_wkv), jnp.stack(out_shift),
            jnp.stack(out_s5))


def kernel(x_prompt, x_sample, cache_k_sb, cache_v_sb, page_table, state_wkv, state_shift, state_s5, ln_g, ln_b, ffn1_wg, ffn1_wu, ffn1_wd, ffn2_wg, ffn2_wu, ffn2_wd, w_in_even, w_out_even, sb_bias, mu_shift, w0, w_w2, a0, w_a2, w_g2, k_k, k_a, r_k, gn_g, gn_b, lam_re, lam_im, log_dt, b_re, b_im, c_re, c_im, d_skip, w_glu_out, w_glu_gate):
    prm = dict(ln_g=ln_g, ln_b=ln_b, sb_bias=sb_bias, mu_shift=mu_shift, w0=w0, w_w2=w_w2, a0=a0, w_a2=w_a2,
               w_g2=w_g2, k_k=k_k, k_a=k_a, r_k=r_k.reshape(r_k.shape[0], -1), gn_g=gn_g, gn_b=gn_b,
               lam_re=lam_re, lam_im=lam_im, log_dt=log_dt, b_re=b_re, b_im=b_im, c_re=c_re, c_im=c_im,
               d_skip=d_skip, w_glu_out=w_glu_out, w_glu_gate=w_glu_gate, w_in_even=w_in_even)
    w = dict(ffn1_wg=ffn1_wg, ffn1_wu=ffn1_wu, ffn1_wd=ffn1_wd, ffn2_wg=ffn2_wg, ffn2_wu=ffn2_wu,
             ffn2_wd=ffn2_wd, w_in_even=w_in_even, w_out_even=w_out_even)
    w = {name: a.astype(BF16) for name, a in w.items()}
    nb = x_prompt.shape[0]
    n_even, n_odd = state_wkv.shape[0], state_s5.shape[0]
    wkv0 = jnp.zeros((n_even, nb) + state_wkv.shape[2:], F32)
    shift0 = jnp.zeros((n_even, nb, state_shift.shape[2]), F32)
    s50 = jnp.zeros((n_odd, nb) + state_s5.shape[2:], F32)
    y_p, p_k, p_v, p_wkv, p_shift, p_s5 = _trunk(x_prompt, None, None, None, wkv0, shift0, s50, prm, w)
    cache_kt = cache_k_sb.transpose(0, 1, 3, 4, 2)
    cache_vt = cache_v_sb.transpose(0, 1, 3, 4, 2)
    y_s, s_k, s_v, s_wkv, s_shift, s_s5 = _trunk(x_sample, cache_kt, cache_vt, page_table, state_wkv,
                                                 state_shift, state_s5, prm, w)
    return (y_p, y_s, p_k, p_v, p_wkv, p_shift, p_s5, s_k, s_v, s_wkv, s_shift, s_s5)
```

```python
import functools
import math

import jax
import jax.numpy as jnp
from jax import lax
from jax.experimental import pallas as pl
from jax.experimental.pallas import tpu as pltpu

F32 = jnp.float32
BF16 = jnp.bfloat16

HEAD_DIM = 64
LOG2_E = 1.0 / math.log(2.0)
LN_EPS = 1e-5
GN_EPS = 64e-5
S5_GROUP = 16
S5_STATE = 64
LANES = 128
SUBLANES = 8
VMEM_LIMIT_BYTES = 56 * 1024 * 1024

ROW_TILE = 512
FF_CHUNKS = 2
SB_TILE_Q = 512
SB_TILE_K = 256
SB_PAGES_PER_STEP = 8
RWKV_CHUNK = 64
RWKV_SOLVE_BLOCK = 16
S5_TIME_CHUNK = 32
S5_SLAB = 1024


def _cparams(sem):
    return pltpu.CompilerParams(dimension_semantics=sem, vmem_limit_bytes=VMEM_LIMIT_BYTES)


def _const_spec(shape):
    nd = len(shape)
    return pl.BlockSpec(shape, lambda *_: (0,) * nd, pipeline_mode=pl.Buffered(1))


def _dot(a, b):
    return jnp.dot(a.astype(BF16), b.astype(BF16), preferred_element_type=F32)


def _dot_nt(a, b):
    return lax.dot_general(a.astype(BF16), b.astype(BF16), (((1,), (1,)), ((), ())),
                           preferred_element_type=F32)


def _dot_tn(a, b):
    return lax.dot_general(a.astype(BF16), b.astype(BF16), (((0,), (0,)), ((), ())),
                           preferred_element_type=F32)


def _split_bf16(a):
    hi = a.astype(BF16)
    lo = (a - hi.astype(F32)).astype(BF16)
    return hi, lo


def _dot_sel_rhs(a, sel):
    hi, lo = _split_bf16(a)
    return (jnp.dot(hi, sel, preferred_element_type=F32) + jnp.dot(lo, sel, preferred_element_type=F32))


def _dot_sel_lhs(sel, a):
    hi, lo = _split_bf16(a)
    return (jnp.dot(sel, hi, preferred_element_type=F32) + jnp.dot(sel, lo, preferred_element_type=F32))


def _layer_norm(y, g, b):
    mu = jnp.mean(y, axis=-1, keepdims=True)
    d = y - mu
    var = jnp.mean(d * d, axis=-1, keepdims=True)
    return d * lax.rsqrt(var + LN_EPS) * g + b


def _softplus(z):
    return jnp.maximum(z, 0.0) + jnp.log1p(jnp.exp(-jnp.abs(z)))


def _gelu_tanh(y):
    c = math.sqrt(2.0 / math.pi)
    return 0.5 * y * (1.0 + jnp.tanh(c * (y + 0.044715 * (y * y * y))))


def _stage_kernel(*refs, alpha, has_mix, proj_splits):
    it = iter(refs)
    x_ref = next(it)
    if has_mix:
        a_ref, b_ref, wa_ref, wb_ref, gm_ref, bm_ref = (next(it) for _ in range(6))
    wg_ref, wu_ref, wd_ref, g_ref, b2_ref = (next(it) for _ in range(5))
    if proj_splits:
        win_ref = next(it)
    o_ref = next(it)
    proj_refs = [next(it) for _ in proj_splits[1:]] if proj_splits else []

    x = x_ref[...]
    if has_mix:
        mix = _dot(a_ref[...], wa_ref[...]) + _dot(b_ref[...], wb_ref[...])
        x = _layer_norm(alpha * x + mix, gm_ref[...], bm_ref[...])
    xb = x.astype(BF16)
    ff = wg_ref.shape[1]
    fc = ff // FF_CHUNKS
    acc = None
    for c in range(0, ff, fc):
        gate = jnp.dot(xb, wg_ref[:, c:c + fc], preferred_element_type=F32)
        up = jnp.dot(xb, wu_ref[:, c:c + fc], preferred_element_type=F32)
        h = (gate * jax.nn.sigmoid(gate) * up).astype(BF16)
        part = jnp.dot(h, wd_ref[c:c + fc, :], preferred_element_type=F32)
        acc = part if acc is None else acc + part
    y = _layer_norm(alpha * x + 0.5 * acc, g_ref[...], b2_ref[...])
    o_ref[...] = y
    if proj_splits:
        proj = jnp.dot(y.astype(BF16), win_ref[...], preferred_element_type=F32)
        for r, lo, hi in zip(proj_refs, proj_splits[:-1], proj_splits[1:]):
            r[...] = proj[:, lo:hi]


def _stage(x, ffn, ln_ffn, alpha, mix=None, w_in=None, proj_splits=()):
    m, d = x.shape
    tm = min(ROW_TILE, m)
    assert m % tm == 0
    row = lambda w: pl.BlockSpec((tm, w), lambda i: (i, 0))
    args, specs = [x], [row(d)]
    if mix is not None:
        a, b, wa, wb, gm, bm = mix
        args += [a, b, wa, wb, gm, bm]
        specs += [row(a.shape[1]), row(b.shape[1]), _const_spec(wa.shape), _const_spec(wb.shape),
                  _const_spec(gm.shape), _const_spec(bm.shape)]
    args += [*ffn, *ln_ffn]
    specs += [_const_spec(w.shape) for w in (*ffn, *ln_ffn)]
    out_shape = [jax.ShapeDtypeStruct((m, d), F32)]
    out_specs = [row(d)]
    if proj_splits:
        args.append(w_in)
        specs.append(_const_spec(w_in.shape))
        for lo, hi in zip(proj_splits[:-1], proj_splits[1:]):
            out_shape.append(jax.ShapeDtypeStruct((m, hi - lo), F32))
            out_specs.append(row(hi - lo))
    kern = functools.partial(_stage_kernel, alpha=alpha, has_mix=mix is not None,
                             proj_splits=tuple(proj_splits))
    outs = pl.pallas_call(
        kern, grid=(m // tm,), in_specs=specs, out_specs=out_specs, out_shape=out_shape,
        compiler_params=_cparams(("parallel",)), name="stage")(*args)
    return outs


def _sb_prompt_kernel(bias_ref, q_ref, k_ref, v_ref, o_ref, *, tq, tk, scale):
    hp = pl.program_id(1)
    qi = pl.program_id(2)
    q = q_ref[0] * (scale * LOG2_E)
    lane = lax.broadcasted_iota(jnp.int32, q.shape, 1)
    kr = lax.broadcasted_iota(jnp.int32, (tk, tk), 0)
    kc = lax.broadcasted_iota(jnp.int32, (tk, tk), 1)
    neg_after = jnp.where(kr > kc, -1.0, 0.0).astype(BF16)
    q_pos = qi * tq + lax.broadcasted_iota(jnp.int32, (tq, tk), 0)
    k_off = lax.broadcasted_iota(jnp.int32, (tq, tk), 1)
    heads = range(LANES // HEAD_DIM)
    qh = [jnp.where((lane >= h * HEAD_DIM) & (lane < (h + 1) * HEAD_DIM), q, 0.0).astype(BF16) for h in heads]
    bias = [bias_ref[2 * hp + h] * LOG2_E for h in heads]

    def tile_step(kb, carry, masked):
        start = pl.multiple_of(kb * tk, tk)
        ks = k_ref[0, pl.ds(start, tk), :].astype(BF16)
        vs = v_ref[0, pl.ds(start, tk), :].astype(BF16)
        causal = (k_off + kb * tk < q_pos) if masked else None
        out = []
        for h in heads:
            acc, run = carry[2 * h], carry[2 * h + 1]
            z = lax.dot_general(qh[h], ks, (((1,), (1,)), ((), ())), preferred_element_type=F32) + bias[h]
            sp = jnp.maximum(z, 0.0) + jnp.log2(1.0 + jnp.exp2(-jnp.abs(z)))
            spm = jnp.where(causal, sp, 0.0) if masked else sp
            later = jnp.dot(spm.astype(BF16), neg_after, preferred_element_type=F32)
            w = jnp.exp2(z - sp + later + run)
            if masked:
                w = jnp.where(causal, w, 0.0)
            acc = acc + jnp.dot(w.astype(BF16), vs, preferred_element_type=F32)
            run = run + later[:, 0:1] - spm[:, 0:1]
            out += [acc, run]
        return tuple(out)

    carry = (jnp.zeros((tq, LANES), F32), jnp.zeros((tq, 1), F32)) * len(heads)
    band = tq // tk
    for m in range(band):
        carry = tile_step(qi * band + (band - 1 - m), carry, True)
    carry = lax.fori_loop(0, qi * band, lambda i, cr: tile_step(qi * band - 1 - i, cr, False), carry)
    o_ref[0] = jnp.where(lane < HEAD_DIM, carry[0], carry[2])


def _sb_prompt(q, k, v, bias):
    b, t, hd = q.shape
    tq = min(SB_TILE_Q, t)
    tk = min(SB_TILE_K, t)
    assert t % tq == 0 and tq % tk == 0 and hd % LANES == 0
    kern = functools.partial(_sb_prompt_kernel, tq=tq, tk=tk, scale=HEAD_DIM ** -0.5)
    qspec = pl.BlockSpec((1, tq, LANES), lambda bi, hp, qi: (bi, qi, hp))
    kvspec = pl.BlockSpec((1, t, LANES), lambda bi, hp, qi: (bi, 0, hp))
    return pl.pallas_call(
        kern, grid=(b, hd // LANES, t // tq),
        in_specs=[pl.BlockSpec(memory_space=pltpu.SMEM), qspec, kvspec, kvspec],
        out_specs=qspec, out_shape=jax.ShapeDtypeStruct((b, t, hd), F32),
        compiler_params=_cparams(("parallel", "parallel", "arbitrary")), name="sb_prompt")(bias, q, k, v)


def _sb_sample_kernel(pt_ref, q_ref, bias_ref, after_ref, *refs, pages, n_heads, scale):
    k_refs = refs[:pages]
    v_refs = refs[pages:2 * pages]
    o_ref = refs[2 * pages]
    qt_ref, acc_ref, run_ref = refs[2 * pages + 1:]
    g = pl.program_id(1)
    page = k_refs[0].shape[2]

    @pl.when(g == 0)
    def _():
        q = q_ref[...] * scale
        sub = lax.broadcasted_iota(jnp.int32, q.shape, 0)
        ones = jnp.ones((n_heads, page), BF16)
        for h in range(n_heads):
            qt_ref[h] = _dot_tn(jnp.where(sub == h, q, 0.0), ones)
        acc_ref[...] = jnp.zeros_like(acc_ref)
        run_ref[...] = jnp.zeros_like(run_ref)

    bias = bias_ref[...]
    after = after_ref[...]
    run = run_ref[...]
    for u in range(pages):
        rows = [jnp.sum(k_refs[u][h] * qt_ref[h], axis=0, keepdims=True) for h in range(n_heads)]
        z = jnp.concatenate(rows, axis=0) + bias
        sp = _softplus(z)
        later = _dot_sel_rhs(-sp, after)
        w = jnp.exp(z - sp + later + run)
        run = run - jnp.sum(sp, axis=1, keepdims=True)
        for h in range(n_heads):
            acc_ref[h] += v_refs[u][h] * w[h:h + 1, :]
    run_ref[...] = run

    @pl.when(g == pl.num_programs(1) - 1)
    def _():
        ones = jnp.ones((SUBLANES, page), BF16)
        rows = []
        for h in range(n_heads):
            hi, lo = _split_bf16(acc_ref[h])
            rows.append((_dot_nt(ones, hi) + _dot_nt(ones, lo))[0:1])
        o_ref[...] = jnp.concatenate(rows, axis=0)


def _sb_sample(q, cache_k, cache_v, page_table, bias):
    nb, n_heads, _ = q.shape
    n_pages = page_table.shape[1]
    page = cache_k.shape[3]
    pages = min(SB_PAGES_PER_STEP, n_pages)
    assert n_pages % pages == 0
    bias_b = jnp.broadcast_to(bias.astype(F32)[:, None], (n_heads, page))
    key = jnp.arange(page)
    after = (key[:, None] > key[None, :]).astype(BF16)

    def page_spec(u):
        def imap(b, g, pt):
            return (pt[b, n_pages - 1 - (g * pages + u)], 0, 0, 0)
        return pl.BlockSpec((None, n_heads, HEAD_DIM, page), imap)

    qspec = pl.BlockSpec((None, n_heads, HEAD_DIM), lambda b, g, pt: (b, 0, 0))
    const = lambda shape: pl.BlockSpec(shape, lambda b, g, pt: (0, 0))
    grid_spec = pltpu.PrefetchScalarGridSpec(
        num_scalar_prefetch=1, grid=(nb, n_pages // pages),
        in_specs=[qspec, const((n_heads, page)), const((page, page))] + [page_spec(u) for u in range(pages)] * 2,
        out_specs=qspec,
        scratch_shapes=[pltpu.VMEM((n_heads, HEAD_DIM, page), F32), pltpu.VMEM((n_heads, HEAD_DIM, page), F32),
                        pltpu.VMEM((n_heads, page), F32)])
    kern = functools.partial(_sb_sample_kernel, pages=pages, n_heads=n_heads, scale=HEAD_DIM ** -0.5)
    return pl.pallas_call(
        kern, grid_spec=grid_spec, out_shape=jax.ShapeDtypeStruct((nb, n_heads, HEAD_DIM), F32),
        compiler_params=_cparams(("parallel", "arbitrary")), name="sb_sample")(
            page_table, q, bias_b, after, *([cache_k] * pages), *([cache_v] * pages))


RWKV_PREP_PARAMS = ("mu", "w0", "a0", "k_k", "k_a", "w_w2", "w_a2", "w_g2", "ones")


def _rwkv_prep_math(pb, prev, mu_ref, w0_ref, a0_ref, kk_ref, ka_ref, ww2_ref, wa2_ref, wg2_ref, ones_ref):
    pm = pb + mu_ref[...] * (prev - pb)
    d = w0_ref.shape[1]
    r = pm[:, 0:d]
    kr = pm[:, d:2 * d]
    vr = pm[:, 2 * d:3 * d]
    wad = pm[:, 3 * d:3 * d + LANES]
    gd = pm[:, 3 * d + LANES:]
    z_w = w0_ref[...] + _dot(jnp.tanh(wad), ww2_ref[...])
    log_decay = -math.exp(-0.5) * jax.nn.sigmoid(z_w)
    iclr = jax.nn.sigmoid(a0_ref[...] + _dot(wad, wa2_ref[...]))
    gate = _dot(jax.nn.sigmoid(gd), wg2_ref[...])
    kkr = kr * kk_ref[...]
    ss = _dot_sel_rhs(kkr * kkr, ones_ref[...])
    kk = kkr * lax.rsqrt(jnp.maximum(ss, 1e-24))
    kf = kr * (1.0 + (iclr - 1.0) * ka_ref[...])
    return r, kf, vr, log_decay, kk, kk * iclr, gate


def _rwkv_prep_kernel(pb_ref, prev_ref, *refs):
    n = len(RWKV_PREP_PARAMS)
    outs = _rwkv_prep_math(pb_ref[...], prev_ref[...], *refs[:n])
    for o_ref, val in zip(refs[n:], outs):
        o_ref[...] = val


def _rwkv_prep(pb, prev, p):
    rows, c = pb.shape
    d = p["w0"].shape[1]
    consts = [p[n] for n in RWKV_PREP_PARAMS]
    return pl.pallas_call(
        _rwkv_prep_kernel, out_shape=[jax.ShapeDtypeStruct((rows, d), F32)] * 7,
        compiler_params=pltpu.CompilerParams(vmem_limit_bytes=VMEM_LIMIT_BYTES), name="rwkv_prep")(
            pb, prev, *consts)


def _rwkv_chunk_kernel(pb_ref, sh_ref, s0_ref, *refs, chunk, n_heads):
    n_prep = len(RWKV_PREP_PARAMS)
    prep_refs = refs[:n_prep]
    rk_ref, gng_ref, gnb_ref, tri_ref, avg_ref, o_ref, sout_ref, st_ref, carry_ref = refs[n_prep:]
    ones_ref = prep_refs[-1]
    L = chunk
    c = pl.program_id(1)

    @pl.when(c == 0)
    def _():
        st_ref[...] = s0_ref[0]
        carry_ref[...] = sh_ref[0]

    pb = pb_ref[0]
    row0 = lax.broadcasted_iota(jnp.int32, pb.shape, 0) == 0
    prev = jnp.where(row0, carry_ref[...], pltpu.roll(pb, 1, axis=0))
    carry_ref[...] = pb[L - 1:L, :]
    r, kf, v, lw, kk, bb, gate = _rwkv_prep_math(pb, prev, *prep_refs)
    cs = _dot_sel_lhs(tri_ref[...], lw)
    cs_end = cs[L - 1:L, :]
    p_in = jnp.exp(cs)
    p_inv = jnp.exp(-cs)
    p_end = jnp.exp(cs_end - cs)
    rt = r * p_in
    kt = kk * jnp.exp(cs - lw)
    bt = bb * p_inv
    kft = kf * p_inv
    kh = kf * p_end
    bh = bb * p_end
    p_last = jnp.exp(cs_end)

    row = lax.broadcasted_iota(jnp.int32, (2 * L, 2 * L), 0)
    col = lax.broadcasted_iota(jnp.int32, (2 * L, 2 * L), 1)
    t_idx = row & (L - 1)
    s_idx = col & (L - 1)
    tril = s_idx < t_idx + row // L
    row1 = lax.broadcasted_iota(jnp.int32, (L, L), 0)
    col1 = lax.broadcasted_iota(jnp.int32, (L, L), 1)
    same_blk = (row1 // RWKV_SOLVE_BLOCK) == (col1 // RWKV_SOLVE_BLOCK)

    hs = range(n_heads)
    sl = [slice(h * HEAD_DIM, (h + 1) * HEAD_DIM) for h in hs]
    s0 = [st_ref[h] for h in hs]
    vh = [v[:, sl[h]] for h in hs]
    x = [jnp.concatenate([kt[:, sl[h]], rt[:, sl[h]]], axis=0) for h in hs]
    yk = [jnp.concatenate([bt[:, sl[h]], kft[:, sl[h]]], axis=0) for h in hs]
    gm = [jnp.where(tril, _dot_nt(x[h], yk[h]), 0.0) for h in hs]
    xm = [_dot_nt(x[h], s0[h]) for h in hs]
    rhs = [-(xm[h][:L] + _dot(gm[h][:L, L:], vh[h])) for h in hs]
    ad = [jnp.where(same_blk, gm[h][:L, :L], 0.0) for h in hs]
    z = [jnp.concatenate([gm[h][:L, :L] - ad[h], rhs[h]], axis=1) for h in hs]
    z = [z[h] - _dot(ad[h], z[h]) for h in hs]
    apow = ad
    span = 2
    while span < RWKV_SOLVE_BLOCK:
        apow = [_dot(apow[h], apow[h]) for h in hs]
        z = [z[h] + _dot(apow[h], z[h]) for h in hs]
        span *= 2
    w2 = [_dot(z[h][:, :L], z[h]) for h in hs]
    x1 = [z[h][:, L:] - w2[h][:, L:] for h in hs]
    u = [x1[h] + _dot(w2[h][:, :L], x1[h]) for h in hs]
    uv = [jnp.concatenate([u[h], vh[h]], axis=0) for h in hs]
    ys = [xm[h][L:] + _dot(gm[h][L:], uv[h]) for h in hs]
    for h in hs:
        bk = jnp.concatenate([bh[:, sl[h]], kh[:, sl[h]]], axis=0)
        st_ref[h] = s0[h] * p_last[:, sl[h]] + _dot_tn(uv[h], bk)
    y = jnp.concatenate(ys, axis=1)
    mean = _dot_sel_rhs(y, avg_ref[...])
    d = y - mean
    var = _dot_sel_rhs(d * d, avg_ref[...])
    yn = d * lax.rsqrt(var + GN_EPS) * gng_ref[...] + gnb_ref[...]
    bonus = _dot_sel_rhs(r * kf * rk_ref[...], ones_ref[...]) * v
    o_ref[0] = (yn + bonus) * gate

    @pl.when(c == pl.num_programs(1) - 1)
    def _():
        sout_ref[0] = st_ref[...]


def _rwkv_chunked(pb, shift, s0, p):
    b, t, c = pb.shape
    d = p["w0"].shape[1]
    n_heads = d // HEAD_DIM
    L = RWKV_CHUNK
    assert t % L == 0 and L // RWKV_SOLVE_BLOCK == 4
    blk = lambda w: pl.BlockSpec((1, L, w), lambda bi, ci: (bi, ci, 0))
    sh_spec = pl.BlockSpec((1, 1, c), lambda bi, ci: (bi, 0, 0))
    sspec = pl.BlockSpec((1, n_heads, HEAD_DIM, HEAD_DIM), lambda bi, ci: (bi, 0, 0, 0))
    tri = jnp.tril(jnp.ones((L, L), F32)).astype(BF16)
    consts = [p[n] for n in RWKV_PREP_PARAMS] + [p["r_k"], p["gn_g"], p["gn_b"], tri, p["avg"]]
    kern = functools.partial(_rwkv_chunk_kernel, chunk=L, n_heads=n_heads)
    return pl.pallas_call(
        kern, grid=(b, t // L),
        in_specs=[blk(c), sh_spec, sspec] + [_const_spec(a.shape) for a in consts],
        out_specs=[blk(d), sspec],
        out_shape=[jax.ShapeDtypeStruct((b, t, d), F32),
                   jax.ShapeDtypeStruct((b, n_heads, HEAD_DIM, HEAD_DIM), F32)],
        scratch_shapes=[pltpu.VMEM((n_heads, HEAD_DIM, HEAD_DIM), F32), pltpu.VMEM((1, c), F32)],
        compiler_params=_cparams(("parallel", "arbitrary")), name="rwkv_chunk")(pb, shift, s0, *consts)


def _rwkv_step_kernel(s_ref, r_ref, k_ref, v_ref, lw_ref, kk_ref, bb_ref, g_ref, rk_ref, gng_ref, gnb_ref,
                      so_ref, o_ref, y_ref):
    r = r_ref[0]
    k = k_ref[0]
    v = v_ref[0]
    w = jnp.exp(lw_ref[0])
    kk = kk_ref[0]
    bb = bb_ref[0]
    for i in range(HEAD_DIM):
        s = s_ref[0, i]
        sa = jnp.sum(s * kk, axis=0, keepdims=True)
        s2 = s * w - sa * bb + v[i:i + 1, :] * k
        so_ref[0, i] = s2
        y_ref[i:i + 1, :] = jnp.sum(s2 * r, axis=0, keepdims=True)
    y = y_ref[...]
    mu = jnp.mean(y, axis=0, keepdims=True)
    d = y - mu
    var = jnp.mean(d * d, axis=0, keepdims=True)
    yn = d * lax.rsqrt(var + GN_EPS) * gng_ref[0] + gnb_ref[0]
    bonus = jnp.sum(r * k * rk_ref[0], axis=0, keepdims=True) * v
    o_ref[0] = (yn + bonus) * g_ref[0]


def _rwkv_step(prep, state, p):
    nb, d = prep[0].shape
    n_heads = d // HEAD_DIM
    to_heads = lambda a: a.reshape(nb, n_heads, HEAD_DIM).transpose(1, 2, 0)
    vecs = [to_heads(a) for a in prep]
    params = [jnp.broadcast_to(p[n].reshape(n_heads, HEAD_DIM, 1), (n_heads, HEAD_DIM, nb))
              for n in ("r_k", "gn_g", "gn_b")]
    st = state.transpose(1, 2, 3, 0)
    sspec = pl.BlockSpec((1, HEAD_DIM, HEAD_DIM, nb), lambda h: (h, 0, 0, 0))
    vspec = pl.BlockSpec((1, HEAD_DIM, nb), lambda h: (h, 0, 0))
    st_new, o = pl.pallas_call(
        _rwkv_step_kernel, grid=(n_heads,),
        in_specs=[sspec] + [vspec] * 10, out_specs=[sspec, vspec],
        out_shape=[jax.ShapeDtypeStruct(st.shape, F32), jax.ShapeDtypeStruct((n_heads, HEAD_DIM, nb), F32)],
        scratch_shapes=[pltpu.VMEM((HEAD_DIM, nb), F32)],
        compiler_params=_cparams(("parallel",)), name="rwkv_step")(st, *vecs, *params)
    return o.transpose(2, 0, 1).reshape(nb, d), st_new.transpose(3, 0, 1, 2)


def _s5_disc_kernel(lr_ref, li_ref, ldt_ref, ar_ref, ai_ref, cr_ref, ci_ref):
    lr = lr_ref[...]
    li = li_ref[...]
    dt = jnp.exp(ldt_ref[...])
    mag = jnp.exp(lr * dt)
    ar = mag * jnp.cos(li * dt)
    ai = mag * jnp.sin(li * dt)
    den = lr * lr + li * li
    ar_ref[...] = ar
    ai_ref[...] = ai
    cr_ref[...] = ((ar - 1.0) * lr + ai * li) / den
    ci_ref[...] = (ai * lr - (ar - 1.0) * li) / den


def _s5_discretize(lam_re, lam_im, log_dt):
    g, p = lam_re.shape
    ldt = jnp.broadcast_to(log_dt.reshape(g, 1), (g, p))
    return pl.pallas_call(_s5_disc_kernel, out_shape=[jax.ShapeDtypeStruct((g, p), F32)] * 4,
                          name="s5_disc")(lam_re, lam_im, ldt)


def _s5_mix_tail(x, y_state, dsk_ref, wo_ref, wgt_ref, g_ref, b_ref, alpha):
    y = y_state + dsk_ref[...] * x
    zg = _gelu_tanh(y).astype(BF16)
    out = (jnp.dot(zg, wo_ref[...], preferred_element_type=F32)
           * jax.nn.sigmoid(jnp.dot(zg, wgt_ref[...], preferred_element_type=F32)))
    return _layer_norm(alpha * x + out, g_ref[...], b_ref[...])


def _s5_scan_kernel(x_ref, wb_ref, wcr_ref, wci_ref, ar_ref, ai_ref, dsk_ref, wo_ref, wgt_ref, g_ref, b_ref,
                    s0r_ref, s0i_ref, perm_ref, permt_ref, o_ref, sr_out, si_out, bur, bui, sr_st, si_st,
                    *, tc, nb, alpha):
    ti = pl.program_id(0)

    @pl.when(ti == 0)
    def _():
        sr_st[...] = s0r_ref[...]
        si_st[...] = s0i_ref[...]

    x = x_ref[...].reshape(nb * tc, x_ref.shape[2])
    xb = jnp.dot(perm_ref[...], x.astype(BF16), preferred_element_type=F32).astype(BF16)
    n_blk = wb_ref.shape[0]
    half = wb_ref.shape[2] // 2
    for j in range(n_blk):
        pj = jnp.dot(xb[:, j * LANES:(j + 1) * LANES], wb_ref[j], preferred_element_type=F32)
        bur[:, j * half:(j + 1) * half] = pj[:, :half]
        bui[:, j * half:(j + 1) * half] = pj[:, half:]
    n_state = bur.shape[1]
    for s in range(n_state // S5_SLAB):
        cols = slice(s * S5_SLAB, (s + 1) * S5_SLAB)
        ar = jnp.broadcast_to(ar_ref[:, cols], (nb, S5_SLAB))
        ai = jnp.broadcast_to(ai_ref[:, cols], (nb, S5_SLAB))

        def step(t, carry):
            sr, si = carry
            rows = pl.ds(pl.multiple_of(t * nb, nb), nb)
            nsr = ar * sr - ai * si + bur[rows, cols]
            nsi = ar * si + ai * sr + bui[rows, cols]
            bur[rows, cols] = nsr
            bui[rows, cols] = nsi
            return nsr, nsi

        sr, si = lax.fori_loop(0, tc, step, (sr_st[:, cols], si_st[:, cols]))
        sr_st[:, cols] = sr
        si_st[:, cols] = si
    ys = []
    for j in range(n_blk):
        ys.append(_dot(bur[:, j * half:(j + 1) * half], wcr_ref[j])
                  + _dot(bui[:, j * half:(j + 1) * half], wci_ref[j]))
    y_tb = jnp.concatenate(ys, axis=1)
    y_state = _dot_sel_lhs(permt_ref[...], y_tb)
    out = _s5_mix_tail(x, y_state, dsk_ref, wo_ref, wgt_ref, g_ref, b_ref, alpha)
    o_ref[...] = out.reshape(o_ref.shape)

    @pl.when(ti == pl.num_programs(0) - 1)
    def _():
        sr_out[...] = sr_st[...]
        si_out[...] = si_st[...]


def _s5_scan(x, s0r, s0i, sp, ln, alpha):
    nb, t, d = x.shape
    tc = min(S5_TIME_CHUNK, t)
    assert t % tc == 0 and nb == SUBLANES
    n_state = s0r.shape[1]
    r = jnp.arange(nb * tc)
    perm = ((r[:, None] // nb == r[None, :] % tc) & (r[:, None] % nb == r[None, :] // tc)).astype(BF16)
    consts = [sp["wb"], sp["wcr"], sp["wci"], sp["ar"], sp["ai"], sp["d_skip"], sp["w_out"], sp["w_gate"],
              ln[0], ln[1], s0r, s0i, perm, perm.T]
    blk = pl.BlockSpec((nb, tc, d), lambda i: (0, i, 0))
    st_spec = pl.BlockSpec((nb, n_state), lambda i: (0, 0))
    kern = functools.partial(_s5_scan_kernel, tc=tc, nb=nb, alpha=alpha)
    return pl.pallas_call(
        kern, grid=(t // tc,),
        in_specs=[blk] + [_const_spec(a.shape) for a in consts],
        out_specs=[blk, st_spec, st_spec],
        out_shape=[jax.ShapeDtypeStruct((nb, t, d), F32), jax.ShapeDtypeStruct((nb, n_state), F32),
                   jax.ShapeDtypeStruct((nb, n_state), F32)],
        scratch_shapes=[pltpu.VMEM((tc * nb, n_state), F32), pltpu.VMEM((tc * nb, n_state), F32),
                        pltpu.VMEM((nb, n_state), F32), pltpu.VMEM((nb, n_state), F32)],
        compiler_params=_cparams(("arbitrary",)), name="s5_scan")(x, *consts)


def _s5_step_kernel(x_ref, wb_ref, wcr_ref, wci_ref, ar_ref, ai_ref, dsk_ref, wo_ref, wgt_ref, g_ref, b_ref,
                    s0r_ref, s0i_ref, o_ref, sr_out, si_out, *, alpha):
    x = x_ref[...]
    xb = x.astype(BF16)
    n_blk = wb_ref.shape[0]
    half = wb_ref.shape[2] // 2
    ys = []
    for j in range(n_blk):
        cols = slice(j * half, (j + 1) * half)
        pj = jnp.dot(xb[:, j * LANES:(j + 1) * LANES], wb_ref[j], preferred_element_type=F32)
        ar = ar_ref[:, cols]
        ai = ai_ref[:, cols]
        sr = s0r_ref[:, cols]
        si = s0i_ref[:, cols]
        nsr = ar * sr - ai * si + pj[:, :half]
        nsi = ar * si + ai * sr + pj[:, half:]
        sr_out[:, cols] = nsr
        si_out[:, cols] = nsi
        ys.append(_dot(nsr, wcr_ref[j]) + _dot(nsi, wci_ref[j]))
    y_state = jnp.concatenate(ys, axis=1)
    o_ref[...] = _s5_mix_tail(x, y_state, dsk_ref, wo_ref, wgt_ref, g_ref, b_ref, alpha)


def _s5_step(x, s0r, s0i, sp, ln, alpha):
    m, d = x.shape
    n_state = s0r.shape[1]
    args = [x, sp["wb"], sp["wcr"], sp["wci"], sp["ar"], sp["ai"], sp["d_skip"], sp["w_out"], sp["w_gate"],
            ln[0], ln[1], s0r, s0i]
    kern = functools.partial(_s5_step_kernel, alpha=alpha)
    return pl.pallas_call(
        kern,
        out_shape=[jax.ShapeDtypeStruct((m, d), F32), jax.ShapeDtypeStruct((m, n_state), F32),
                   jax.ShapeDtypeStruct((m, n_state), F32)],
        compiler_params=pltpu.CompilerParams(vmem_limit_bytes=VMEM_LIMIT_BYTES), name="s5_step")(*args)


def _s5_weights(prm, i):
    ar, ai, cr, ci = _s5_discretize(prm["lam_re"][i], prm["lam_im"][i], prm["log_dt"][i])
    b_re, b_im = prm["b_re"][i], prm["b_im"][i]
    bbr = cr[..., None] * b_re - ci[..., None] * b_im
    bbi = cr[..., None] * b_im + ci[..., None] * b_re
    n_groups, n_state, n_ch = bbr.shape
    gpb = LANES // n_ch
    n_blk = n_groups // gpb
    eye = jnp.eye(gpb, dtype=F32)

    def blk_in(bb):
        w = jnp.einsum("jgpc,gh->jgchp", bb.reshape(n_blk, gpb, n_state, n_ch), eye)
        return w.reshape(n_blk, LANES, gpb * n_state)

    def blk_out(cc):
        w = jnp.einsum("jgcp,gh->jhpgc", cc.reshape(n_blk, gpb, n_ch, n_state), eye)
        return w.reshape(n_blk, gpb * n_state, LANES)

    return dict(
        wb=jnp.concatenate([blk_in(bbr), blk_in(bbi)], axis=-1).astype(BF16),
        wcr=blk_out(prm["c_re"][i]).astype(BF16), wci=(-blk_out(prm["c_im"][i])).astype(BF16),
        ar=ar.reshape(1, -1), ai=ai.reshape(1, -1), d_skip=prm["d_skip"][i].reshape(1, -1),
        w_out=prm["w_glu_out"][i].astype(BF16), w_gate=prm["w_glu_gate"][i].astype(BF16))


def _rwkv_weights(prm, i, d_rwkv):
    rank_d = prm["w_w2"].shape[1]
    rank_a = prm["w_a2"].shape[1]
    assert rank_d + rank_a == LANES
    zeros_d = jnp.zeros((rank_d, d_rwkv), F32)
    zeros_a = jnp.zeros((rank_a, d_rwkv), F32)
    head = jnp.arange(d_rwkv) // HEAD_DIM
    ones = (head[:, None] == head[None, :]).astype(F32)
    row = lambda a: a.reshape(1, -1).astype(F32)
    return dict(
        mu=row(prm["mu_shift"][i]), w0=row(prm["w0"][i]), a0=row(prm["a0"][i]), k_k=row(prm["k_k"][i]),
        k_a=row(prm["k_a"][i]), r_k=row(prm["r_k"][i]), gn_g=row(prm["gn_g"][i]), gn_b=row(prm["gn_b"][i]),
        w_w2=jnp.concatenate([prm["w_w2"][i], zeros_a], axis=0).astype(BF16),
        w_a2=jnp.concatenate([zeros_d, prm["w_a2"][i]], axis=0).astype(BF16),
        w_g2=prm["w_g2"][i].astype(BF16), ones=ones.astype(BF16), avg=(ones / HEAD_DIM).astype(BF16))


def _trunk(x, cache_k, cache_v, page_table, wkv0, shift0, s50, prm, w):
    bsz, t, d = x.shape
    depth = prm["ln_g"].shape[0]
    alpha = (2.0 * depth) ** 0.25
    d_sb = prm["sb_bias"].shape[1] * HEAD_DIM
    d_rwkv = prm["w0"].shape[1]
    splits = (0, d_sb, 2 * d_sb, 3 * d_sb, prm["w_in_even"].shape[2])
    ln = lambda layer, j: (prm["ln_g"][layer, j].reshape(1, d), prm["ln_b"][layer, j].reshape(1, d))
    prompt = cache_k is None
    rows = x.reshape(bsz * t, d)
    out_k, out_v, out_wkv, out_shift, out_s5 = [], [], [], [], []
    for layer in range(depth):
        i = layer // 2
        ffn1 = (w["ffn1_wg"][layer], w["ffn1_wu"][layer], w["ffn1_wd"][layer])
        ffn2 = (w["ffn2_wg"][layer], w["ffn2_wu"][layer], w["ffn2_wd"][layer])
        if layer % 2 == 0:
            x1, q, k, v, pb = _stage(rows, ffn1, ln(layer, 0), alpha, w_in=w["w_in_even"][i],
                                     proj_splits=splits)
            rp = _rwkv_weights(prm, i, d_rwkv)
            pb3 = pb.reshape(bsz, t, -1)
            if prompt:
                o_sb = _sb_prompt(q.reshape(bsz, t, d_sb), k.reshape(bsz, t, d_sb), v.reshape(bsz, t, d_sb),
                                  prm["sb_bias"][i]).reshape(bsz * t, d_sb)
                o_rwkv, wkv = _rwkv_chunked(pb3, shift0[i][:, None, :], wkv0[i], rp)
                o_rwkv = o_rwkv.reshape(bsz * t, d_rwkv)
            else:
                assert t == 1
                n_heads = d_sb // HEAD_DIM
                o_sb = _sb_sample(q.reshape(bsz, n_heads, HEAD_DIM), cache_k[i], cache_v[i], page_table,
                                  prm["sb_bias"][i]).reshape(bsz, d_sb)
                o_rwkv, wkv = _rwkv_step(_rwkv_prep(pb, shift0[i], rp), wkv0[i], rp)
            out_k.append(k.reshape(bsz, t, -1, HEAD_DIM))
            out_v.append(v.reshape(bsz, t, -1, HEAD_DIM))
            out_wkv.append(wkv)
            out_shift.append(pb3[:, -1])
            w_out = w["w_out_even"][i]
            mix = (o_sb, o_rwkv, w_out[:d_sb], w_out[d_sb:], *ln(layer, 1))
            (rows,) = _stage(x1, ffn2, ln(layer, 2), alpha, mix=mix)
        else:
            sp = _s5_weights(prm, i)
            s0r = s50[i][..., 0].reshape(bsz, -1)
            s0i = s50[i][..., 1].reshape(bsz, -1)
            (x1,) = _stage(rows, ffn1, ln(layer, 0), alpha)
            if t > 1:
                x2, sr, si = _s5_scan(x1.reshape(bsz, t, d), s0r, s0i, sp, ln(layer, 1), alpha)
                x2 = x2.reshape(bsz * t, d)
            else:
                x2, sr, si = _s5_step(x1, s0r, s0i, sp, ln(layer, 1), alpha)
            n_groups = prm["lam_re"].shape[1]
            out_s5.append(jnp.stack([sr.reshape(bsz, n_groups, -1), si.reshape(bsz, n_groups, -1)], axis=-1))
            (rows,) = _stage(x2, ffn2, ln(layer, 2), alpha)
    y = rows.reshape(bsz, t, d)
    return (y, jnp.stack(out_k), jnp.stack(out_v), jnp.stack(out_wkv), jnp.stack(out_shift),
            jnp.stack(out_s5))


def kernel(x_prompt, x_sample, cache_k_sb, cache_v_sb, page_table, state_wkv, state_shift, state_s5, ln_g, ln_b, ffn1_wg, ffn1_wu, ffn1_wd, ffn2_wg, ffn2_wu, ffn2_wd, w_in_even, w_out_even, sb_bias, mu_shift, w0, w_w2, a0, w_a2, w_g2, k_k, k_a, r_k, gn_g, gn_b, lam_re, lam_im, log_dt, b_re, b_im, c_re, c_im, d_skip, w_glu_out, w_glu_gate):
    prm = dict(ln_g=ln_g, ln_b=ln_b, sb_bias=sb_bias, mu_shift=mu_shift, w0=w0, w_w2=w_w2, a0=a0, w_a2=w_a2,
               w_g2=w_g2, k_k=k_k, k_a=k_a, r_k=r_k.reshape(r_k.shape[0], -1), gn_g=gn_g, gn_b=gn_b,
               lam_re=lam_re, lam_im=lam_im, log_dt=log_dt, b_re=b_re, b_im=b_im, c_re=c_re, c_im=c_im,
               d_skip=d_skip, w_glu_out=w_glu_out, w_glu_gate=w_glu_gate, w_in_even=w_in_even)
    w = dict(ffn1_wg=ffn1_wg, ffn1_wu=ffn1_wu, ffn1_wd=ffn1_wd, ffn2_wg=ffn2_wg, ffn2_wu=ffn2_wu,
             ffn2_wd=ffn2_wd, w_in_even=w_in_even, w_out_even=w_out_even)
    w = {name: a.astype(BF16) for name, a in w.items()}
    nb = x_prompt.shape[0]
    n_even, n_odd = state_wkv.shape[0], state_s5.shape[0]
    wkv0 = jnp.zeros((n_even, nb) + state_wkv.shape[2:], F32)
    shift0 = jnp.zeros((n_even, nb, state_shift.shape[2]), F32)
    s50 = jnp.zeros((n_odd, nb) + state_s5.shape[2:], F32)
    y_p, p_k, p_v, p_wkv, p_shift, p_s5 = _trunk(x_prompt, None, None, None, wkv0, shift0, s50, prm, w)
    cache_kt = cache_k_sb.transpose(0, 1, 3, 4, 2)
    cache_vt = cache_v_sb.transpose(0, 1, 3, 4, 2)
    y_s, s_k, s_v, s_wkv, s_shift, s_s5 = _trunk(x_sample, cache_kt, cache_vt, page_table, state_wkv,
                                                 state_shift, state_s5, prm, w)
    return (y_p, y_s, p_k, p_v, p_wkv, p_shift, p_s5, s_k, s_v, s_wkv, s_shift, s_s5)
```

```python
import functools
import math

import jax
import jax.numpy as jnp
from jax import lax
from jax.experimental import pallas as pl
from jax.experimental.pallas import tpu as pltpu

F32 = jnp.float32
BF16 = jnp.bfloat16

HEAD_DIM = 64
LOG2_E = 1.0 / math.log(2.0)
LN_EPS = 1e-5
GN_EPS = 64e-5
S5_GROUP = 16
S5_STATE = 64
LANES = 128
SUBLANES = 8
VMEM_LIMIT_BYTES = 56 * 1024 * 1024

ROW_TILE = 512
FF_CHUNKS = 2
SB_TILE_Q = 512
SB_TILE_K = 256
SB_PAGES_PER_STEP = 16
RWKV_CHUNK = 64
RWKV_SOLVE_BLOCK = 16
RWKV_SEQS_PER_STEP = 2
S5_TIME_CHUNK = 32
S5_SLAB = 1024


def _cparams(sem):
    return pltpu.CompilerParams(dimension_semantics=sem, vmem_limit_bytes=VMEM_LIMIT_BYTES)


def _const_spec(shape):
    nd = len(shape)
    return pl.BlockSpec(shape, lambda *_: (0,) * nd, pipeline_mode=pl.Buffered(1))


def _dot(a, b):
    return jnp.dot(a.astype(BF16), b.astype(BF16), preferred_element_type=F32)


def _dot_nt(a, b):
    return lax.dot_general(a.astype(BF16), b.astype(BF16), (((1,), (1,)), ((), ())),
                           preferred_element_type=F32)


def _dot_tn(a, b):
    return lax.dot_general(a.astype(BF16), b.astype(BF16), (((0,), (0,)), ((), ())),
                           preferred_element_type=F32)


def _split_bf16(a):
    hi = a.astype(BF16)
    lo = (a - hi.astype(F32)).astype(BF16)
    return hi, lo


def _dot_sel_rhs(a, sel):
    hi, lo = _split_bf16(a)
    return (jnp.dot(hi, sel, preferred_element_type=F32) + jnp.dot(lo, sel, preferred_element_type=F32))


def _dot_sel_lhs(sel, a):
    hi, lo = _split_bf16(a)
    return (jnp.dot(sel, hi, preferred_element_type=F32) + jnp.dot(sel, lo, preferred_element_type=F32))


def _layer_norm(y, g, b):
    mu = jnp.mean(y, axis=-1, keepdims=True)
    d = y - mu
    var = jnp.mean(d * d, axis=-1, keepdims=True)
    return d * lax.rsqrt(var + LN_EPS) * g + b


def _softplus(z):
    return jnp.maximum(z, 0.0) + jnp.log1p(jnp.exp(-jnp.abs(z)))


def _gelu_tanh(y):
    c = math.sqrt(2.0 / math.pi)
    return 0.5 * y * (1.0 + jnp.tanh(c * (y + 0.044715 * (y * y * y))))


def _stage_kernel(*refs, alpha, has_mix, proj_splits, proj_transposed):
    it = iter(refs)
    x_ref = next(it)
    if has_mix:
        a_ref, b_ref, wa_ref, wb_ref, gm_ref, bm_ref = (next(it) for _ in range(6))
    wg_ref, wu_ref, wd_ref, g_ref, b2_ref = (next(it) for _ in range(5))
    if proj_splits:
        win_ref = next(it)
    o_ref = next(it)
    proj_refs = [next(it) for _ in proj_splits[1:]] if proj_splits else []
    proj_t_refs = [next(it) for _ in proj_transposed]

    x = x_ref[...]
    if has_mix:
        mix = _dot(a_ref[...], wa_ref[...]) + _dot(b_ref[...], wb_ref[...])
        x = _layer_norm(alpha * x + mix, gm_ref[...], bm_ref[...])
    xb = x.astype(BF16)
    ff = wg_ref.shape[1]
    fc = ff // FF_CHUNKS
    acc = None
    for c in range(0, ff, fc):
        gate = jnp.dot(xb, wg_ref[:, c:c + fc], preferred_element_type=F32)
        up = jnp.dot(xb, wu_ref[:, c:c + fc], preferred_element_type=F32)
        h = (gate * jax.nn.sigmoid(gate) * up).astype(BF16)
        part = jnp.dot(h, wd_ref[c:c + fc, :], preferred_element_type=F32)
        acc = part if acc is None else acc + part
    y = _layer_norm(alpha * x + 0.5 * acc, g_ref[...], b2_ref[...])
    o_ref[...] = y
    if proj_splits:
        proj = jnp.dot(y.astype(BF16), win_ref[...], preferred_element_type=F32)
        for r, lo, hi in zip(proj_refs, proj_splits[:-1], proj_splits[1:]):
            r[...] = proj[:, lo:hi]
        for r, j in zip(proj_t_refs, proj_transposed):
            r[0] = proj[:, proj_splits[j]:proj_splits[j + 1]].T


def _stage(x, ffn, ln_ffn, alpha, mix=None, w_in=None, proj_splits=(), proj_transposed=(), group_rows=None):
    m, d = x.shape
    tm = min(ROW_TILE, m, group_rows or m)
    assert m % tm == 0
    row = lambda w: pl.BlockSpec((tm, w), lambda i: (i, 0))
    args, specs = [x], [row(d)]
    if mix is not None:
        a, b, wa, wb, gm, bm = mix
        args += [a, b, wa, wb, gm, bm]
        specs += [row(a.shape[1]), row(b.shape[1]), _const_spec(wa.shape), _const_spec(wb.shape),
                  _const_spec(gm.shape), _const_spec(bm.shape)]
    args += [*ffn, *ln_ffn]
    specs += [_const_spec(w.shape) for w in (*ffn, *ln_ffn)]
    out_shape = [jax.ShapeDtypeStruct((m, d), F32)]
    out_specs = [row(d)]
    if proj_splits:
        args.append(w_in)
        specs.append(_const_spec(w_in.shape))
        for lo, hi in zip(proj_splits[:-1], proj_splits[1:]):
            out_shape.append(jax.ShapeDtypeStruct((m, hi - lo), F32))
            out_specs.append(row(hi - lo))
        for j in proj_transposed:
            width = proj_splits[j + 1] - proj_splits[j]
            tiles = group_rows // tm
            assert group_rows % tm == 0 and m % group_rows == 0
            out_shape.append(jax.ShapeDtypeStruct((m // group_rows, width, group_rows), F32))
            out_specs.append(pl.BlockSpec((1, width, tm), lambda i, tiles=tiles: (i // tiles, 0, i % tiles)))
    kern = functools.partial(_stage_kernel, alpha=alpha, has_mix=mix is not None,
                             proj_splits=tuple(proj_splits), proj_transposed=tuple(proj_transposed))
    outs = pl.pallas_call(
        kern, grid=(m // tm,), in_specs=specs, out_specs=out_specs, out_shape=out_shape,
        compiler_params=_cparams(("parallel",)), name="stage")(*args)
    return outs


def _sb_prompt_kernel(bias_ref, q_ref, k_ref, v_ref, o_ref, *, tq, tk, scale):
    hp = pl.program_id(1)
    qi = pl.program_id(2)
    q = q_ref[0] * (scale * LOG2_E)
    lane = lax.broadcasted_iota(jnp.int32, q.shape, 1)
    kr = lax.broadcasted_iota(jnp.int32, (tk, tk), 0)
    kc = lax.broadcasted_iota(jnp.int32, (tk, tk), 1)
    neg_after = jnp.where(kr > kc, -1.0, 0.0).astype(BF16)
    q_pos = qi * tq + lax.broadcasted_iota(jnp.int32, (tq, tk), 0)
    k_off = lax.broadcasted_iota(jnp.int32, (tq, tk), 1)
    heads = range(LANES // HEAD_DIM)
    qh = [jnp.where((lane >= h * HEAD_DIM) & (lane < (h + 1) * HEAD_DIM), q, 0.0).astype(BF16) for h in heads]
    bias = [bias_ref[2 * hp + h] * LOG2_E for h in heads]

    def tile_step(kb, carry, masked):
        start = pl.multiple_of(kb * tk, tk)
        ks = k_ref[0, pl.ds(start, tk), :].astype(BF16)
        vs = v_ref[0, pl.ds(start, tk), :].astype(BF16)
        causal = (k_off + kb * tk < q_pos) if masked else None
        out = []
        for h in heads:
            acc, run = carry[2 * h], carry[2 * h + 1]
            z = lax.dot_general(qh[h], ks, (((1,), (1,)), ((), ())), preferred_element_type=F32) + bias[h]
            sp = jnp.maximum(z, 0.0) + jnp.log2(1.0 + jnp.exp2(-jnp.abs(z)))
            spm = jnp.where(causal, sp, 0.0) if masked else sp
            later = jnp.dot(spm.astype(BF16), neg_after, preferred_element_type=F32)
            w = jnp.exp2(z - sp + later + run)
            if masked:
                w = jnp.where(causal, w, 0.0)
            acc = acc + jnp.dot(w.astype(BF16), vs, preferred_element_type=F32)
            run = run + later[:, 0:1] - spm[:, 0:1]
            out += [acc, run]
        return tuple(out)

    carry = (jnp.zeros((tq, LANES), F32), jnp.zeros((tq, 1), F32)) * len(heads)
    band = tq // tk
    for m in range(band):
        carry = tile_step(qi * band + (band - 1 - m), carry, True)
    carry = lax.fori_loop(0, qi * band, lambda i, cr: tile_step(qi * band - 1 - i, cr, False), carry)
    o_ref[0] = jnp.where(lane < HEAD_DIM, carry[0], carry[2])


def _sb_prompt(q, k, v, bias):
    b, t, hd = q.shape
    tq = min(SB_TILE_Q, t)
    tk = min(SB_TILE_K, t)
    assert t % tq == 0 and tq % tk == 0 and hd % LANES == 0
    kern = functools.partial(_sb_prompt_kernel, tq=tq, tk=tk, scale=HEAD_DIM ** -0.5)
    qspec = pl.BlockSpec((1, tq, LANES), lambda bi, hp, qi: (bi, qi, hp))
    kvspec = pl.BlockSpec((1, t, LANES), lambda bi, hp, qi: (bi, 0, hp))
    return pl.pallas_call(
        kern, grid=(b, hd // LANES, t // tq),
        in_specs=[pl.BlockSpec(memory_space=pltpu.SMEM), qspec, kvspec, kvspec],
        out_specs=qspec, out_shape=jax.ShapeDtypeStruct((b, t, hd), F32),
        compiler_params=_cparams(("parallel", "parallel", "arbitrary")), name="sb_prompt")(bias, q, k, v)


def _sb_sample_kernel(pt_ref, q_ref, bias_ref, after_ref, prior_ref, *refs, pages, n_heads, scale):
    k_refs = refs[:pages]
    v_refs = refs[pages:2 * pages]
    o_ref = refs[2 * pages]
    qt_ref, acc_ref, run_ref, zw_ref = refs[2 * pages + 1:]
    g = pl.program_id(1)
    page = k_refs[0].shape[2]

    @pl.when(g == 0)
    def _():
        q = q_ref[...] * scale
        sub = lax.broadcasted_iota(jnp.int32, q.shape, 0)
        ones = jnp.ones((n_heads, page), BF16)
        for h in range(n_heads):
            qt_ref[h] = _dot_tn(jnp.where(sub == h, q, 0.0), ones)
        acc_ref[...] = jnp.zeros_like(acc_ref)
        run_ref[...] = jnp.zeros_like(run_ref)

    for u in range(pages):
        for h in range(n_heads):
            zw_ref[u * n_heads + h:u * n_heads + h + 1, :] = jnp.sum(k_refs[u][h] * qt_ref[h], axis=0,
                                                                     keepdims=True)
    z = zw_ref[...] + bias_ref[...]
    sp = _softplus(z)
    later = _dot_sel_rhs(-sp, after_ref[...])
    tot = jnp.broadcast_to(jnp.sum(sp, axis=1, keepdims=True), sp.shape)
    passed = _dot_sel_lhs(prior_ref[...], tot)
    run = jnp.concatenate([run_ref[...]] * pages, axis=0) - passed
    zw_ref[...] = jnp.exp(z - sp + later + run)
    last = slice((pages - 1) * n_heads, pages * n_heads)
    run_ref[...] = run[last] - tot[last]
    for h in range(n_heads):
        acc = acc_ref[h]
        for u in range(pages):
            acc = acc + v_refs[u][h] * zw_ref[u * n_heads + h:u * n_heads + h + 1, :]
        acc_ref[h] = acc

    @pl.when(g == pl.num_programs(1) - 1)
    def _():
        ones = jnp.ones((SUBLANES, page), BF16)
        rows = []
        for h in range(n_heads):
            hi, lo = _split_bf16(acc_ref[h])
            rows.append((_dot_nt(ones, hi) + _dot_nt(ones, lo))[0:1])
        o_ref[...] = jnp.concatenate(rows, axis=0)


def _sb_sample(q, cache_k, cache_v, page_table, bias):
    nb, n_heads, _ = q.shape
    n_pages = page_table.shape[1]
    page = cache_k.shape[3]
    pages = min(SB_PAGES_PER_STEP, n_pages)
    assert n_pages % pages == 0
    rows = pages * n_heads
    bias_b = jnp.broadcast_to(jnp.tile(bias.astype(F32), pages)[:, None], (rows, page))
    key = jnp.arange(page)
    after = (key[:, None] > key[None, :]).astype(BF16)
    r = jnp.arange(rows)
    prior = ((r[:, None] % n_heads == r[None, :] % n_heads)
             & (r[None, :] // n_heads < r[:, None] // n_heads)).astype(BF16)

    def page_spec(u):
        def imap(b, g, pt):
            return (pt[b, n_pages - 1 - (g * pages + u)], 0, 0, 0)
        return pl.BlockSpec((None, n_heads, HEAD_DIM, page), imap)

    qspec = pl.BlockSpec((None, n_heads, HEAD_DIM), lambda b, g, pt: (b, 0, 0))
    const = lambda shape: pl.BlockSpec(shape, lambda b, g, pt: (0, 0))
    grid_spec = pltpu.PrefetchScalarGridSpec(
        num_scalar_prefetch=1, grid=(nb, n_pages // pages),
        in_specs=[qspec, const((rows, page)), const((page, page)), const((rows, rows))]
                 + [page_spec(u) for u in range(pages)] * 2,
        out_specs=qspec,
        scratch_shapes=[pltpu.VMEM((n_heads, HEAD_DIM, page), F32), pltpu.VMEM((n_heads, HEAD_DIM, page), F32),
                        pltpu.VMEM((n_heads, page), F32), pltpu.VMEM((rows, page), F32)])
    kern = functools.partial(_sb_sample_kernel, pages=pages, n_heads=n_heads, scale=HEAD_DIM ** -0.5)
    return pl.pallas_call(
        kern, grid_spec=grid_spec, out_shape=jax.ShapeDtypeStruct((nb, n_heads, HEAD_DIM), F32),
        compiler_params=_cparams(("parallel", "arbitrary")), name="sb_sample")(
            page_table, q, bias_b, after, prior, *([cache_k] * pages), *([cache_v] * pages))


RWKV_PREP_PARAMS = ("mu", "w0", "a0", "k_k", "k_a", "w_w2", "w_a2", "w_g2", "ones")


def _rwkv_prep_math(pb, prev, mu_ref, w0_ref, a0_ref, kk_ref, ka_ref, ww2_ref, wa2_ref, wg2_ref, ones_ref):
    pm = pb + mu_ref[...] * (prev - pb)
    d = w0_ref.shape[1]
    r = pm[:, 0:d]
    kr = pm[:, d:2 * d]
    vr = pm[:, 2 * d:3 * d]
    wad = pm[:, 3 * d:3 * d + LANES]
    gd = pm[:, 3 * d + LANES:]
    z_w = w0_ref[...] + _dot(jnp.tanh(wad), ww2_ref[...])
    log_decay = -math.exp(-0.5) * jax.nn.sigmoid(z_w)
    iclr = jax.nn.sigmoid(a0_ref[...] + _dot(wad, wa2_ref[...]))
    gate = _dot(jax.nn.sigmoid(gd), wg2_ref[...])
    kkr = kr * kk_ref[...]
    ss = _dot(kkr * kkr, ones_ref[...])
    kk = kkr * lax.rsqrt(jnp.maximum(ss, 1e-24))
    kf = kr * (1.0 + (iclr - 1.0) * ka_ref[...])
    return r, kf, vr, log_decay, kk, kk * iclr, gate


def _rwkv_prep_kernel(pb_ref, prev_ref, *refs):
    n = len(RWKV_PREP_PARAMS)
    outs = _rwkv_prep_math(pb_ref[...], prev_ref[...], *refs[:n])
    for o_ref, val in zip(refs[n:], outs):
        o_ref[...] = val


def _rwkv_prep(pb, prev, p):
    rows, c = pb.shape
    d = p["w0"].shape[1]
    consts = [p[n] for n in RWKV_PREP_PARAMS]
    return pl.pallas_call(
        _rwkv_prep_kernel, out_shape=[jax.ShapeDtypeStruct((rows, d), F32)] * 7,
        compiler_params=pltpu.CompilerParams(vmem_limit_bytes=VMEM_LIMIT_BYTES), name="rwkv_prep")(
            pb, prev, *consts)


def _rwkv_chunk_kernel(pb_ref, sh_ref, s0_ref, *refs, chunk, n_heads, n_seq):
    n_prep = len(RWKV_PREP_PARAMS)
    prep_refs = refs[:n_prep]
    rk_ref, gng_ref, gnb_ref, tri_ref, avg_ref, o_ref, sout_ref, st_ref, carry_ref = refs[n_prep:]
    ones_ref = prep_refs[-1]
    L = chunk
    c = pl.program_id(1)

    @pl.when(c == 0)
    def _():
        st_ref[...] = s0_ref[...]
        carry_ref[...] = sh_ref[...]

    pb = pb_ref[...].reshape(n_seq * L, pb_ref.shape[2])
    row = lax.broadcasted_iota(jnp.int32, pb.shape, 0)
    prev = pltpu.roll(pb, 1, axis=0)
    for s in range(n_seq):
        prev = jnp.where(row == s * L, carry_ref[s], prev)
        carry_ref[s] = pb[(s + 1) * L - 1:(s + 1) * L, :]
    r, kf, v, lw, kk, bb, gate = _rwkv_prep_math(pb, prev, *prep_refs)
    cs = _dot_sel_lhs(tri_ref[...], lw)
    ends = [cs[(s + 1) * L - 1:(s + 1) * L, :] for s in range(n_seq)]
    cs_end = jnp.concatenate([jnp.broadcast_to(e, (L, e.shape[1])) for e in ends], axis=0)
    p_in = jnp.exp(cs)
    p_inv = jnp.exp(-cs)
    p_end = jnp.exp(cs_end - cs)
    rt = r * p_in
    kt = kk * jnp.exp(cs - lw)
    bt = bb * p_inv
    kft = kf * p_inv
    kh = kf * p_end
    bh = bb * p_end
    p_last = [jnp.exp(e) for e in ends]

    row = lax.broadcasted_iota(jnp.int32, (2 * L, 2 * L), 0)
    col = lax.broadcasted_iota(jnp.int32, (2 * L, 2 * L), 1)
    t_idx = row & (L - 1)
    s_idx = col & (L - 1)
    tril = s_idx < t_idx + row // L
    row1 = lax.broadcasted_iota(jnp.int32, (L, L), 0)
    col1 = lax.broadcasted_iota(jnp.int32, (L, L), 1)
    same_blk = (row1 // RWKV_SOLVE_BLOCK) == (col1 // RWKV_SOLVE_BLOCK)

    pairs = [(s, h) for s in range(n_seq) for h in range(n_heads)]
    hs = range(len(pairs))
    rs = [slice(s * L, (s + 1) * L) for s, _ in pairs]
    sl = [slice(h * HEAD_DIM, (h + 1) * HEAD_DIM) for _, h in pairs]
    s0 = [st_ref[s, h] for s, h in pairs]
    vh = [v[rs[h], sl[h]] for h in hs]
    x = [jnp.concatenate([kt[rs[h], sl[h]], rt[rs[h], sl[h]]], axis=0) for h in hs]
    yk = [jnp.concatenate([bt[rs[h], sl[h]], kft[rs[h], sl[h]]], axis=0) for h in hs]
    gm = [jnp.where(tril, _dot_nt(x[h], yk[h]), 0.0) for h in hs]
    xm = [_dot_nt(x[h], s0[h]) for h in hs]
    rhs = [-(xm[h][:L] + _dot(gm[h][:L, L:], vh[h])) for h in hs]
    ad = [jnp.where(same_blk, gm[h][:L, :L], 0.0) for h in hs]
    z = [jnp.concatenate([gm[h][:L, :L] - ad[h], rhs[h]], axis=1) for h in hs]
    z = [z[h] - _dot(ad[h], z[h]) for h in hs]
    apow = ad
    span = 2
    while span < RWKV_SOLVE_BLOCK:
        apow = [_dot(apow[h], apow[h]) for h in hs]
        z = [z[h] + _dot(apow[h], z[h]) for h in hs]
        span *= 2
    w2 = [_dot(z[h][:, :L], z[h]) for h in hs]
    x1 = [z[h][:, L:] - w2[h][:, L:] for h in hs]
    u = [x1[h] + _dot(w2[h][:, :L], x1[h]) for h in hs]
    uv = [jnp.concatenate([u[h], vh[h]], axis=0) for h in hs]
    ys = [xm[h][L:] + _dot(gm[h][L:], uv[h]) for h in hs]
    for (s, hd), h in zip(pairs, hs):
        bk = jnp.concatenate([bh[rs[h], sl[h]], kh[rs[h], sl[h]]], axis=0)
        st_ref[s, hd] = s0[h] * p_last[s][:, sl[h]] + _dot_tn(uv[h], bk)
    y = jnp.concatenate([jnp.concatenate(ys[s * n_heads:(s + 1) * n_heads], axis=1) for s in range(n_seq)],
                        axis=0)
    mean = _dot(y, avg_ref[...])
    d = y - mean
    var = _dot(d * d, avg_ref[...])
    yn = d * lax.rsqrt(var + GN_EPS) * gng_ref[...] + gnb_ref[...]
    bonus = _dot(r * kf * rk_ref[...], ones_ref[...]) * v
    o_ref[...] = ((yn + bonus) * gate).reshape(o_ref.shape)

    @pl.when(c == pl.num_programs(1) - 1)
    def _():
        sout_ref[...] = st_ref[...]


def _rwkv_chunked(pb, shift, s0, p):
    b, t, c = pb.shape
    d = p["w0"].shape[1]
    n_heads = d // HEAD_DIM
    L = RWKV_CHUNK
    n_seq = math.gcd(RWKV_SEQS_PER_STEP, b)
    assert t % L == 0 and L // RWKV_SOLVE_BLOCK == 4
    blk = lambda w: pl.BlockSpec((n_seq, L, w), lambda bi, ci: (bi, ci, 0))
    sh_spec = pl.BlockSpec((n_seq, 1, c), lambda bi, ci: (bi, 0, 0))
    sspec = pl.BlockSpec((n_seq, n_heads, HEAD_DIM, HEAD_DIM), lambda bi, ci: (bi, 0, 0, 0))
    tri = jnp.kron(jnp.eye(n_seq, dtype=F32), jnp.tril(jnp.ones((L, L), F32))).astype(BF16)
    consts = [p[n] for n in RWKV_PREP_PARAMS] + [p["r_k"], p["gn_g"], p["gn_b"], tri, p["avg"]]
    kern = functools.partial(_rwkv_chunk_kernel, chunk=L, n_heads=n_heads, n_seq=n_seq)
    return pl.pallas_call(
        kern, grid=(b // n_seq, t // L),
        in_specs=[blk(c), sh_spec, sspec] + [_const_spec(a.shape) for a in consts],
        out_specs=[blk(d), sspec],
        out_shape=[jax.ShapeDtypeStruct((b, t, d), F32),
                   jax.ShapeDtypeStruct((b, n_heads, HEAD_DIM, HEAD_DIM), F32)],
        scratch_shapes=[pltpu.VMEM((n_seq, n_heads, HEAD_DIM, HEAD_DIM), F32), pltpu.VMEM((n_seq, 1, c), F32)],
        compiler_params=_cparams(("parallel", "arbitrary")), name="rwkv_chunk")(pb, shift, s0, *consts)


def _rwkv_step_kernel(s_ref, r_ref, k_ref, v_ref, lw_ref, kk_ref, bb_ref, g_ref, rk_ref, gng_ref, gnb_ref,
                      so_ref, o_ref, y_ref):
    r = r_ref[0]
    k = k_ref[0]
    v = v_ref[0]
    w = jnp.exp(lw_ref[0])
    kk = kk_ref[0]
    bb = bb_ref[0]
    for i in range(HEAD_DIM):
        s = s_ref[0, i]
        sa = jnp.sum(s * kk, axis=0, keepdims=True)
        s2 = s * w - sa * bb + v[i:i + 1, :] * k
        so_ref[0, i] = s2
        y_ref[i:i + 1, :] = jnp.sum(s2 * r, axis=0, keepdims=True)
    y = y_ref[...]
    mu = jnp.mean(y, axis=0, keepdims=True)
    d = y - mu
    var = jnp.mean(d * d, axis=0, keepdims=True)
    yn = d * lax.rsqrt(var + GN_EPS) * gng_ref[0] + gnb_ref[0]
    bonus = jnp.sum(r * k * rk_ref[0], axis=0, keepdims=True) * v
    o_ref[0] = (yn + bonus) * g_ref[0]


def _rwkv_step(prep, state, p):
    nb, d = prep[0].shape
    n_heads = d // HEAD_DIM
    to_heads = lambda a: a.reshape(nb, n_heads, HEAD_DIM).transpose(1, 2, 0)
    vecs = [to_heads(a) for a in prep]
    params = [jnp.broadcast_to(p[n].reshape(n_heads, HEAD_DIM, 1), (n_heads, HEAD_DIM, nb))
              for n in ("r_k", "gn_g", "gn_b")]
    st = state.transpose(1, 2, 3, 0)
    sspec = pl.BlockSpec((1, HEAD_DIM, HEAD_DIM, nb), lambda h: (h, 0, 0, 0))
    vspec = pl.BlockSpec((1, HEAD_DIM, nb), lambda h: (h, 0, 0))
    st_new, o = pl.pallas_call(
        _rwkv_step_kernel, grid=(n_heads,),
        in_specs=[sspec] + [vspec] * 10, out_specs=[sspec, vspec],
        out_shape=[jax.ShapeDtypeStruct(st.shape, F32), jax.ShapeDtypeStruct((n_heads, HEAD_DIM, nb), F32)],
        scratch_shapes=[pltpu.VMEM((HEAD_DIM, nb), F32)],
        compiler_params=_cparams(("parallel",)), name="rwkv_step")(st, *vecs, *params)
    return o.transpose(2, 0, 1).reshape(nb, d), st_new.transpose(3, 0, 1, 2)


def _s5_disc_kernel(lr_ref, li_ref, ldt_ref, ar_ref, ai_ref, cr_ref, ci_ref):
    lr = lr_ref[...]
    li = li_ref[...]
    dt = jnp.exp(ldt_ref[...])
    mag = jnp.exp(lr * dt)
    ar = mag * jnp.cos(li * dt)
    ai = mag * jnp.sin(li * dt)
    den = lr * lr + li * li
    ar_ref[...] = ar
    ai_ref[...] = ai
    cr_ref[...] = ((ar - 1.0) * lr + ai * li) / den
    ci_ref[...] = (ai * lr - (ar - 1.0) * li) / den


def _s5_discretize(lam_re, lam_im, log_dt):
    g, p = lam_re.shape
    ldt = jnp.broadcast_to(log_dt.reshape(g, 1), (g, p))
    return pl.pallas_call(_s5_disc_kernel, out_shape=[jax.ShapeDtypeStruct((g, p), F32)] * 4,
                          name="s5_disc")(lam_re, lam_im, ldt)


def _s5_mix_tail(x, y_state, dsk_ref, wo_ref, wgt_ref, g_ref, b_ref, alpha):
    y = y_state + dsk_ref[...] * x
    zg = _gelu_tanh(y).astype(BF16)
    out = (jnp.dot(zg, wo_ref[...], preferred_element_type=F32)
           * jax.nn.sigmoid(jnp.dot(zg, wgt_ref[...], preferred_element_type=F32)))
    return _layer_norm(alpha * x + out, g_ref[...], b_ref[...])


def _s5_scan_kernel(x_ref, wb_ref, wcr_ref, wci_ref, ar_ref, ai_ref, dsk_ref, wo_ref, wgt_ref, g_ref, b_ref,
                    s0r_ref, s0i_ref, perm_ref, permt_ref, o_ref, sr_out, si_out, bur, bui, sr_st, si_st,
                    *, tc, nb, alpha):
    ti = pl.program_id(0)

    @pl.when(ti == 0)
    def _():
        sr_st[...] = s0r_ref[...]
        si_st[...] = s0i_ref[...]

    x = x_ref[...].reshape(nb * tc, x_ref.shape[2])
    xb = jnp.dot(perm_ref[...], x.astype(BF16), preferred_element_type=F32).astype(BF16)
    n_blk = wb_ref.shape[0]
    half = wb_ref.shape[2] // 2
    for j in range(n_blk):
        pj = jnp.dot(xb[:, j * LANES:(j + 1) * LANES], wb_ref[j], preferred_element_type=F32)
        bur[:, j * half:(j + 1) * half] = pj[:, :half]
        bui[:, j * half:(j + 1) * half] = pj[:, half:]
    n_state = bur.shape[1]
    for s in range(n_state // S5_SLAB):
        cols = slice(s * S5_SLAB, (s + 1) * S5_SLAB)
        ar = jnp.broadcast_to(ar_ref[:, cols], (nb, S5_SLAB))
        ai = jnp.broadcast_to(ai_ref[:, cols], (nb, S5_SLAB))

        def step(t, carry):
            sr, si = carry
            rows = pl.ds(pl.multiple_of(t * nb, nb), nb)
            nsr = ar * sr - ai * si + bur[rows, cols]
            nsi = ar * si + ai * sr + bui[rows, cols]
            bur[rows, cols] = nsr
            bui[rows, cols] = nsi
            return nsr, nsi

        sr, si = lax.fori_loop(0, tc, step, (sr_st[:, cols], si_st[:, cols]))
        sr_st[:, cols] = sr
        si_st[:, cols] = si
    ys = []
    for j in range(n_blk):
        ys.append(_dot(bur[:, j * half:(j + 1) * half], wcr_ref[j])
                  + _dot(bui[:, j * half:(j + 1) * half], wci_ref[j]))
    y_tb = jnp.concatenate(ys, axis=1)
    y_state = _dot_sel_lhs(permt_ref[...], y_tb)
    out = _s5_mix_tail(x, y_state, dsk_ref, wo_ref, wgt_ref, g_ref, b_ref, alpha)
    o_ref[...] = out.reshape(o_ref.shape)

    @pl.when(ti == pl.num_programs(0) - 1)
    def _():
        sr_out[...] = sr_st[...]
        si_out[...] = si_st[...]


def _s5_scan(x, s0r, s0i, sp, ln, alpha):
    nb, t, d = x.shape
    tc = min(S5_TIME_CHUNK, t)
    assert t % tc == 0 and nb == SUBLANES
    n_state = s0r.shape[1]
    r = jnp.arange(nb * tc)
    perm = ((r[:, None] // nb == r[None, :] % tc) & (r[:, None] % nb == r[None, :] // tc)).astype(BF16)
    consts = [sp["wb"], sp["wcr"], sp["wci"], sp["ar"], sp["ai"], sp["d_skip"], sp["w_out"], sp["w_gate"],
              ln[0], ln[1], s0r, s0i, perm, perm.T]
    blk = pl.BlockSpec((nb, tc, d), lambda i: (0, i, 0))
    st_spec = pl.BlockSpec((nb, n_state), lambda i: (0, 0))
    kern = functools.partial(_s5_scan_kernel, tc=tc, nb=nb, alpha=alpha)
    return pl.pallas_call(
        kern, grid=(t // tc,),
        in_specs=[blk] + [_const_spec(a.shape) for a in consts],
        out_specs=[blk, st_spec, st_spec],
        out_shape=[jax.ShapeDtypeStruct((nb, t, d), F32), jax.ShapeDtypeStruct((nb, n_state), F32),
                   jax.ShapeDtypeStruct((nb, n_state), F32)],
        scratch_shapes=[pltpu.VMEM((tc * nb, n_state), F32), pltpu.VMEM((tc * nb, n_state), F32),
                        pltpu.VMEM((nb, n_state), F32), pltpu.VMEM((nb, n_state), F32)],
        compiler_params=_cparams(("arbitrary",)), name="s5_scan")(x, *consts)


def _s5_step_kernel(x_ref, wb_ref, wcr_ref, wci_ref, ar_ref, ai_ref, dsk_ref, wo_ref, wgt_ref, g_ref, b_ref,
                    s0r_ref, s0i_ref, o_ref, sr_out, si_out, *, alpha):
    x = x_ref[...]
    xb = x.astype(BF16)
    n_blk = wb_ref.shape[0]
    half = wb_ref.shape[2] // 2
    ys = []
    for j in range(n_blk):
        cols = slice(j * half, (j + 1) * half)
        pj = jnp.dot(xb[:, j * LANES:(j + 1) * LANES], wb_ref[j], preferred_element_type=F32)
        ar = ar_ref[:, cols]
        ai = ai_ref[:, cols]
        sr = s0r_ref[:, cols]
        si = s0i_ref[:, cols]
        nsr = ar * sr - ai * si + pj[:, :half]
        nsi = ar * si + ai * sr + pj[:, half:]
        sr_out[:, cols] = nsr
        si_out[:, cols] = nsi
        ys.append(_dot(nsr, wcr_ref[j]) + _dot(nsi, wci_ref[j]))
    y_state = jnp.concatenate(ys, axis=1)
    o_ref[...] = _s5_mix_tail(x, y_state, dsk_ref, wo_ref, wgt_ref, g_ref, b_ref, alpha)


def _s5_step(x, s0r, s0i, sp, ln, alpha):
    m, d = x.shape
    n_state = s0r.shape[1]
    args = [x, sp["wb"], sp["wcr"], sp["wci"], sp["ar"], sp["ai"], sp["d_skip"], sp["w_out"], sp["w_gate"],
            ln[0], ln[1], s0r, s0i]
    kern = functools.partial(_s5_step_kernel, alpha=alpha)
    return pl.pallas_call(
        kern,
        out_shape=[jax.ShapeDtypeStruct((m, d), F32), jax.ShapeDtypeStruct((m, n_state), F32),
                   jax.ShapeDtypeStruct((m, n_state), F32)],
        compiler_params=pltpu.CompilerParams(vmem_limit_bytes=VMEM_LIMIT_BYTES), name="s5_step")(*args)


def _s5_weights(prm, i):
    ar, ai, cr, ci = _s5_discretize(prm["lam_re"][i], prm["lam_im"][i], prm["log_dt"][i])
    b_re, b_im = prm["b_re"][i], prm["b_im"][i]
    bbr = cr[..., None] * b_re - ci[..., None] * b_im
    bbi = cr[..., None] * b_im + ci[..., None] * b_re
    n_groups, n_state, n_ch = bbr.shape
    gpb = LANES // n_ch
    n_blk = n_groups // gpb
    eye = jnp.eye(gpb, dtype=F32)

    def blk_in(bb):
        w = jnp.einsum("jgpc,gh->jgchp", bb.reshape(n_blk, gpb, n_state, n_ch), eye)
        return w.reshape(n_blk, LANES, gpb * n_state)

    def blk_out(cc):
        w = jnp.einsum("jgcp,gh->jhpgc", cc.reshape(n_blk, gpb, n_ch, n_state), eye)
        return w.reshape(n_blk, gpb * n_state, LANES)

    return dict(
        wb=jnp.concatenate([blk_in(bbr), blk_in(bbi)], axis=-1).astype(BF16),
        wcr=blk_out(prm["c_re"][i]).astype(BF16), wci=(-blk_out(prm["c_im"][i])).astype(BF16),
        ar=ar.reshape(1, -1), ai=ai.reshape(1, -1), d_skip=prm["d_skip"][i].reshape(1, -1),
        w_out=prm["w_glu_out"][i].astype(BF16), w_gate=prm["w_glu_gate"][i].astype(BF16))


def _rwkv_weights(prm, i, d_rwkv):
    rank_d = prm["w_w2"].shape[1]
    rank_a = prm["w_a2"].shape[1]
    assert rank_d + rank_a == LANES
    zeros_d = jnp.zeros((rank_d, d_rwkv), F32)
    zeros_a = jnp.zeros((rank_a, d_rwkv), F32)
    head = jnp.arange(d_rwkv) // HEAD_DIM
    ones = (head[:, None] == head[None, :]).astype(F32)
    row = lambda a: a.reshape(1, -1).astype(F32)
    return dict(
        mu=row(prm["mu_shift"][i]), w0=row(prm["w0"][i]), a0=row(prm["a0"][i]), k_k=row(prm["k_k"][i]),
        k_a=row(prm["k_a"][i]), r_k=row(prm["r_k"][i]), gn_g=row(prm["gn_g"][i]), gn_b=row(prm["gn_b"][i]),
        w_w2=jnp.concatenate([prm["w_w2"][i], zeros_a], axis=0).astype(BF16),
        w_a2=jnp.concatenate([zeros_d, prm["w_a2"][i]], axis=0).astype(BF16),
        w_g2=prm["w_g2"][i].astype(BF16), ones=ones.astype(BF16), avg=(ones / HEAD_DIM).astype(BF16))


def _trunk(x, cache_k, cache_v, page_table, wkv0, shift0, s50, prm, w):
    bsz, t, d = x.shape
    depth = prm["ln_g"].shape[0]
    alpha = (2.0 * depth) ** 0.25
    d_sb = prm["sb_bias"].shape[1] * HEAD_DIM
    d_rwkv = prm["w0"].shape[1]
    splits = (0, d_sb, 2 * d_sb, 3 * d_sb, prm["w_in_even"].shape[2])
    ln = lambda layer, j: (prm["ln_g"][layer, j].reshape(1, d), prm["ln_b"][layer, j].reshape(1, d))
    prompt = cache_k is None
    rows = x.reshape(bsz * t, d)
    out_k, out_v, out_wkv, out_shift, out_s5 = [], [], [], [], []
    for layer in range(depth):
        i = layer // 2
        ffn1 = (w["ffn1_wg"][layer], w["ffn1_wu"][layer], w["ffn1_wd"][layer])
        ffn2 = (w["ffn2_wg"][layer], w["ffn2_wu"][layer], w["ffn2_wd"][layer])
        if layer % 2 == 0:
            group_rows = t if prompt else bsz
            x1, q, k, v, pb, k_fm, v_fm = _stage(rows, ffn1, ln(layer, 0), alpha, w_in=w["w_in_even"][i],
                                                 proj_splits=splits, proj_transposed=(1, 2),
                                                 group_rows=group_rows)
            rp = _rwkv_weights(prm, i, d_rwkv)
            pb3 = pb.reshape(bsz, t, -1)
            if prompt:
                o_sb = _sb_prompt(q.reshape(bsz, t, d_sb), k.reshape(bsz, t, d_sb), v.reshape(bsz, t, d_sb),
                                  prm["sb_bias"][i]).reshape(bsz * t, d_sb)
                o_rwkv, wkv = _rwkv_chunked(pb3, shift0[i][:, None, :], wkv0[i], rp)
                o_rwkv = o_rwkv.reshape(bsz * t, d_rwkv)
            else:
                assert t == 1
                n_heads = d_sb // HEAD_DIM
                o_sb = _sb_sample(q.reshape(bsz, n_heads, HEAD_DIM), cache_k[i], cache_v[i], page_table,
                                  prm["sb_bias"][i]).reshape(bsz, d_sb)
                o_rwkv, wkv = _rwkv_step(_rwkv_prep(pb, shift0[i], rp), wkv0[i], rp)
            for fm, out in ((k_fm, out_k), (v_fm, out_v)):
                fm = fm.reshape(-1, d_sb // HEAD_DIM, HEAD_DIM, group_rows)
                if prompt:
                    out.append(fm.transpose(0, 3, 1, 2))
                else:
                    out.append(fm[0].transpose(2, 0, 1)[:, None])
            out_wkv.append(wkv)
            out_shift.append(pb3[:, -1])
            w_out = w["w_out_even"][i]
            mix = (o_sb, o_rwkv, w_out[:d_sb], w_out[d_sb:], *ln(layer, 1))
            (rows,) = _stage(x1, ffn2, ln(layer, 2), alpha, mix=mix)
        else:
            sp = _s5_weights(prm, i)
            s0r = s50[i][..., 0].reshape(bsz, -1)
            s0i = s50[i][..., 1].reshape(bsz, -1)
            (x1,) = _stage(rows, ffn1, ln(layer, 0), alpha)
            if t > 1:
                x2, sr, si = _s5_scan(x1.reshape(bsz, t, d), s0r, s0i, sp, ln(layer, 1), alpha)
                x2 = x2.reshape(bsz * t, d)
            else:
                x2, sr, si = _s5_step(x1, s0r, s0i, sp, ln(layer, 1), alpha)
            n_groups = prm["lam_re"].shape[1]
            out_s5.append(jnp.stack([sr.reshape(bsz, n_groups, -1), si.reshape(bsz, n_groups, -1)], axis=-1))
            (rows,) = _stage(x2, ffn2, ln(layer, 2), alpha)
    y = rows.reshape(bsz, t, d)
    return (y, jnp.stack(out_k), jnp.stack(out_v), jnp.stack(out_wkv), jnp.stack(out_shift),
            jnp.stack(out_s5))


def kernel(x_prompt, x_sample, cache_k_sb, cache_v_sb, page_table, state_wkv, state_shift, state_s5, ln_g, ln_b, ffn1_wg, ffn1_wu, ffn1_wd, ffn2_wg, ffn2_wu, ffn2_wd, w_in_even, w_out_even, sb_bias, mu_shift, w0, w_w2, a0, w_a2, w_g2, k_k, k_a, r_k, gn_g, gn_b, lam_re, lam_im, log_dt, b_re, b_im, c_re, c_im, d_skip, w_glu_out, w_glu_gate):
    prm = dict(ln_g=ln_g, ln_b=ln_b, sb_bias=sb_bias, mu_shift=mu_shift, w0=w0, w_w2=w_w2, a0=a0, w_a2=w_a2,
               w_g2=w_g2, k_k=k_k, k_a=k_a, r_k=r_k.reshape(r_k.shape[0], -1), gn_g=gn_g, gn_b=gn_b,
               lam_re=lam_re, lam_im=lam_im, log_dt=log_dt, b_re=b_re, b_im=b_im, c_re=c_re, c_im=c_im,
               d_skip=d_skip, w_glu_out=w_glu_out, w_glu_gate=w_glu_gate, w_in_even=w_in_even)
    w = dict(ffn1_wg=ffn1_wg, ffn1_wu=ffn1_wu, ffn1_wd=ffn1_wd, ffn2_wg=ffn2_wg, ffn2_wu=ffn2_wu,
             ffn2_wd=ffn2_wd, w_in_even=w_in_even, w_out_even=w_out_even)
    w = {name: a.astype(BF16) for name, a in w.items()}
    nb = x_prompt.shape[0]
    n_even, n_odd = state_wkv.shape[0], state_s5.shape[0]
    wkv0 = jnp.zeros((n_even, nb) + state_wkv.shape[2:], F32)
    shift0 = jnp.zeros((n_even, nb, state_shift.shape[2]), F32)
    s50 = jnp.zeros((n_odd, nb) + state_s5.shape[2:], F32)
    y_p, p_k, p_v, p_wkv, p_shift, p_s5 = _trunk(x_prompt, None, None, None, wkv0, shift0, s50, prm, w)
    cache_kt = cache_k_sb.transpose(0, 1, 3, 4, 2)
    cache_vt = cache_v_sb.transpose(0, 1, 3, 4, 2)
    y_s, s_k, s_v, s_wkv, s_shift, s_s5 = _trunk(x_sample, cache_kt, cache_vt, page_table, state_wkv,
                                                 state_shift, state_s5, prm, w)
    return (y_p, y_s, p_k, p_v, p_wkv, p_shift, p_s5, s_k, s_v, s_wkv, s_shift, s_s5)
```

```python
import functools
import math

import jax
import jax.numpy as jnp
from jax import lax
from jax.experimental import pallas as pl
from jax.experimental.pallas import tpu as pltpu

F32 = jnp.float32
BF16 = jnp.bfloat16

HEAD_DIM = 64
LOG2_E = 1.0 / math.log(2.0)
LN_EPS = 1e-5
GN_EPS = 64e-5
S5_GROUP = 16
S5_STATE = 64
LANES = 128
SUBLANES = 8
VMEM_LIMIT_BYTES = 56 * 1024 * 1024

ROW_TILE = 512
FF_CHUNKS = 2
SB_TILE_Q = 512
SB_TILE_K = 256
SB_PAGES_PER_STEP = 16
RWKV_CHUNK = 64
RWKV_SOLVE_BLOCK = 16
RWKV_SEQS_PER_STEP = 4
S5_TIME_CHUNK = 32
S5_SLAB = 1024


def _cparams(sem):
    return pltpu.CompilerParams(dimension_semantics=sem, vmem_limit_bytes=VMEM_LIMIT_BYTES)


def _const_spec(shape):
    nd = len(shape)
    return pl.BlockSpec(shape, lambda *_: (0,) * nd, pipeline_mode=pl.Buffered(1))


def _dot(a, b):
    return jnp.dot(a.astype(BF16), b.astype(BF16), preferred_element_type=F32)


def _dot_nt(a, b):
    return lax.dot_general(a.astype(BF16), b.astype(BF16), (((1,), (1,)), ((), ())),
                           preferred_element_type=F32)


def _dot_tn(a, b):
    return lax.dot_general(a.astype(BF16), b.astype(BF16), (((0,), (0,)), ((), ())),
                           preferred_element_type=F32)


def _split_bf16(a):
    hi = a.astype(BF16)
    lo = (a - hi.astype(F32)).astype(BF16)
    return hi, lo


def _dot_sel_rhs(a, sel):
    hi, lo = _split_bf16(a)
    return (jnp.dot(hi, sel, preferred_element_type=F32) + jnp.dot(lo, sel, preferred_element_type=F32))


def _dot_sel_lhs(sel, a):
    hi, lo = _split_bf16(a)
    return (jnp.dot(sel, hi, preferred_element_type=F32) + jnp.dot(sel, lo, preferred_element_type=F32))


def _layer_norm(y, g, b):
    mu = jnp.mean(y, axis=-1, keepdims=True)
    d = y - mu
    var = jnp.mean(d * d, axis=-1, keepdims=True)
    return d * lax.rsqrt(var + LN_EPS) * g + b


def _softplus(z):
    return jnp.maximum(z, 0.0) + jnp.log1p(jnp.exp(-jnp.abs(z)))


def _gelu_tanh(y):
    c = math.sqrt(2.0 / math.pi)
    return 0.5 * y * (1.0 + jnp.tanh(c * (y + 0.044715 * (y * y * y))))


def _stage_kernel(*refs, alpha, has_mix, proj_splits, proj_transposed):
    it = iter(refs)
    x_ref = next(it)
    if has_mix:
        a_ref, b_ref, wa_ref, wb_ref, gm_ref, bm_ref = (next(it) for _ in range(6))
    wg_ref, wu_ref, wd_ref, g_ref, b2_ref = (next(it) for _ in range(5))
    if proj_splits:
        win_ref = next(it)
    o_ref = next(it)
    proj_refs = [next(it) for _ in proj_splits[1:]] if proj_splits else []
    proj_t_refs = [next(it) for _ in proj_transposed]

    x = x_ref[...]
    if has_mix:
        mix = _dot(a_ref[...], wa_ref[...]) + _dot(b_ref[...], wb_ref[...])
        x = _layer_norm(alpha * x + mix, gm_ref[...], bm_ref[...])
    xb = x.astype(BF16)
    ff = wg_ref.shape[1]
    fc = ff // FF_CHUNKS
    acc = None
    for c in range(0, ff, fc):
        gate = jnp.dot(xb, wg_ref[:, c:c + fc], preferred_element_type=F32)
        up = jnp.dot(xb, wu_ref[:, c:c + fc], preferred_element_type=F32)
        h = (gate * jax.nn.sigmoid(gate) * up).astype(BF16)
        part = jnp.dot(h, wd_ref[c:c + fc, :], preferred_element_type=F32)
        acc = part if acc is None else acc + part
    y = _layer_norm(alpha * x + 0.5 * acc, g_ref[...], b2_ref[...])
    o_ref[...] = y
    if proj_splits:
        proj = jnp.dot(y.astype(BF16), win_ref[...], preferred_element_type=F32)
        for r, lo, hi in zip(proj_refs, proj_splits[:-1], proj_splits[1:]):
            r[...] = proj[:, lo:hi]
        for r, j in zip(proj_t_refs, proj_transposed):
            r[0] = proj[:, proj_splits[j]:proj_splits[j + 1]].T


def _stage(x, ffn, ln_ffn, alpha, mix=None, w_in=None, proj_splits=(), proj_transposed=(), group_rows=None):
    m, d = x.shape
    tm = min(ROW_TILE, m, group_rows or m)
    assert m % tm == 0
    row = lambda w: pl.BlockSpec((tm, w), lambda i: (i, 0))
    args, specs = [x], [row(d)]
    if mix is not None:
        a, b, wa, wb, gm, bm = mix
        args += [a, b, wa, wb, gm, bm]
        specs += [row(a.shape[1]), row(b.shape[1]), _const_spec(wa.shape), _const_spec(wb.shape),
                  _const_spec(gm.shape), _const_spec(bm.shape)]
    args += [*ffn, *ln_ffn]
    specs += [_const_spec(w.shape) for w in (*ffn, *ln_ffn)]
    out_shape = [jax.ShapeDtypeStruct((m, d), F32)]
    out_specs = [row(d)]
    if proj_splits:
        args.append(w_in)
        specs.append(_const_spec(w_in.shape))
        for lo, hi in zip(proj_splits[:-1], proj_splits[1:]):
            out_shape.append(jax.ShapeDtypeStruct((m, hi - lo), F32))
            out_specs.append(row(hi - lo))
        for j in proj_transposed:
            width = proj_splits[j + 1] - proj_splits[j]
            tiles = group_rows // tm
            assert group_rows % tm == 0 and m % group_rows == 0
            out_shape.append(jax.ShapeDtypeStruct((m // group_rows, width, group_rows), F32))
            out_specs.append(pl.BlockSpec((1, width, tm), lambda i, tiles=tiles: (i // tiles, 0, i % tiles)))
    kern = functools.partial(_stage_kernel, alpha=alpha, has_mix=mix is not None,
                             proj_splits=tuple(proj_splits), proj_transposed=tuple(proj_transposed))
    outs = pl.pallas_call(
        kern, grid=(m // tm,), in_specs=specs, out_specs=out_specs, out_shape=out_shape,
        compiler_params=_cparams(("parallel",)), name="stage")(*args)
    return outs


def _sb_prompt_kernel(bias_ref, q_ref, k_ref, v_ref, o_ref, *, tq, tk, scale):
    hp = pl.program_id(1)
    qi = pl.program_id(2)
    q = q_ref[0] * (scale * LOG2_E)
    lane = lax.broadcasted_iota(jnp.int32, q.shape, 1)
    kr = lax.broadcasted_iota(jnp.int32, (tk, tk), 0)
    kc = lax.broadcasted_iota(jnp.int32, (tk, tk), 1)
    neg_after = jnp.where(kr > kc, -1.0, 0.0).astype(BF16)
    causal = kc < kr
    heads = range(LANES // HEAD_DIM)
    qh = [jnp.where((lane >= h * HEAD_DIM) & (lane < (h + 1) * HEAD_DIM), q, 0.0).astype(BF16) for h in heads]
    bias = [bias_ref[2 * hp + h] * LOG2_E for h in heads]
    band = tq // tk
    qblk = [[qh[h][j * tk:(j + 1) * tk] for h in heads] for j in range(band)]

    def tile_step(kb, qs, carry, masked=False):
        start = pl.multiple_of(kb * tk, tk)
        ks = k_ref[0, pl.ds(start, tk), :].astype(BF16)
        vs = v_ref[0, pl.ds(start, tk), :].astype(BF16)
        out = []
        for h in heads:
            acc, run = carry[2 * h], carry[2 * h + 1]
            z = lax.dot_general(qs[h], ks, (((1,), (1,)), ((), ())), preferred_element_type=F32) + bias[h]
            sp = jnp.maximum(z, 0.0) + jnp.log2(1.0 + jnp.exp2(-jnp.abs(z)))
            spm = jnp.where(causal, sp, 0.0) if masked else sp
            later = jnp.dot(spm.astype(BF16), neg_after, preferred_element_type=F32)
            w = jnp.exp2(z - sp + later + run)
            if masked:
                w = jnp.where(causal, w, 0.0)
            out.append(acc + jnp.dot(w.astype(BF16), vs, preferred_element_type=F32))
            out.append(run - jnp.sum(spm, axis=1, keepdims=True))
        return tuple(out)

    blocks = []
    for j in range(band):
        cr = (jnp.zeros((tk, LANES), F32), jnp.zeros((tk, 1), F32)) * len(heads)
        cr = tile_step(qi * band + j, qblk[j], cr, masked=True)
        for jj in reversed(range(j)):
            cr = tile_step(qi * band + jj, qblk[j], cr)
        blocks.append(cr)
    carry = tuple(jnp.concatenate([blk[n] for blk in blocks], axis=0) for n in range(2 * len(heads)))

    def band_step(i, cr):
        for m in range(band):
            cr = tile_step((qi - 1 - i) * band + (band - 1 - m), qh, cr)
        return cr

    carry = lax.fori_loop(0, qi, band_step, carry)
    o_ref[0] = jnp.where(lane < HEAD_DIM, carry[0], carry[2])


def _sb_prompt(q, k, v, bias):
    b, t, hd = q.shape
    tq = min(SB_TILE_Q, t)
    tk = min(SB_TILE_K, t)
    assert t % tq == 0 and tq % tk == 0 and hd % LANES == 0
    kern = functools.partial(_sb_prompt_kernel, tq=tq, tk=tk, scale=HEAD_DIM ** -0.5)
    qspec = pl.BlockSpec((1, tq, LANES), lambda bi, hp, qi: (bi, qi, hp))
    kvspec = pl.BlockSpec((1, t, LANES), lambda bi, hp, qi: (bi, 0, hp))
    return pl.pallas_call(
        kern, grid=(b, hd // LANES, t // tq),
        in_specs=[pl.BlockSpec(memory_space=pltpu.SMEM), qspec, kvspec, kvspec],
        out_specs=qspec, out_shape=jax.ShapeDtypeStruct((b, t, hd), F32),
        compiler_params=_cparams(("parallel", "parallel", "arbitrary")), name="sb_prompt")(bias, q, k, v)


def _sb_sample_kernel(pt_ref, q_ref, bias_ref, after_ref, prior_ref, *refs, pages, n_heads, scale):
    k_refs = refs[:pages]
    v_refs = refs[pages:2 * pages]
    o_ref = refs[2 * pages]
    qt_ref, acc_ref, run_ref, zw_ref = refs[2 * pages + 1:]
    g = pl.program_id(1)
    page = k_refs[0].shape[2]

    @pl.when(g == 0)
    def _():
        q = q_ref[...] * scale
        sub = lax.broadcasted_iota(jnp.int32, q.shape, 0)
        ones = jnp.ones((n_heads, page), BF16)
        for h in range(n_heads):
            qt_ref[h] = _dot_tn(jnp.where(sub == h, q, 0.0), ones)
        acc_ref[...] = jnp.zeros_like(acc_ref)
        run_ref[...] = jnp.zeros_like(run_ref)

    for u in range(pages):
        for h in range(n_heads):
            zw_ref[u * n_heads + h:u * n_heads + h + 1, :] = jnp.sum(k_refs[u][h] * qt_ref[h], axis=0,
                                                                     keepdims=True)
    z = zw_ref[...] + bias_ref[...]
    sp = _softplus(z)
    later = _dot_sel_rhs(-sp, after_ref[...])
    tot = jnp.broadcast_to(jnp.sum(sp, axis=1, keepdims=True), sp.shape)
    passed = _dot_sel_lhs(prior_ref[...], tot)
    run = jnp.concatenate([run_ref[...]] * pages, axis=0) - passed
    zw_ref[...] = jnp.exp(z - sp + later + run)
    last = slice((pages - 1) * n_heads, pages * n_heads)
    run_ref[...] = run[last] - tot[last]
    for h in range(n_heads):
        acc = acc_ref[h]
        for u in range(pages):
            acc = acc + v_refs[u][h] * zw_ref[u * n_heads + h:u * n_heads + h + 1, :]
        acc_ref[h] = acc

    @pl.when(g == pl.num_programs(1) - 1)
    def _():
        ones = jnp.ones((SUBLANES, page), BF16)
        rows = []
        for h in range(n_heads):
            hi, lo = _split_bf16(acc_ref[h])
            rows.append((_dot_nt(ones, hi) + _dot_nt(ones, lo))[0:1])
        o_ref[...] = jnp.concatenate(rows, axis=0)


def _sb_sample(q, cache_k, cache_v, page_table, bias):
    nb, n_heads, _ = q.shape
    n_pages = page_table.shape[1]
    page = cache_k.shape[3]
    pages = min(SB_PAGES_PER_STEP, n_pages)
    assert n_pages % pages == 0
    rows = pages * n_heads
    bias_b = jnp.broadcast_to(jnp.tile(bias.astype(F32), pages)[:, None], (rows, page))
    key = jnp.arange(page)
    after = (key[:, None] > key[None, :]).astype(BF16)
    r = jnp.arange(rows)
    prior = ((r[:, None] % n_heads == r[None, :] % n_heads)
             & (r[None, :] // n_heads < r[:, None] // n_heads)).astype(BF16)

    def page_spec(u):
        def imap(b, g, pt):
            return (pt[b, n_pages - 1 - (g * pages + u)], 0, 0, 0)
        return pl.BlockSpec((None, n_heads, HEAD_DIM, page), imap)

    qspec = pl.BlockSpec((None, n_heads, HEAD_DIM), lambda b, g, pt: (b, 0, 0))
    const = lambda shape: pl.BlockSpec(shape, lambda b, g, pt: (0, 0))
    grid_spec = pltpu.PrefetchScalarGridSpec(
        num_scalar_prefetch=1, grid=(nb, n_pages // pages),
        in_specs=[qspec, const((rows, page)), const((page, page)), const((rows, rows))]
                 + [page_spec(u) for u in range(pages)] * 2,
        out_specs=qspec,
        scratch_shapes=[pltpu.VMEM((n_heads, HEAD_DIM, page), F32), pltpu.VMEM((n_heads, HEAD_DIM, page), F32),
                        pltpu.VMEM((n_heads, page), F32), pltpu.VMEM((rows, page), F32)])
    kern = functools.partial(_sb_sample_kernel, pages=pages, n_heads=n_heads, scale=HEAD_DIM ** -0.5)
    return pl.pallas_call(
        kern, grid_spec=grid_spec, out_shape=jax.ShapeDtypeStruct((nb, n_heads, HEAD_DIM), F32),
        compiler_params=_cparams(("parallel", "arbitrary")), name="sb_sample")(
            page_table, q, bias_b, after, prior, *([cache_k] * pages), *([cache_v] * pages))


RWKV_PREP_PARAMS = ("mu", "w0", "a0", "k_k", "k_a", "w_w2", "w_a2", "w_g2", "ones")


def _rwkv_prep_math(pb, prev, mu_ref, w0_ref, a0_ref, kk_ref, ka_ref, ww2_ref, wa2_ref, wg2_ref, ones_ref):
    pm = pb + mu_ref[...] * (prev - pb)
    d = w0_ref.shape[1]
    r = pm[:, 0:d]
    kr = pm[:, d:2 * d]
    vr = pm[:, 2 * d:3 * d]
    wad = pm[:, 3 * d:3 * d + LANES]
    gd = pm[:, 3 * d + LANES:]
    z_w = w0_ref[...] + _dot(jnp.tanh(wad), ww2_ref[...])
    log_decay = -math.exp(-0.5) * jax.nn.sigmoid(z_w)
    iclr = jax.nn.sigmoid(a0_ref[...] + _dot(wad, wa2_ref[...]))
    gate = _dot(jax.nn.sigmoid(gd), wg2_ref[...])
    kkr = kr * kk_ref[...]
    ss = _dot(kkr * kkr, ones_ref[...])
    kk = kkr * lax.rsqrt(jnp.maximum(ss, 1e-24))
    kf = kr * (1.0 + (iclr - 1.0) * ka_ref[...])
    return r, kf, vr, log_decay, kk, kk * iclr, gate


def _rwkv_prep_kernel(pb_ref, prev_ref, *refs):
    n = len(RWKV_PREP_PARAMS)
    outs = _rwkv_prep_math(pb_ref[...], prev_ref[...], *refs[:n])
    for o_ref, val in zip(refs[n:], outs):
        o_ref[...] = val


def _rwkv_prep(pb, prev, p):
    rows, c = pb.shape
    d = p["w0"].shape[1]
    consts = [p[n] for n in RWKV_PREP_PARAMS]
    return pl.pallas_call(
        _rwkv_prep_kernel, out_shape=[jax.ShapeDtypeStruct((rows, d), F32)] * 7,
        compiler_params=pltpu.CompilerParams(vmem_limit_bytes=VMEM_LIMIT_BYTES), name="rwkv_prep")(
            pb, prev, *consts)


def _rwkv_chunk_kernel(pb_ref, sh_ref, s0_ref, *refs, chunk, n_heads, n_seq):
    n_prep = len(RWKV_PREP_PARAMS)
    prep_refs = refs[:n_prep]
    rk_ref, gng_ref, gnb_ref, tri_ref, avg_ref, o_ref, sout_ref, st_ref, carry_ref = refs[n_prep:]
    ones_ref = prep_refs[-1]
    L = chunk
    c = pl.program_id(1)

    @pl.when(c == 0)
    def _():
        st_ref[...] = s0_ref[...]
        carry_ref[...] = sh_ref[...]

    pb = pb_ref[...].reshape(n_seq * L, pb_ref.shape[2])
    row = lax.broadcasted_iota(jnp.int32, pb.shape, 0)
    prev = pltpu.roll(pb, 1, axis=0)
    for s in range(n_seq):
        prev = jnp.where(row == s * L, carry_ref[s], prev)
        carry_ref[s] = pb[(s + 1) * L - 1:(s + 1) * L, :]
    r, kf, v, lw, kk, bb, gate = _rwkv_prep_math(pb, prev, *prep_refs)
    cs = _dot_sel_lhs(tri_ref[...], lw)
    ends = [cs[(s + 1) * L - 1:(s + 1) * L, :] for s in range(n_seq)]
    cs_end = jnp.concatenate([jnp.broadcast_to(e, (L, e.shape[1])) for e in ends], axis=0)
    p_in = jnp.exp(cs)
    p_inv = jnp.exp(-cs)
    p_end = jnp.exp(cs_end - cs)
    rt = r * p_in
    kt = kk * jnp.exp(cs - lw)
    bt = bb * p_inv
    kft = kf * p_inv
    kh = kf * p_end
    bh = bb * p_end
    p_last = [jnp.exp(e) for e in ends]

    row = lax.broadcasted_iota(jnp.int32, (2 * L, 2 * L), 0)
    col = lax.broadcasted_iota(jnp.int32, (2 * L, 2 * L), 1)
    t_idx = row & (L - 1)
    s_idx = col & (L - 1)
    tril = s_idx < t_idx + row // L
    row1 = lax.broadcasted_iota(jnp.int32, (L, L), 0)
    col1 = lax.broadcasted_iota(jnp.int32, (L, L), 1)
    same_blk = (row1 // RWKV_SOLVE_BLOCK) == (col1 // RWKV_SOLVE_BLOCK)

    pairs = [(s, h) for s in range(n_seq) for h in range(n_heads)]
    hs = range(len(pairs))
    rs = [slice(s * L, (s + 1) * L) for s, _ in pairs]
    sl = [slice(h * HEAD_DIM, (h + 1) * HEAD_DIM) for _, h in pairs]
    s0 = [st_ref[s, h] for s, h in pairs]
    vh = [v[rs[h], sl[h]] for h in hs]
    x = [jnp.concatenate([kt[rs[h], sl[h]], rt[rs[h], sl[h]]], axis=0) for h in hs]
    yk = [jnp.concatenate([bt[rs[h], sl[h]], kft[rs[h], sl[h]]], axis=0) for h in hs]
    gm = [jnp.where(tril, _dot_nt(x[h], yk[h]), 0.0) for h in hs]
    xm = [_dot_nt(x[h], s0[h]) for h in hs]
    rhs = [-(xm[h][:L] + _dot(gm[h][:L, L:], vh[h])) for h in hs]
    ad = [jnp.where(same_blk, gm[h][:L, :L], 0.0) for h in hs]
    z = [jnp.concatenate([gm[h][:L, :L] - ad[h], rhs[h]], axis=1) for h in hs]
    z = [z[h] - _dot(ad[h], z[h]) for h in hs]
    apow = ad
    span = 2
    while span < RWKV_SOLVE_BLOCK:
        apow = [_dot(apow[h], apow[h]) for h in hs]
        z = [z[h] + _dot(apow[h], z[h]) for h in hs]
        span *= 2
    w2 = [_dot(z[h][:, :L], z[h]) for h in hs]
    x1 = [z[h][:, L:] - w2[h][:, L:] for h in hs]
    u = [x1[h] + _dot(w2[h][:, :L], x1[h]) for h in hs]
    uv = [jnp.concatenate([u[h], vh[h]], axis=0) for h in hs]
    ys = [xm[h][L:] + _dot(gm[h][L:], uv[h]) for h in hs]
    for (s, hd), h in zip(pairs, hs):
        bk = jnp.concatenate([bh[rs[h], sl[h]], kh[rs[h], sl[h]]], axis=0)
        st_ref[s, hd] = s0[h] * p_last[s][:, sl[h]] + _dot_tn(uv[h], bk)
    y = jnp.concatenate([jnp.concatenate(ys[s * n_heads:(s + 1) * n_heads], axis=1) for s in range(n_seq)],
                        axis=0)
    mean = _dot(y, avg_ref[...])
    d = y - mean
    var = _dot(d * d, avg_ref[...])
    yn = d * lax.rsqrt(var + GN_EPS) * gng_ref[...] + gnb_ref[...]
    bonus = _dot(r * kf * rk_ref[...], ones_ref[...]) * v
    o_ref[...] = ((yn + bonus) * gate).reshape(o_ref.shape)

    @pl.when(c == pl.num_programs(1) - 1)
    def _():
        sout_ref[...] = st_ref[...]


def _rwkv_chunked(pb, shift, s0, p):
    b, t, c = pb.shape
    d = p["w0"].shape[1]
    n_heads = d // HEAD_DIM
    L = RWKV_CHUNK
    n_seq = math.gcd(RWKV_SEQS_PER_STEP, b)
    assert t % L == 0 and L // RWKV_SOLVE_BLOCK == 4
    blk = lambda w: pl.BlockSpec((n_seq, L, w), lambda bi, ci: (bi, ci, 0))
    sh_spec = pl.BlockSpec((n_seq, 1, c), lambda bi, ci: (bi, 0, 0))
    sspec = pl.BlockSpec((n_seq, n_heads, HEAD_DIM, HEAD_DIM), lambda bi, ci: (bi, 0, 0, 0))
    tri = jnp.kron(jnp.eye(n_seq, dtype=F32), jnp.tril(jnp.ones((L, L), F32))).astype(BF16)
    consts = [p[n] for n in RWKV_PREP_PARAMS] + [p["r_k"], p["gn_g"], p["gn_b"], tri, p["avg"]]
    kern = functools.partial(_rwkv_chunk_kernel, chunk=L, n_heads=n_heads, n_seq=n_seq)
    return pl.pallas_call(
        kern, grid=(b // n_seq, t // L),
        in_specs=[blk(c), sh_spec, sspec] + [_const_spec(a.shape) for a in consts],
        out_specs=[blk(d), sspec],
        out_shape=[jax.ShapeDtypeStruct((b, t, d), F32),
                   jax.ShapeDtypeStruct((b, n_heads, HEAD_DIM, HEAD_DIM), F32)],
        scratch_shapes=[pltpu.VMEM((n_seq, n_heads, HEAD_DIM, HEAD_DIM), F32), pltpu.VMEM((n_seq, 1, c), F32)],
        compiler_params=_cparams(("parallel", "arbitrary")), name="rwkv_chunk")(pb, shift, s0, *consts)


def _rwkv_step_kernel(s_ref, r_ref, k_ref, v_ref, lw_ref, kk_ref, bb_ref, g_ref, rk_ref, gng_ref, gnb_ref,
                      so_ref, o_ref, y_ref):
    r = r_ref[0]
    k = k_ref[0]
    v = v_ref[0]
    w = jnp.exp(lw_ref[0])
    kk = kk_ref[0]
    bb = bb_ref[0]
    for i in range(HEAD_DIM):
        s = s_ref[0, i]
        sa = jnp.sum(s * kk, axis=0, keepdims=True)
        s2 = s * w - sa * bb + v[i:i + 1, :] * k
        so_ref[0, i] = s2
        y_ref[i:i + 1, :] = jnp.sum(s2 * r, axis=0, keepdims=True)
    y = y_ref[...]
    mu = jnp.mean(y, axis=0, keepdims=True)
    d = y - mu
    var = jnp.mean(d * d, axis=0, keepdims=True)
    yn = d * lax.rsqrt(var + GN_EPS) * gng_ref[0] + gnb_ref[0]
    bonus = jnp.sum(r * k * rk_ref[0], axis=0, keepdims=True) * v
    o_ref[0] = (yn + bonus) * g_ref[0]


def _rwkv_step(prep, state, p):
    nb, d = prep[0].shape
    n_heads = d // HEAD_DIM
    to_heads = lambda a: a.reshape(nb, n_heads, HEAD_DIM).transpose(1, 2, 0)
    vecs = [to_heads(a) for a in prep]
    params = [jnp.broadcast_to(p[n].reshape(n_heads, HEAD_DIM, 1), (n_heads, HEAD_DIM, nb))
              for n in ("r_k", "gn_g", "gn_b")]
    st = state.transpose(1, 2, 3, 0)
    sspec = pl.BlockSpec((1, HEAD_DIM, HEAD_DIM, nb), lambda h: (h, 0, 0, 0))
    vspec = pl.BlockSpec((1, HEAD_DIM, nb), lambda h: (h, 0, 0))
    st_new, o = pl.pallas_call(
        _rwkv_step_kernel, grid=(n_heads,),
        in_specs=[sspec] + [vspec] * 10, out_specs=[sspec, vspec],
        out_shape=[jax.ShapeDtypeStruct(st.shape, F32), jax.ShapeDtypeStruct((n_heads, HEAD_DIM, nb), F32)],
        scratch_shapes=[pltpu.VMEM((HEAD_DIM, nb), F32)],
        compiler_params=_cparams(("parallel",)), name="rwkv_step")(st, *vecs, *params)
    return o.transpose(2, 0, 1).reshape(nb, d), st_new.transpose(3, 0, 1, 2)


def _s5_disc_kernel(lr_ref, li_ref, ldt_ref, ar_ref, ai_ref, cr_ref, ci_ref):
    lr = lr_ref[...]
    li = li_ref[...]
    dt = jnp.exp(ldt_ref[...])
    mag = jnp.exp(lr * dt)
    ar = mag * jnp.cos(li * dt)
    ai = mag * jnp.sin(li * dt)
    den = lr * lr + li * li
    ar_ref[...] = ar
    ai_ref[...] = ai
    cr_ref[...] = ((ar - 1.0) * lr + ai * li) / den
    ci_ref[...] = (ai * lr - (ar - 1.0) * li) / den


def _s5_discretize(lam_re, lam_im, log_dt):
    g, p = lam_re.shape
    ldt = jnp.broadcast_to(log_dt.reshape(g, 1), (g, p))
    return pl.pallas_call(_s5_disc_kernel, out_shape=[jax.ShapeDtypeStruct((g, p), F32)] * 4,
                          name="s5_disc")(lam_re, lam_im, ldt)


def _s5_mix_tail(x, y_state, dsk_ref, wo_ref, wgt_ref, g_ref, b_ref, alpha):
    y = y_state + dsk_ref[...] * x
    zg = _gelu_tanh(y).astype(BF16)
    out = (jnp.dot(zg, wo_ref[...], preferred_element_type=F32)
           * jax.nn.sigmoid(jnp.dot(zg, wgt_ref[...], preferred_element_type=F32)))
    return _layer_norm(alpha * x + out, g_ref[...], b_ref[...])


def _s5_scan_kernel(x_ref, wb_ref, wcr_ref, wci_ref, ar_ref, ai_ref, dsk_ref, wo_ref, wgt_ref, g_ref, b_ref,
                    s0r_ref, s0i_ref, perm_ref, permt_ref, o_ref, sr_out, si_out, bur, bui, sr_st, si_st,
                    *, tc, nb, alpha):
    ti = pl.program_id(0)

    @pl.when(ti == 0)
    def _():
        sr_st[...] = s0r_ref[...]
        si_st[...] = s0i_ref[...]

    x = x_ref[...].reshape(nb * tc, x_ref.shape[2])
    xb = jnp.dot(perm_ref[...], x.astype(BF16), preferred_element_type=F32).astype(BF16)
    n_blk = wb_ref.shape[0]
    half = wb_ref.shape[2] // 2
    for j in range(n_blk):
        pj = jnp.dot(xb[:, j * LANES:(j + 1) * LANES], wb_ref[j], preferred_element_type=F32)
        bur[:, j * half:(j + 1) * half] = pj[:, :half]
        bui[:, j * half:(j + 1) * half] = pj[:, half:]
    n_state = bur.shape[1]
    for s in range(n_state // S5_SLAB):
        cols = slice(s * S5_SLAB, (s + 1) * S5_SLAB)
        ar = jnp.broadcast_to(ar_ref[:, cols], (nb, S5_SLAB))
        ai = jnp.broadcast_to(ai_ref[:, cols], (nb, S5_SLAB))

        sr, si = sr_st[:, cols], si_st[:, cols]
        for t in range(tc):
            rows = slice(t * nb, (t + 1) * nb)
            sr, si = ar * sr - ai * si + bur[rows, cols], ar * si + ai * sr + bui[rows, cols]
            bur[rows, cols] = sr
            bui[rows, cols] = si
        sr_st[:, cols] = sr
        si_st[:, cols] = si
    ys = []
    for j in range(n_blk):
        ys.append(_dot(bur[:, j * half:(j + 1) * half], wcr_ref[j])
                  + _dot(bui[:, j * half:(j + 1) * half], wci_ref[j]))
    y_tb = jnp.concatenate(ys, axis=1)
    y_state = _dot_sel_lhs(permt_ref[...], y_tb)
    out = _s5_mix_tail(x, y_state, dsk_ref, wo_ref, wgt_ref, g_ref, b_ref, alpha)
    o_ref[...] = out.reshape(o_ref.shape)

    @pl.when(ti == pl.num_programs(0) - 1)
    def _():
        sr_out[...] = sr_st[...]
        si_out[...] = si_st[...]


def _s5_scan(x, s0r, s0i, sp, ln, alpha):
    nb, t, d = x.shape
    tc = min(S5_TIME_CHUNK, t)
    assert t % tc == 0 and nb == SUBLANES
    n_state = s0r.shape[1]
    r = jnp.arange(nb * tc)
    perm = ((r[:, None] // nb == r[None, :] % tc) & (r[:, None] % nb == r[None, :] // tc)).astype(BF16)
    consts = [sp["wb"], sp["wcr"], sp["wci"], sp["ar"], sp["ai"], sp["d_skip"], sp["w_out"], sp["w_gate"],
              ln[0], ln[1], s0r, s0i, perm, perm.T]
    blk = pl.BlockSpec((nb, tc, d), lambda i: (0, i, 0))
    st_spec = pl.BlockSpec((nb, n_state), lambda i: (0, 0))
    kern = functools.partial(_s5_scan_kernel, tc=tc, nb=nb, alpha=alpha)
    return pl.pallas_call(
        kern, grid=(t // tc,),
        in_specs=[blk] + [_const_spec(a.shape) for a in consts],
        out_specs=[blk, st_spec, st_spec],
        out_shape=[jax.ShapeDtypeStruct((nb, t, d), F32), jax.ShapeDtypeStruct((nb, n_state), F32),
                   jax.ShapeDtypeStruct((nb, n_state), F32)],
        scratch_shapes=[pltpu.VMEM((tc * nb, n_state), F32), pltpu.VMEM((tc * nb, n_state), F32),
                        pltpu.VMEM((nb, n_state), F32), pltpu.VMEM((nb, n_state), F32)],
        compiler_params=_cparams(("arbitrary",)), name="s5_scan")(x, *consts)


def _s5_step_kernel(x_ref, wb_ref, wcr_ref, wci_ref, ar_ref, ai_ref, dsk_ref, wo_ref, wgt_ref, g_ref, b_ref,
                    s0r_ref, s0i_ref, o_ref, sr_out, si_out, *, alpha):
    x = x_ref[...]
    xb = x.astype(BF16)
    n_blk = wb_ref.shape[0]
    half = wb_ref.shape[2] // 2
    ys = []
    for j in range(n_blk):
        cols = slice(j * half, (j + 1) * half)
        pj = jnp.dot(xb[:, j * LANES:(j + 1) * LANES], wb_ref[j], preferred_element_type=F32)
        ar = ar_ref[:, cols]
        ai = ai_ref[:, cols]
        sr = s0r_ref[:, cols]
        si = s0i_ref[:, cols]
        nsr = ar * sr - ai * si + pj[:, :half]
        nsi = ar * si + ai * sr + pj[:, half:]
        sr_out[:, cols] = nsr
        si_out[:, cols] = nsi
        ys.append(_dot(nsr, wcr_ref[j]) + _dot(nsi, wci_ref[j]))
    y_state = jnp.concatenate(ys, axis=1)
    o_ref[...] = _s5_mix_tail(x, y_state, dsk_ref, wo_ref, wgt_ref, g_ref, b_ref, alpha)


def _s5_step(x, s0r, s0i, sp, ln, alpha):
    m, d = x.shape
    n_state = s0r.shape[1]
    args = [x, sp["wb"], sp["wcr"], sp["wci"], sp["ar"], sp["ai"], sp["d_skip"], sp["w_out"], sp["w_gate"],
            ln[0], ln[1], s0r, s0i]
    kern = functools.partial(_s5_step_kernel, alpha=alpha)
    return pl.pallas_call(
        kern,
        out_shape=[jax.ShapeDtypeStruct((m, d), F32), jax.ShapeDtypeStruct((m, n_state), F32),
                   jax.ShapeDtypeStruct((m, n_state), F32)],
        compiler_params=pltpu.CompilerParams(vmem_limit_bytes=VMEM_LIMIT_BYTES), name="s5_step")(*args)


def _s5_weights(prm, i):
    ar, ai, cr, ci = _s5_discretize(prm["lam_re"][i], prm["lam_im"][i], prm["log_dt"][i])
    b_re, b_im = prm["b_re"][i], prm["b_im"][i]
    bbr = cr[..., None] * b_re - ci[..., None] * b_im
    bbi = cr[..., None] * b_im + ci[..., None] * b_re
    n_groups, n_state, n_ch = bbr.shape
    gpb = LANES // n_ch
    n_blk = n_groups // gpb
    eye = jnp.eye(gpb, dtype=F32)

    def blk_in(bb):
        w = jnp.einsum("jgpc,gh->jgchp", bb.reshape(n_blk, gpb, n_state, n_ch), eye)
        return w.reshape(n_blk, LANES, gpb * n_state)

    def blk_out(cc):
        w = jnp.einsum("jgcp,gh->jhpgc", cc.reshape(n_blk, gpb, n_ch, n_state), eye)
        return w.reshape(n_blk, gpb * n_state, LANES)

    return dict(
        wb=jnp.concatenate([blk_in(bbr), blk_in(bbi)], axis=-1).astype(BF16),
        wcr=blk_out(prm["c_re"][i]).astype(BF16), wci=(-blk_out(prm["c_im"][i])).astype(BF16),
        ar=ar.reshape(1, -1), ai=ai.reshape(1, -1), d_skip=prm["d_skip"][i].reshape(1, -1),
        w_out=prm["w_glu_out"][i].astype(BF16), w_gate=prm["w_glu_gate"][i].astype(BF16))


def _rwkv_weights(prm, i, d_rwkv):
    rank_d = prm["w_w2"].shape[1]
    rank_a = prm["w_a2"].shape[1]
    assert rank_d + rank_a == LANES
    zeros_d = jnp.zeros((rank_d, d_rwkv), F32)
    zeros_a = jnp.zeros((rank_a, d_rwkv), F32)
    head = jnp.arange(d_rwkv) // HEAD_DIM
    ones = (head[:, None] == head[None, :]).astype(F32)
    row = lambda a: a.reshape(1, -1).astype(F32)
    return dict(
        mu=row(prm["mu_shift"][i]), w0=row(prm["w0"][i]), a0=row(prm["a0"][i]), k_k=row(prm["k_k"][i]),
        k_a=row(prm["k_a"][i]), r_k=row(prm["r_k"][i]), gn_g=row(prm["gn_g"][i]), gn_b=row(prm["gn_b"][i]),
        w_w2=jnp.concatenate([prm["w_w2"][i], zeros_a], axis=0).astype(BF16),
        w_a2=jnp.concatenate([zeros_d, prm["w_a2"][i]], axis=0).astype(BF16),
        w_g2=prm["w_g2"][i].astype(BF16), ones=ones.astype(BF16), avg=(ones / HEAD_DIM).astype(BF16))


def _trunk(x, cache_k, cache_v, page_table, wkv0, shift0, s50, prm, w):
    bsz, t, d = x.shape
    depth = prm["ln_g"].shape[0]
    alpha = (2.0 * depth) ** 0.25
    d_sb = prm["sb_bias"].shape[1] * HEAD_DIM
    d_rwkv = prm["w0"].shape[1]
    splits = (0, d_sb, 2 * d_sb, 3 * d_sb, prm["w_in_even"].shape[2])
    ln = lambda layer, j: (prm["ln_g"][layer, j].reshape(1, d), prm["ln_b"][layer, j].reshape(1, d))
    prompt = cache_k is None
    rows = x.reshape(bsz * t, d)
    out_k, out_v, out_wkv, out_shift, out_s5 = [], [], [], [], []
    for layer in range(depth):
        i = layer // 2
        ffn1 = (w["ffn1_wg"][layer], w["ffn1_wu"][layer], w["ffn1_wd"][layer])
        ffn2 = (w["ffn2_wg"][layer], w["ffn2_wu"][layer], w["ffn2_wd"][layer])
        if layer % 2 == 0:
            group_rows = t if prompt else bsz
            x1, q, k, v, pb, k_fm, v_fm = _stage(rows, ffn1, ln(layer, 0), alpha, w_in=w["w_in_even"][i],
                                                 proj_splits=splits, proj_transposed=(1, 2),
                                                 group_rows=group_rows)
            rp = _rwkv_weights(prm, i, d_rwkv)
            pb3 = pb.reshape(bsz, t, -1)
            if prompt:
                o_sb = _sb_prompt(q.reshape(bsz, t, d_sb), k.reshape(bsz, t, d_sb), v.reshape(bsz, t, d_sb),
                                  prm["sb_bias"][i]).reshape(bsz * t, d_sb)
                o_rwkv, wkv = _rwkv_chunked(pb3, shift0[i][:, None, :], wkv0[i], rp)
                o_rwkv = o_rwkv.reshape(bsz * t, d_rwkv)
            else:
                assert t == 1
                n_heads = d_sb // HEAD_DIM
                o_sb = _sb_sample(q.reshape(bsz, n_heads, HEAD_DIM), cache_k[i], cache_v[i], page_table,
                                  prm["sb_bias"][i]).reshape(bsz, d_sb)
                o_rwkv, wkv = _rwkv_step(_rwkv_prep(pb, shift0[i], rp), wkv0[i], rp)
            for fm, out in ((k_fm, out_k), (v_fm, out_v)):
                fm = fm.reshape(-1, d_sb // HEAD_DIM, HEAD_DIM, group_rows)
                if prompt:
                    out.append(fm.transpose(0, 3, 1, 2))
                else:
                    out.append(fm[0].transpose(2, 0, 1)[:, None])
            out_wkv.append(wkv)
            out_shift.append(pb3[:, -1])
            w_out = w["w_out_even"][i]
            mix = (o_sb, o_rwkv, w_out[:d_sb], w_out[d_sb:], *ln(layer, 1))
            (rows,) = _stage(x1, ffn2, ln(layer, 2), alpha, mix=mix)
        else:
            sp = _s5_weights(prm, i)
            s0r = s50[i][..., 0].reshape(bsz, -1)
            s0i = s50[i][..., 1].reshape(bsz, -1)
            (x1,) = _stage(rows, ffn1, ln(layer, 0), alpha)
            if t > 1:
                x2, sr, si = _s5_scan(x1.reshape(bsz, t, d), s0r, s0i, sp, ln(layer, 1), alpha)
                x2 = x2.reshape(bsz * t, d)
            else:
                x2, sr, si = _s5_step(x1, s0r, s0i, sp, ln(layer, 1), alpha)
            n_groups = prm["lam_re"].shape[1]
            out_s5.append(jnp.stack([sr.reshape(bsz, n_groups, -1), si.reshape(bsz, n_groups, -1)], axis=-1))
            (rows,) = _stage(x2, ffn2, ln(layer, 2), alpha)
    y = rows.reshape(bsz, t, d)
    return (y, jnp.stack(out_k), jnp.stack(out_v), jnp.stack(out_wkv), jnp.stack(out_shift),
            jnp.stack(out_s5))


def kernel(x_prompt, x_sample, cache_k_sb, cache_v_sb, page_table, state_wkv, state_shift, state_s5, ln_g, ln_b, ffn1_wg, ffn1_wu, ffn1_wd, ffn2_wg, ffn2_wu, ffn2_wd, w_in_even, w_out_even, sb_bias, mu_shift, w0, w_w2, a0, w_a2, w_g2, k_k, k_a, r_k, gn_g, gn_b, lam_re, lam_im, log_dt, b_re, b_im, c_re, c_im, d_skip, w_glu_out, w_glu_gate):
    prm = dict(ln_g=ln_g, ln_b=ln_b, sb_bias=sb_bias, mu_shift=mu_shift, w0=w0, w_w2=w_w2, a0=a0, w_a2=w_a2,
               w_g2=w_g2, k_k=k_k, k_a=k_a, r_k=r_k.reshape(r_k.shape[0], -1), gn_g=gn_g, gn_b=gn_b,
               lam_re=lam_re, lam_im=lam_im, log_dt=log_dt, b_re=b_re, b_im=b_im, c_re=c_re, c_im=c_im,
               d_skip=d_skip, w_glu_out=w_glu_out, w_glu_gate=w_glu_gate, w_in_even=w_in_even)
    w = dict(ffn1_wg=ffn1_wg, ffn1_wu=ffn1_wu, ffn1_wd=ffn1_wd, ffn2_wg=ffn2_wg, ffn2_wu=ffn2_wu,
             ffn2_wd=ffn2_wd, w_in_even=w_in_even, w_out_even=w_out_even)
    w = {name: a.astype(BF16) for name, a in w.items()}
    nb = x_prompt.shape[0]
    n_even, n_odd = state_wkv.shape[0], state_s5.shape[0]
    wkv0 = jnp.zeros((n_even, nb) + state_wkv.shape[2:], F32)
    shift0 = jnp.zeros((n_even, nb, state_shift.shape[2]), F32)
    s50 = jnp.zeros((n_odd, nb) + state_s5.shape[2:], F32)
    y_p, p_k, p_v, p_wkv, p_shift, p_s5 = _trunk(x_prompt, None, None, None, wkv0, shift0, s50, prm, w)
    cache_kt = cache_k_sb.transpose(0, 1, 3, 4, 2)
    cache_vt = cache_v_sb.transpose(0, 1, 3, 4, 2)
    y_s, s_k, s_v, s_wkv, s_shift, s_s5 = _trunk(x_sample, cache_kt, cache_vt, page_table, state_wkv,
                                                 state_shift, state_s5, prm, w)
    return (y_p, y_s, p_k, p_v, p_wkv, p_shift, p_s5, s_k, s_v, s_wkv, s_shift, s_s5)
```

```python
import functools
import math

import jax
import jax.numpy as jnp
from jax import lax
from jax.experimental import pallas as pl
from jax.experimental.pallas import tpu as pltpu

F32 = jnp.float32
BF16 = jnp.bfloat16

HEAD_DIM = 64
LOG2_E = 1.0 / math.log(2.0)
LN_EPS = 1e-5
GN_EPS = 64e-5
S5_GROUP = 16
S5_STATE = 64
LANES = 128
SUBLANES = 8
VMEM_LIMIT_BYTES = 56 * 1024 * 1024

ROW_TILE = 512
FF_CHUNKS = 2
SB_TILE_Q = 512
SB_TILE_K = 256
SB_EXP2_CAP = 100.0
SB_PAGES_PER_STEP = 16
RWKV_CHUNK = 64
RWKV_SOLVE_BLOCK = 16
RWKV_SEQS_PER_STEP = 4
S5_TIME_CHUNK = 32
S5_SLAB = 1024


def _cparams(sem):
    return pltpu.CompilerParams(dimension_semantics=sem, vmem_limit_bytes=VMEM_LIMIT_BYTES)


def _const_spec(shape):
    nd = len(shape)
    return pl.BlockSpec(shape, lambda *_: (0,) * nd, pipeline_mode=pl.Buffered(1))


def _dot(a, b):
    return jnp.dot(a.astype(BF16), b.astype(BF16), preferred_element_type=F32)


def _dot_nt(a, b):
    return lax.dot_general(a.astype(BF16), b.astype(BF16), (((1,), (1,)), ((), ())),
                           preferred_element_type=F32)


def _dot_tn(a, b):
    return lax.dot_general(a.astype(BF16), b.astype(BF16), (((0,), (0,)), ((), ())),
                           preferred_element_type=F32)


def _split_bf16(a):
    hi = a.astype(BF16)
    lo = (a - hi.astype(F32)).astype(BF16)
    return hi, lo


def _dot_sel_rhs(a, sel):
    hi, lo = _split_bf16(a)
    return (jnp.dot(hi, sel, preferred_element_type=F32) + jnp.dot(lo, sel, preferred_element_type=F32))


def _dot_sel_lhs(sel, a):
    hi, lo = _split_bf16(a)
    return (jnp.dot(sel, hi, preferred_element_type=F32) + jnp.dot(sel, lo, preferred_element_type=F32))


def _layer_norm(y, g, b):
    mu = jnp.mean(y, axis=-1, keepdims=True)
    d = y - mu
    var = jnp.mean(d * d, axis=-1, keepdims=True)
    return d * lax.rsqrt(var + LN_EPS) * g + b


def _softplus(z):
    return jnp.maximum(z, 0.0) + jnp.log1p(jnp.exp(-jnp.abs(z)))


def _gelu_tanh(y):
    c = math.sqrt(2.0 / math.pi)
    return 0.5 * y * (1.0 + jnp.tanh(c * (y + 0.044715 * (y * y * y))))


def _stage_kernel(*refs, alpha, has_mix, proj_splits, proj_transposed):
    it = iter(refs)
    x_ref = next(it)
    if has_mix:
        a_ref, b_ref, wa_ref, wb_ref, gm_ref, bm_ref = (next(it) for _ in range(6))
    wg_ref, wu_ref, wd_ref, g_ref, b2_ref = (next(it) for _ in range(5))
    if proj_splits:
        win_ref = next(it)
    o_ref = next(it)
    proj_refs = [next(it) for _ in proj_splits[1:]] if proj_splits else []
    proj_t_refs = [next(it) for _ in proj_transposed]

    x = x_ref[...]
    if has_mix:
        mix = _dot(a_ref[...], wa_ref[...]) + _dot(b_ref[...], wb_ref[...])
        x = _layer_norm(alpha * x + mix, gm_ref[...], bm_ref[...])
    xb = x.astype(BF16)
    ff = wg_ref.shape[1]
    fc = ff // FF_CHUNKS
    acc = None
    for c in range(0, ff, fc):
        gate = jnp.dot(xb, wg_ref[:, c:c + fc], preferred_element_type=F32)
        up = jnp.dot(xb, wu_ref[:, c:c + fc], preferred_element_type=F32)
        h = (gate * jax.nn.sigmoid(gate) * up).astype(BF16)
        part = jnp.dot(h, wd_ref[c:c + fc, :], preferred_element_type=F32)
        acc = part if acc is None else acc + part
    y = _layer_norm(alpha * x + 0.5 * acc, g_ref[...], b2_ref[...])
    o_ref[...] = y
    if proj_splits:
        proj = jnp.dot(y.astype(BF16), win_ref[...], preferred_element_type=F32)
        for r, lo, hi in zip(proj_refs, proj_splits[:-1], proj_splits[1:]):
            r[...] = proj[:, lo:hi]
        for r, j in zip(proj_t_refs, proj_transposed):
            r[0] = proj[:, proj_splits[j]:proj_splits[j + 1]].T


def _layer_weight(a, layer, part=0, parts=1):
    rows = a.shape[1] // parts
    index = (layer, part, 0)
    return a, pl.BlockSpec((None, rows, a.shape[2]), lambda *_: index, pipeline_mode=pl.Buffered(1))


def _stage(x, ffn, ln_ffn, alpha, mix=None, w_in=None, proj_splits=(), proj_transposed=(), group_rows=None):
    m, d = x.shape
    tm = min(ROW_TILE, m, group_rows or m)
    assert m % tm == 0
    row = lambda w: pl.BlockSpec((tm, w), lambda i: (i, 0))
    args, specs = [x], [row(d)]
    if mix is not None:
        a, b, wa, wb, gm, bm = mix
        args += [a, b, wa[0], wb[0], gm, bm]
        specs += [row(a.shape[1]), row(b.shape[1]), wa[1], wb[1], _const_spec(gm.shape), _const_spec(bm.shape)]
    args += [w[0] for w in ffn] + [*ln_ffn]
    specs += [w[1] for w in ffn] + [_const_spec(p.shape) for p in ln_ffn]
    out_shape = [jax.ShapeDtypeStruct((m, d), F32)]
    out_specs = [row(d)]
    if proj_splits:
        args.append(w_in[0])
        specs.append(w_in[1])
        for lo, hi in zip(proj_splits[:-1], proj_splits[1:]):
            out_shape.append(jax.ShapeDtypeStruct((m, hi - lo), F32))
            out_specs.append(row(hi - lo))
        for j in proj_transposed:
            width = proj_splits[j + 1] - proj_splits[j]
            tiles = group_rows // tm
            assert group_rows % tm == 0 and m % group_rows == 0
            out_shape.append(jax.ShapeDtypeStruct((m // group_rows, width, group_rows), F32))
            out_specs.append(pl.BlockSpec((1, width, tm), lambda i, tiles=tiles: (i // tiles, 0, i % tiles)))
    kern = functools.partial(_stage_kernel, alpha=alpha, has_mix=mix is not None,
                             proj_splits=tuple(proj_splits), proj_transposed=tuple(proj_transposed))
    outs = pl.pallas_call(
        kern, grid=(m // tm,), in_specs=specs, out_specs=out_specs, out_shape=out_shape,
        compiler_params=_cparams(("parallel",)), name="stage")(*args)
    return outs


def _sb_prompt_kernel(bias_ref, q_ref, k_ref, v_ref, o_ref, *, tq, tk, scale):
    hp = pl.program_id(1)
    qi = pl.program_id(2)
    q = q_ref[0] * (scale * LOG2_E)
    lane = lax.broadcasted_iota(jnp.int32, q.shape, 1)
    kr = lax.broadcasted_iota(jnp.int32, (tk, tk), 0)
    kc = lax.broadcasted_iota(jnp.int32, (tk, tk), 1)
    neg_after = jnp.where(kr > kc, -1.0, 0.0).astype(BF16)
    causal = kc < kr
    heads = range(LANES // HEAD_DIM)
    qh = [jnp.where((lane >= h * HEAD_DIM) & (lane < (h + 1) * HEAD_DIM), q, 0.0).astype(BF16) for h in heads]
    bias = [bias_ref[2 * hp + h] * LOG2_E for h in heads]
    band = tq // tk
    qblk = [[qh[h][j * tk:(j + 1) * tk] for h in heads] for j in range(band)]

    def tile_step(kb, qs, carry, masked=False):
        start = pl.multiple_of(kb * tk, tk)
        ks = k_ref[0, pl.ds(start, tk), :].astype(BF16)
        vs = v_ref[0, pl.ds(start, tk), :].astype(BF16)
        out = []
        for h in heads:
            acc, run = carry[2 * h], carry[2 * h + 1]
            z = lax.dot_general(qs[h], ks, (((1,), (1,)), ((), ())), preferred_element_type=F32) + bias[h]
            sp = jnp.maximum(jnp.log2(1.0 + jnp.exp2(jnp.minimum(z, SB_EXP2_CAP))), z)
            spm = jnp.where(causal, sp, 0.0) if masked else sp
            later = jnp.dot(spm.astype(BF16), neg_after, preferred_element_type=F32)
            w = jnp.exp2(z - sp + later + run)
            if masked:
                w = jnp.where(causal, w, 0.0)
            out.append(acc + jnp.dot(w.astype(BF16), vs, preferred_element_type=F32))
            out.append(run - jnp.sum(spm, axis=1, keepdims=True))
        return tuple(out)

    blocks = []
    for j in range(band):
        cr = (jnp.zeros((tk, LANES), F32), jnp.zeros((tk, 1), F32)) * len(heads)
        cr = tile_step(qi * band + j, qblk[j], cr, masked=True)
        for jj in reversed(range(j)):
            cr = tile_step(qi * band + jj, qblk[j], cr)
        blocks.append(cr)
    carry = tuple(jnp.concatenate([blk[n] for blk in blocks], axis=0) for n in range(2 * len(heads)))

    def band_step(i, cr):
        for m in range(band):
            cr = tile_step((qi - 1 - i) * band + (band - 1 - m), qh, cr)
        return cr

    carry = lax.fori_loop(0, qi, band_step, carry)
    o_ref[0] = jnp.where(lane < HEAD_DIM, carry[0], carry[2])


def _sb_prompt(q, k, v, bias):
    b, t, hd = q.shape
    tq = min(SB_TILE_Q, t)
    tk = min(SB_TILE_K, t)
    assert t % tq == 0 and tq % tk == 0 and hd % LANES == 0
    kern = functools.partial(_sb_prompt_kernel, tq=tq, tk=tk, scale=HEAD_DIM ** -0.5)
    qspec = pl.BlockSpec((1, tq, LANES), lambda bi, hp, qi: (bi, qi, hp))
    kvspec = pl.BlockSpec((1, t, LANES), lambda bi, hp, qi: (bi, 0, hp))
    return pl.pallas_call(
        kern, grid=(b, hd // LANES, t // tq),
        in_specs=[pl.BlockSpec(memory_space=pltpu.SMEM), qspec, kvspec, kvspec],
        out_specs=qspec, out_shape=jax.ShapeDtypeStruct((b, t, hd), F32),
        compiler_params=_cparams(("parallel", "parallel", "arbitrary")), name="sb_prompt")(bias, q, k, v)


def _sb_sample_kernel(pt_ref, q_ref, bias_ref, after_ref, prior_ref, *refs, pages, n_heads, scale):
    k_refs = refs[:pages]
    v_refs = refs[pages:2 * pages]
    o_ref = refs[2 * pages]
    qt_ref, acc_ref, run_ref, zw_ref = refs[2 * pages + 1:]
    g = pl.program_id(1)
    page = k_refs[0].shape[2]

    @pl.when(g == 0)
    def _():
        q = q_ref[...] * scale
        sub = lax.broadcasted_iota(jnp.int32, q.shape, 0)
        ones = jnp.ones((n_heads, page), BF16)
        for h in range(n_heads):
            qt_ref[h] = _dot_tn(jnp.where(sub == h, q, 0.0), ones)
        acc_ref[...] = jnp.zeros_like(acc_ref)
        run_ref[...] = jnp.zeros_like(run_ref)

    for u in range(pages):
        for h in range(n_heads):
            zw_ref[u * n_heads + h:u * n_heads + h + 1, :] = jnp.sum(k_refs[u][h] * qt_ref[h], axis=0,
                                                                     keepdims=True)
    z = zw_ref[...] + bias_ref[...]
    sp = _softplus(z)
    later = _dot_sel_rhs(-sp, after_ref[...])
    tot = jnp.broadcast_to(jnp.sum(sp, axis=1, keepdims=True), sp.shape)
    passed = _dot_sel_lhs(prior_ref[...], tot)
    run = jnp.concatenate([run_ref[...]] * pages, axis=0) - passed
    zw_ref[...] = jnp.exp(z - sp + later + run)
    last = slice((pages - 1) * n_heads, pages * n_heads)
    run_ref[...] = run[last] - tot[last]
    for h in range(n_heads):
        acc = acc_ref[h]
        for u in range(pages):
            acc = acc + v_refs[u][h] * zw_ref[u * n_heads + h:u * n_heads + h + 1, :]
        acc_ref[h] = acc

    @pl.when(g == pl.num_programs(1) - 1)
    def _():
        ones = jnp.ones((SUBLANES, page), BF16)
        rows = []
        for h in range(n_heads):
            hi, lo = _split_bf16(acc_ref[h])
            rows.append((_dot_nt(ones, hi) + _dot_nt(ones, lo))[0:1])
        o_ref[...] = jnp.concatenate(rows, axis=0)


def _sb_sample(q, cache_k, cache_v, page_table, bias):
    nb, n_heads, _ = q.shape
    n_pages = page_table.shape[1]
    page = cache_k.shape[3]
    pages = min(SB_PAGES_PER_STEP, n_pages)
    assert n_pages % pages == 0
    rows = pages * n_heads
    bias_b = jnp.broadcast_to(jnp.tile(bias.astype(F32), pages)[:, None], (rows, page))
    key = jnp.arange(page)
    after = (key[:, None] > key[None, :]).astype(BF16)
    r = jnp.arange(rows)
    prior = ((r[:, None] % n_heads == r[None, :] % n_heads)
             & (r[None, :] // n_heads < r[:, None] // n_heads)).astype(BF16)

    def page_spec(u):
        def imap(b, g, pt):
            return (pt[b, n_pages - 1 - (g * pages + u)], 0, 0, 0)
        return pl.BlockSpec((None, n_heads, HEAD_DIM, page), imap)

    qspec = pl.BlockSpec((None, n_heads, HEAD_DIM), lambda b, g, pt: (b, 0, 0))
    const = lambda shape: pl.BlockSpec(shape, lambda b, g, pt: (0, 0))
    grid_spec = pltpu.PrefetchScalarGridSpec(
        num_scalar_prefetch=1, grid=(nb, n_pages // pages),
        in_specs=[qspec, const((rows, page)), const((page, page)), const((rows, rows))]
                 + [page_spec(u) for u in range(pages)] * 2,
        out_specs=qspec,
        scratch_shapes=[pltpu.VMEM((n_heads, HEAD_DIM, page), F32), pltpu.VMEM((n_heads, HEAD_DIM, page), F32),
                        pltpu.VMEM((n_heads, page), F32), pltpu.VMEM((rows, page), F32)])
    kern = functools.partial(_sb_sample_kernel, pages=pages, n_heads=n_heads, scale=HEAD_DIM ** -0.5)
    return pl.pallas_call(
        kern, grid_spec=grid_spec, out_shape=jax.ShapeDtypeStruct((nb, n_heads, HEAD_DIM), F32),
        compiler_params=_cparams(("parallel", "arbitrary")), name="sb_sample")(
            page_table, q, bias_b, after, prior, *([cache_k] * pages), *([cache_v] * pages))


RWKV_PREP_PARAMS = ("mu", "w0", "a0", "k_k", "k_a", "w_w2", "w_a2", "w_g2", "ones")


def _rwkv_prep_math(pb, prev, mu_ref, w0_ref, a0_ref, kk_ref, ka_ref, ww2_ref, wa2_ref, wg2_ref, ones_ref):
    pm = pb + mu_ref[...] * (prev - pb)
    d = w0_ref.shape[1]
    r = pm[:, 0:d]
    kr = pm[:, d:2 * d]
    vr = pm[:, 2 * d:3 * d]
    wad = pm[:, 3 * d:3 * d + LANES]
    gd = pm[:, 3 * d + LANES:]
    z_w = w0_ref[...] + _dot(jnp.tanh(wad), ww2_ref[...])
    log_decay = -math.exp(-0.5) * jax.nn.sigmoid(z_w)
    iclr = jax.nn.sigmoid(a0_ref[...] + _dot(wad, wa2_ref[...]))
    gate = _dot(jax.nn.sigmoid(gd), wg2_ref[...])
    kkr = kr * kk_ref[...]
    ss = _dot(kkr * kkr, ones_ref[...])
    kk = kkr * lax.rsqrt(jnp.maximum(ss, 1e-24))
    kf = kr * (1.0 + (iclr - 1.0) * ka_ref[...])
    return r, kf, vr, log_decay, kk, kk * iclr, gate


def _rwkv_prep_kernel(pb_ref, prev_ref, *refs):
    n = len(RWKV_PREP_PARAMS)
    outs = _rwkv_prep_math(pb_ref[...], prev_ref[...], *refs[:n])
    for o_ref, val in zip(refs[n:], outs):
        o_ref[...] = val


def _rwkv_prep(pb, prev, p):
    rows, c = pb.shape
    d = p["w0"].shape[1]
    consts = [p[n] for n in RWKV_PREP_PARAMS]
    return pl.pallas_call(
        _rwkv_prep_kernel, out_shape=[jax.ShapeDtypeStruct((rows, d), F32)] * 7,
        compiler_params=pltpu.CompilerParams(vmem_limit_bytes=VMEM_LIMIT_BYTES), name="rwkv_prep")(
            pb, prev, *consts)


def _rwkv_chunk_kernel(pb_ref, sh_ref, s0_ref, *refs, chunk, n_heads, n_seq):
    n_prep = len(RWKV_PREP_PARAMS)
    prep_refs = refs[:n_prep]
    rk_ref, gng_ref, gnb_ref, tri_ref, avg_ref, o_ref, sout_ref, st_ref, carry_ref = refs[n_prep:]
    ones_ref = prep_refs[-1]
    L = chunk
    c = pl.program_id(1)

    @pl.when(c == 0)
    def _():
        st_ref[...] = s0_ref[...]
        carry_ref[...] = sh_ref[...]

    pb = pb_ref[...].reshape(n_seq * L, pb_ref.shape[2])
    row = lax.broadcasted_iota(jnp.int32, pb.shape, 0)
    prev = pltpu.roll(pb, 1, axis=0)
    for s in range(n_seq):
        prev = jnp.where(row == s * L, carry_ref[s], prev)
        carry_ref[s] = pb[(s + 1) * L - 1:(s + 1) * L, :]
    r, kf, v, lw, kk, bb, gate = _rwkv_prep_math(pb, prev, *prep_refs)
    cs = _dot_sel_lhs(tri_ref[...], lw)
    ends = [cs[(s + 1) * L - 1:(s + 1) * L, :] for s in range(n_seq)]
    cs_end = jnp.concatenate([jnp.broadcast_to(e, (L, e.shape[1])) for e in ends], axis=0)
    p_in = jnp.exp(cs)
    p_inv = jnp.exp(-cs)
    p_end = jnp.exp(cs_end - cs)
    rt = r * p_in
    kt = kk * jnp.exp(cs - lw)
    bt = bb * p_inv
    kft = kf * p_inv
    kh = kf * p_end
    bh = bb * p_end
    p_last = [jnp.exp(e) for e in ends]

    row = lax.broadcasted_iota(jnp.int32, (2 * L, 2 * L), 0)
    col = lax.broadcasted_iota(jnp.int32, (2 * L, 2 * L), 1)
    t_idx = row & (L - 1)
    s_idx = col & (L - 1)
    tril = s_idx < t_idx + row // L
    row1 = lax.broadcasted_iota(jnp.int32, (L, L), 0)
    col1 = lax.broadcasted_iota(jnp.int32, (L, L), 1)
    same_blk = (row1 // RWKV_SOLVE_BLOCK) == (col1 // RWKV_SOLVE_BLOCK)

    pairs = [(s, h) for s in range(n_seq) for h in range(n_heads)]
    hs = range(len(pairs))
    rs = [slice(s * L, (s + 1) * L) for s, _ in pairs]
    sl = [slice(h * HEAD_DIM, (h + 1) * HEAD_DIM) for _, h in pairs]
    s0 = [st_ref[s, h] for s, h in pairs]
    vh = [v[rs[h], sl[h]] for h in hs]
    x = [jnp.concatenate([kt[rs[h], sl[h]], rt[rs[h], sl[h]]], axis=0) for h in hs]
    yk = [jnp.concatenate([bt[rs[h], sl[h]], kft[rs[h], sl[h]]], axis=0) for h in hs]
    gm = [jnp.where(tril, _dot_nt(x[h], yk[h]), 0.0) for h in hs]
    xm = [_dot_nt(x[h], s0[h]) for h in hs]
    rhs = [-(xm[h][:L] + _dot(gm[h][:L, L:], vh[h])) for h in hs]
    ad = [jnp.where(same_blk, gm[h][:L, :L], 0.0) for h in hs]
    z = [jnp.concatenate([gm[h][:L, :L] - ad[h], rhs[h]], axis=1) for h in hs]
    z = [z[h] - _dot(ad[h], z[h]) for h in hs]
    apow = ad
    span = 2
    while span < RWKV_SOLVE_BLOCK:
        apow = [_dot(apow[h], apow[h]) for h in hs]
        z = [z[h] + _dot(apow[h], z[h]) for h in hs]
        span *= 2
    w2 = [_dot(z[h][:, :L], z[h]) for h in hs]
    x1 = [z[h][:, L:] - w2[h][:, L:] for h in hs]
    u = [x1[h] + _dot(w2[h][:, :L], x1[h]) for h in hs]
    uv = [jnp.concatenate([u[h], vh[h]], axis=0) for h in hs]
    ys = [xm[h][L:] + _dot(gm[h][L:], uv[h]) for h in hs]
    for (s, hd), h in zip(pairs, hs):
        bk = jnp.concatenate([bh[rs[h], sl[h]], kh[rs[h], sl[h]]], axis=0)
        st_ref[s, hd] = s0[h] * p_last[s][:, sl[h]] + _dot_tn(uv[h], bk)
    y = jnp.concatenate([jnp.concatenate(ys[s * n_heads:(s + 1) * n_heads], axis=1) for s in range(n_seq)],
                        axis=0)
    mean = _dot(y, avg_ref[...])
    d = y - mean
    var = _dot(d * d, avg_ref[...])
    yn = d * lax.rsqrt(var + GN_EPS) * gng_ref[...] + gnb_ref[...]
    bonus = _dot(r * kf * rk_ref[...], ones_ref[...]) * v
    o_ref[...] = ((yn + bonus) * gate).reshape(o_ref.shape)

    @pl.when(c == pl.num_programs(1) - 1)
    def _():
        sout_ref[...] = st_ref[...]


def _rwkv_chunked(pb, shift, s0, p):
    b, t, c = pb.shape
    d = p["w0"].shape[1]
    n_heads = d // HEAD_DIM
    L = RWKV_CHUNK
    n_seq = math.gcd(RWKV_SEQS_PER_STEP, b)
    assert t % L == 0 and L // RWKV_SOLVE_BLOCK == 4
    blk = lambda w: pl.BlockSpec((n_seq, L, w), lambda bi, ci: (bi, ci, 0))
    sh_spec = pl.BlockSpec((n_seq, 1, c), lambda bi, ci: (bi, 0, 0))
    sspec = pl.BlockSpec((n_seq, n_heads, HEAD_DIM, HEAD_DIM), lambda bi, ci: (bi, 0, 0, 0))
    tri = jnp.kron(jnp.eye(n_seq, dtype=F32), jnp.tril(jnp.ones((L, L), F32))).astype(BF16)
    consts = [p[n] for n in RWKV_PREP_PARAMS] + [p["r_k"], p["gn_g"], p["gn_b"], tri, p["avg"]]
    kern = functools.partial(_rwkv_chunk_kernel, chunk=L, n_heads=n_heads, n_seq=n_seq)
    return pl.pallas_call(
        kern, grid=(b // n_seq, t // L),
        in_specs=[blk(c), sh_spec, sspec] + [_const_spec(a.shape) for a in consts],
        out_specs=[blk(d), sspec],
        out_shape=[jax.ShapeDtypeStruct((b, t, d), F32),
                   jax.ShapeDtypeStruct((b, n_heads, HEAD_DIM, HEAD_DIM), F32)],
        scratch_shapes=[pltpu.VMEM((n_seq, n_heads, HEAD_DIM, HEAD_DIM), F32), pltpu.VMEM((n_seq, 1, c), F32)],
        compiler_params=_cparams(("parallel", "arbitrary")), name="rwkv_chunk")(pb, shift, s0, *consts)


def _rwkv_step_kernel(s_ref, r_ref, k_ref, v_ref, lw_ref, kk_ref, bb_ref, g_ref, rk_ref, gng_ref, gnb_ref,
                      so_ref, o_ref, y_ref):
    r = r_ref[0]
    k = k_ref[0]
    v = v_ref[0]
    w = jnp.exp(lw_ref[0])
    kk = kk_ref[0]
    bb = bb_ref[0]
    for i in range(HEAD_DIM):
        s = s_ref[0, i]
        sa = jnp.sum(s * kk, axis=0, keepdims=True)
        s2 = s * w - sa * bb + v[i:i + 1, :] * k
        so_ref[0, i] = s2
        y_ref[i:i + 1, :] = jnp.sum(s2 * r, axis=0, keepdims=True)
    y = y_ref[...]
    mu = jnp.mean(y, axis=0, keepdims=True)
    d = y - mu
    var = jnp.mean(d * d, axis=0, keepdims=True)
    yn = d * lax.rsqrt(var + GN_EPS) * gng_ref[0] + gnb_ref[0]
    bonus = jnp.sum(r * k * rk_ref[0], axis=0, keepdims=True) * v
    o_ref[0] = (yn + bonus) * g_ref[0]


def _rwkv_step(prep, state, p):
    nb, d = prep[0].shape
    n_heads = d // HEAD_DIM
    to_heads = lambda a: a.reshape(nb, n_heads, HEAD_DIM).transpose(1, 2, 0)
    vecs = [to_heads(a) for a in prep]
    params = [jnp.broadcast_to(p[n].reshape(n_heads, HEAD_DIM, 1), (n_heads, HEAD_DIM, nb))
              for n in ("r_k", "gn_g", "gn_b")]
    st = state.transpose(1, 2, 3, 0)
    sspec = pl.BlockSpec((1, HEAD_DIM, HEAD_DIM, nb), lambda h: (h, 0, 0, 0))
    vspec = pl.BlockSpec((1, HEAD_DIM, nb), lambda h: (h, 0, 0))
    st_new, o = pl.pallas_call(
        _rwkv_step_kernel, grid=(n_heads,),
        in_specs=[sspec] + [vspec] * 10, out_specs=[sspec, vspec],
        out_shape=[jax.ShapeDtypeStruct(st.shape, F32), jax.ShapeDtypeStruct((n_heads, HEAD_DIM, nb), F32)],
        scratch_shapes=[pltpu.VMEM((HEAD_DIM, nb), F32)],
        compiler_params=_cparams(("parallel",)), name="rwkv_step")(st, *vecs, *params)
    return o.transpose(2, 0, 1).reshape(nb, d), st_new.transpose(3, 0, 1, 2)


def _s5_disc_kernel(lr_ref, li_ref, ldt_ref, ar_ref, ai_ref, cr_ref, ci_ref):
    lr = lr_ref[...]
    li = li_ref[...]
    dt = jnp.exp(ldt_ref[...])
    mag = jnp.exp(lr * dt)
    ar = mag * jnp.cos(li * dt)
    ai = mag * jnp.sin(li * dt)
    den = lr * lr + li * li
    ar_ref[...] = ar
    ai_ref[...] = ai
    cr_ref[...] = ((ar - 1.0) * lr + ai * li) / den
    ci_ref[...] = (ai * lr - (ar - 1.0) * li) / den


def _s5_discretize(lam_re, lam_im, log_dt):
    g, p = lam_re.shape
    ldt = jnp.broadcast_to(log_dt.reshape(g, 1), (g, p))
    return pl.pallas_call(_s5_disc_kernel, out_shape=[jax.ShapeDtypeStruct((g, p), F32)] * 4,
                          name="s5_disc")(lam_re, lam_im, ldt)


def _s5_mix_tail(x, y_state, dsk_ref, wo_ref, wgt_ref, g_ref, b_ref, alpha):
    y = y_state + dsk_ref[...] * x
    zg = _gelu_tanh(y).astype(BF16)
    out = (jnp.dot(zg, wo_ref[...], preferred_element_type=F32)
           * jax.nn.sigmoid(jnp.dot(zg, wgt_ref[...], preferred_element_type=F32)))
    return _layer_norm(alpha * x + out, g_ref[...], b_ref[...])


def _s5_scan_kernel(x_ref, wb_ref, wcr_ref, wci_ref, ar_ref, ai_ref, dsk_ref, wo_ref, wgt_ref, g_ref, b_ref,
                    s0r_ref, s0i_ref, perm_ref, permt_ref, o_ref, sr_out, si_out, bur, bui, sr_st, si_st,
                    *, tc, nb, alpha):
    ti = pl.program_id(0)

    @pl.when(ti == 0)
    def _():
        sr_st[...] = s0r_ref[...]
        si_st[...] = s0i_ref[...]

    x = x_ref[...].reshape(nb * tc, x_ref.shape[2])
    xb = jnp.dot(perm_ref[...], x.astype(BF16), preferred_element_type=F32).astype(BF16)
    n_blk = wb_ref.shape[0]
    half = wb_ref.shape[2] // 2
    for j in range(n_blk):
        pj = jnp.dot(xb[:, j * LANES:(j + 1) * LANES], wb_ref[j], preferred_element_type=F32)
        bur[:, j * half:(j + 1) * half] = pj[:, :half]
        bui[:, j * half:(j + 1) * half] = pj[:, half:]
    n_state = bur.shape[1]
    for s in range(n_state // S5_SLAB):
        cols = slice(s * S5_SLAB, (s + 1) * S5_SLAB)
        ar = jnp.broadcast_to(ar_ref[:, cols], (nb, S5_SLAB))
        ai = jnp.broadcast_to(ai_ref[:, cols], (nb, S5_SLAB))

        sr, si = sr_st[:, cols], si_st[:, cols]
        for t in range(tc):
            rows = slice(t * nb, (t + 1) * nb)
            sr, si = ar * sr - ai * si + bur[rows, cols], ar * si + ai * sr + bui[rows, cols]
            bur[rows, cols] = sr
            bui[rows, cols] = si
        sr_st[:, cols] = sr
        si_st[:, cols] = si
    ys = []
    for j in range(n_blk):
        ys.append(_dot(bur[:, j * half:(j + 1) * half], wcr_ref[j])
                  + _dot(bui[:, j * half:(j + 1) * half], wci_ref[j]))
    y_tb = jnp.concatenate(ys, axis=1)
    y_state = _dot_sel_lhs(permt_ref[...], y_tb)
    out = _s5_mix_tail(x, y_state, dsk_ref, wo_ref, wgt_ref, g_ref, b_ref, alpha)
    o_ref[...] = out.reshape(o_ref.shape)

    @pl.when(ti == pl.num_programs(0) - 1)
    def _():
        sr_out[...] = sr_st[...]
        si_out[...] = si_st[...]


def _s5_scan(x, s0r, s0i, sp, ln, alpha):
    nb, t, d = x.shape
    tc = min(S5_TIME_CHUNK, t)
    assert t % tc == 0 and nb == SUBLANES
    n_state = s0r.shape[1]
    r = jnp.arange(nb * tc)
    perm = ((r[:, None] // nb == r[None, :] % tc) & (r[:, None] % nb == r[None, :] // tc)).astype(BF16)
    consts = [sp["wb"], sp["wcr"], sp["wci"], sp["ar"], sp["ai"], sp["d_skip"], sp["w_out"], sp["w_gate"],
              ln[0], ln[1], s0r, s0i, perm, perm.T]
    blk = pl.BlockSpec((nb, tc, d), lambda i: (0, i, 0))
    st_spec = pl.BlockSpec((nb, n_state), lambda i: (0, 0))
    kern = functools.partial(_s5_scan_kernel, tc=tc, nb=nb, alpha=alpha)
    return pl.pallas_call(
        kern, grid=(t // tc,),
        in_specs=[blk] + [_const_spec(a.shape) for a in consts],
        out_specs=[blk, st_spec, st_spec],
        out_shape=[jax.ShapeDtypeStruct((nb, t, d), F32), jax.ShapeDtypeStruct((nb, n_state), F32),
                   jax.ShapeDtypeStruct((nb, n_state), F32)],
        scratch_shapes=[pltpu.VMEM((tc * nb, n_state), F32), pltpu.VMEM((tc * nb, n_state), F32),
                        pltpu.VMEM((nb, n_state), F32), pltpu.VMEM((nb, n_state), F32)],
        compiler_params=_cparams(("arbitrary",)), name="s5_scan")(x, *consts)


def _s5_step_kernel(x_ref, wb_ref, wcr_ref, wci_ref, ar_ref, ai_ref, dsk_ref, wo_ref, wgt_ref, g_ref, b_ref,
                    s0r_ref, s0i_ref, o_ref, sr_out, si_out, *, alpha):
    x = x_ref[...]
    xb = x.astype(BF16)
    n_blk = wb_ref.shape[0]
    half = wb_ref.shape[2] // 2
    ys = []
    for j in range(n_blk):
        cols = slice(j * half, (j + 1) * half)
        pj = jnp.dot(xb[:, j * LANES:(j + 1) * LANES], wb_ref[j], preferred_element_type=F32)
        ar = ar_ref[:, cols]
        ai = ai_ref[:, cols]
        sr = s0r_ref[:, cols]
        si = s0i_ref[:, cols]
        nsr = ar * sr - ai * si + pj[:, :half]
        nsi = ar * si + ai * sr + pj[:, half:]
        sr_out[:, cols] = nsr
        si_out[:, cols] = nsi
        ys.append(_dot(nsr, wcr_ref[j]) + _dot(nsi, wci_ref[j]))
    y_state = jnp.concatenate(ys, axis=1)
    o_ref[...] = _s5_mix_tail(x, y_state, dsk_ref, wo_ref, wgt_ref, g_ref, b_ref, alpha)


def _s5_step(x, s0r, s0i, sp, ln, alpha):
    m, d = x.shape
    n_state = s0r.shape[1]
    args = [x, sp["wb"], sp["wcr"], sp["wci"], sp["ar"], sp["ai"], sp["d_skip"], sp["w_out"], sp["w_gate"],
            ln[0], ln[1], s0r, s0i]
    kern = functools.partial(_s5_step_kernel, alpha=alpha)
    return pl.pallas_call(
        kern,
        out_shape=[jax.ShapeDtypeStruct((m, d), F32), jax.ShapeDtypeStruct((m, n_state), F32),
                   jax.ShapeDtypeStruct((m, n_state), F32)],
        compiler_params=pltpu.CompilerParams(vmem_limit_bytes=VMEM_LIMIT_BYTES), name="s5_step")(*args)


def _s5_weights(prm, i):
    ar, ai, cr, ci = _s5_discretize(prm["lam_re"][i], prm["lam_im"][i], prm["log_dt"][i])
    b_re, b_im = prm["b_re"][i], prm["b_im"][i]
    bbr = cr[..., None] * b_re - ci[..., None] * b_im
    bbi = cr[..., None] * b_im + ci[..., None] * b_re
    n_groups, n_state, n_ch = bbr.shape
    gpb = LANES // n_ch
    n_blk = n_groups // gpb
    eye = jnp.eye(gpb, dtype=F32)

    def blk_in(bb):
        w = jnp.einsum("jgpc,gh->jgchp", bb.reshape(n_blk, gpb, n_state, n_ch), eye)
        return w.reshape(n_blk, LANES, gpb * n_state)

    def blk_out(cc):
        w = jnp.einsum("jgcp,gh->jhpgc", cc.reshape(n_blk, gpb, n_ch, n_state), eye)
        return w.reshape(n_blk, gpb * n_state, LANES)

    return dict(
        wb=jnp.concatenate([blk_in(bbr), blk_in(bbi)], axis=-1).astype(BF16),
        wcr=blk_out(prm["c_re"][i]).astype(BF16), wci=(-blk_out(prm["c_im"][i])).astype(BF16),
        ar=ar.reshape(1, -1), ai=ai.reshape(1, -1), d_skip=prm["d_skip"][i].reshape(1, -1),
        w_out=prm["w_glu_out"][i].astype(BF16), w_gate=prm["w_glu_gate"][i].astype(BF16))


def _rwkv_weights(prm, i, d_rwkv):
    rank_d = prm["w_w2"].shape[1]
    rank_a = prm["w_a2"].shape[1]
    assert rank_d + rank_a == LANES
    zeros_d = jnp.zeros((rank_d, d_rwkv), F32)
    zeros_a = jnp.zeros((rank_a, d_rwkv), F32)
    head = jnp.arange(d_rwkv) // HEAD_DIM
    ones = (head[:, None] == head[None, :]).astype(F32)
    row = lambda a: a.reshape(1, -1).astype(F32)
    return dict(
        mu=row(prm["mu_shift"][i]), w0=row(prm["w0"][i]), a0=row(prm["a0"][i]), k_k=row(prm["k_k"][i]),
        k_a=row(prm["k_a"][i]), r_k=row(prm["r_k"][i]), gn_g=row(prm["gn_g"][i]), gn_b=row(prm["gn_b"][i]),
        w_w2=jnp.concatenate([prm["w_w2"][i], zeros_a], axis=0).astype(BF16),
        w_a2=jnp.concatenate([zeros_d, prm["w_a2"][i]], axis=0).astype(BF16),
        w_g2=prm["w_g2"][i].astype(BF16), ones=ones.astype(BF16), avg=(ones / HEAD_DIM).astype(BF16))


def _trunk(x, cache_k, cache_v, page_table, wkv0, shift0, s50, prm, w):
    bsz, t, d = x.shape
    depth = prm["ln_g"].shape[0]
    alpha = (2.0 * depth) ** 0.25
    d_sb = prm["sb_bias"].shape[1] * HEAD_DIM
    d_rwkv = prm["w0"].shape[1]
    splits = (0, d_sb, 2 * d_sb, 3 * d_sb, prm["w_in_even"].shape[2])
    ln = lambda layer, j: (prm["ln_g"][layer, j].reshape(1, d), prm["ln_b"][layer, j].reshape(1, d))
    prompt = cache_k is None
    rows = x.reshape(bsz * t, d)
    out_k, out_v, out_wkv, out_shift, out_s5 = [], [], [], [], []
    for layer in range(depth):
        i = layer // 2
        ffn1 = tuple(_layer_weight(w[n], layer) for n in ("ffn1_wg", "ffn1_wu", "ffn1_wd"))
        ffn2 = tuple(_layer_weight(w[n], layer) for n in ("ffn2_wg", "ffn2_wu", "ffn2_wd"))
        if layer % 2 == 0:
            group_rows = t if prompt else bsz
            x1, q, k, v, pb, k_fm, v_fm = _stage(rows, ffn1, ln(layer, 0), alpha,
                                                 w_in=_layer_weight(w["w_in_even"], i),
                                                 proj_splits=splits, proj_transposed=(1, 2),
                                                 group_rows=group_rows)
            rp = _rwkv_weights(prm, i, d_rwkv)
            pb3 = pb.reshape(bsz, t, -1)
            if prompt:
                o_sb = _sb_prompt(q.reshape(bsz, t, d_sb), k.reshape(bsz, t, d_sb), v.reshape(bsz, t, d_sb),
                                  prm["sb_bias"][i]).reshape(bsz * t, d_sb)
                o_rwkv, wkv = _rwkv_chunked(pb3, shift0[i][:, None, :], wkv0[i], rp)
                o_rwkv = o_rwkv.reshape(bsz * t, d_rwkv)
            else:
                assert t == 1
                n_heads = d_sb // HEAD_DIM
                o_sb = _sb_sample(q.reshape(bsz, n_heads, HEAD_DIM), cache_k[i], cache_v[i], page_table,
                                  prm["sb_bias"][i]).reshape(bsz, d_sb)
                o_rwkv, wkv = _rwkv_step(_rwkv_prep(pb, shift0[i], rp), wkv0[i], rp)
            for fm, out in ((k_fm, out_k), (v_fm, out_v)):
                fm = fm.reshape(-1, d_sb // HEAD_DIM, HEAD_DIM, group_rows)
                if prompt:
                    out.append(fm.transpose(0, 3, 1, 2))
                else:
                    out.append(fm[0].transpose(2, 0, 1)[:, None])
            out_wkv.append(wkv)
            out_shift.append(pb3[:, -1])
            assert d_sb == d_rwkv
            mix = (o_sb, o_rwkv, _layer_weight(w["w_out_even"], i, 0, 2),
                   _layer_weight(w["w_out_even"], i, 1, 2), *ln(layer, 1))
            (rows,) = _stage(x1, ffn2, ln(layer, 2), alpha, mix=mix)
        else:
            sp = _s5_weights(prm, i)
            s0r = s50[i][..., 0].reshape(bsz, -1)
            s0i = s50[i][..., 1].reshape(bsz, -1)
            (x1,) = _stage(rows, ffn1, ln(layer, 0), alpha)
            if t > 1:
                x2, sr, si = _s5_scan(x1.reshape(bsz, t, d), s0r, s0i, sp, ln(layer, 1), alpha)
                x2 = x2.reshape(bsz * t, d)
            else:
                x2, sr, si = _s5_step(x1, s0r, s0i, sp, ln(layer, 1), alpha)
            n_groups = prm["lam_re"].shape[1]
            out_s5.append(jnp.stack([sr.reshape(bsz, n_groups, -1), si.reshape(bsz, n_groups, -1)], axis=-1))
            (rows,) = _stage(x2, ffn2, ln(layer, 2), alpha)
    y = rows.reshape(bsz, t, d)
    return (y, jnp.stack(out_k), jnp.stack(out_v), jnp.stack(out_wkv), jnp.stack(out_shift),
            jnp.stack(out_s5))


def kernel(x_prompt, x_sample, cache_k_sb, cache_v_sb, page_table, state_wkv, state_shift, state_s5, ln_g, ln_b, ffn1_wg, ffn1_wu, ffn1_wd, ffn2_wg, ffn2_wu, ffn2_wd, w_in_even, w_out_even, sb_bias, mu_shift, w0, w_w2, a0, w_a2, w_g2, k_k, k_a, r_k, gn_g, gn_b, lam_re, lam_im, log_dt, b_re, b_im, c_re, c_im, d_skip, w_glu_out, w_glu_gate):
    prm = dict(ln_g=ln_g, ln_b=ln_b, sb_bias=sb_bias, mu_shift=mu_shift, w0=w0, w_w2=w_w2, a0=a0, w_a2=w_a2,
               w_g2=w_g2, k_k=k_k, k_a=k_a, r_k=r_k.reshape(r_k.shape[0], -1), gn_g=gn_g, gn_b=gn_b,
               lam_re=lam_re, lam_im=lam_im, log_dt=log_dt, b_re=b_re, b_im=b_im, c_re=c_re, c_im=c_im,
               d_skip=d_skip, w_glu_out=w_glu_out, w_glu_gate=w_glu_gate, w_in_even=w_in_even)
    w = dict(ffn1_wg=ffn1_wg, ffn1_wu=ffn1_wu, ffn1_wd=ffn1_wd, ffn2_wg=ffn2_wg, ffn2_wu=ffn2_wu,
             ffn2_wd=ffn2_wd, w_in_even=w_in_even, w_out_even=w_out_even)
    w = {name: a.astype(BF16) for name, a in w.items()}
    nb = x_prompt.shape[0]
    n_even, n_odd = state_wkv.shape[0], state_s5.shape[0]
    wkv0 = jnp.zeros((n_even, nb) + state_wkv.shape[2:], F32)
    shift0 = jnp.zeros((n_even, nb, state_shift.shape[2]), F32)
    s50 = jnp.zeros((n_odd, nb) + state_s5.shape[2:], F32)
    y_p, p_k, p_v, p_wkv, p_shift, p_s5 = _trunk(x_prompt, None, None, None, wkv0, shift0, s50, prm, w)
    cache_kt = cache_k_sb.transpose(0, 1, 3, 4, 2)
    cache_vt = cache_v_sb.transpose(0, 1, 3, 4, 2)
    y_s, s_k, s_v, s_wkv, s_shift, s_s5 = _trunk(x_sample, cache_kt, cache_vt, page_table, state_wkv,
                                                 state_shift, state_s5, prm, w)
    return (y_p, y_s, p_k, p_v, p_wkv, p_shift, p_s5, s_k, s_v, s_wkv, s_shift, s_s5)
```

```python
import functools
import math

import jax
import jax.numpy as jnp
from jax import lax
from jax.experimental import pallas as pl
from jax.experimental.pallas import tpu as pltpu

F32 = jnp.float32
BF16 = jnp.bfloat16

HEAD_DIM = 64
LOG2_E = 1.0 / math.log(2.0)
LN_EPS = 1e-5
GN_EPS = 64e-5
S5_GROUP = 16
S5_STATE = 64
LANES = 128
SUBLANES = 8
VMEM_LIMIT_BYTES = 56 * 1024 * 1024

ROW_TILE = 512
FF_CHUNKS = 2
SB_TILE_Q = 512
SB_TILE_K = 256
SB_EXP2_CAP = 100.0
RWKV_CHUNK = 64
RWKV_SOLVE_BLOCK = 16
RWKV_SEQS_PER_STEP = 4
S5_TIME_CHUNK = 32
S5_SLAB = 1024


def _cparams(sem):
    return pltpu.CompilerParams(dimension_semantics=sem, vmem_limit_bytes=VMEM_LIMIT_BYTES)


def _const_spec(shape):
    nd = len(shape)
    return pl.BlockSpec(shape, lambda *_: (0,) * nd, pipeline_mode=pl.Buffered(1))


def _dot(a, b):
    return jnp.dot(a.astype(BF16), b.astype(BF16), preferred_element_type=F32)


def _dot_nt(a, b):
    return lax.dot_general(a.astype(BF16), b.astype(BF16), (((1,), (1,)), ((), ())),
                           preferred_element_type=F32)


def _dot_tn(a, b):
    return lax.dot_general(a.astype(BF16), b.astype(BF16), (((0,), (0,)), ((), ())),
                           preferred_element_type=F32)


def _split_bf16(a):
    hi = a.astype(BF16)
    lo = (a - hi.astype(F32)).astype(BF16)
    return hi, lo


def _dot_sel_rhs(a, sel):
    hi, lo = _split_bf16(a)
    return (jnp.dot(hi, sel, preferred_element_type=F32) + jnp.dot(lo, sel, preferred_element_type=F32))


def _dot_sel_lhs(sel, a):
    hi, lo = _split_bf16(a)
    return (jnp.dot(sel, hi, preferred_element_type=F32) + jnp.dot(sel, lo, preferred_element_type=F32))


def _layer_norm(y, g, b):
    mu = jnp.mean(y, axis=-1, keepdims=True)
    d = y - mu
    var = jnp.mean(d * d, axis=-1, keepdims=True)
    return d * lax.rsqrt(var + LN_EPS) * g + b


def _softplus(z):
    return jnp.maximum(z, 0.0) + jnp.log1p(jnp.exp(-jnp.abs(z)))


def _gelu_tanh(y):
    c = math.sqrt(2.0 / math.pi)
    return 0.5 * y * (1.0 + jnp.tanh(c * (y + 0.044715 * (y * y * y))))


def _stage_kernel(*refs, alpha, has_mix, proj_splits, proj_transposed):
    it = iter(refs)
    x_ref = next(it)
    if has_mix:
        a_ref, b_ref, wa_ref, wb_ref, gm_ref, bm_ref = (next(it) for _ in range(6))
    wg_ref, wu_ref, wd_ref, g_ref, b2_ref = (next(it) for _ in range(5))
    if proj_splits:
        win_ref = next(it)
    o_ref = next(it)
    proj_refs = [next(it) for _ in proj_splits[1:]] if proj_splits else []
    proj_t_refs = [next(it) for _ in proj_transposed]

    x = x_ref[...]
    if has_mix:
        mix = _dot(a_ref[...], wa_ref[...]) + _dot(b_ref[...], wb_ref[...])
        x = _layer_norm(alpha * x + mix, gm_ref[...], bm_ref[...])
    xb = x.astype(BF16)
    ff = wg_ref.shape[1]
    fc = ff // FF_CHUNKS
    acc = None
    for c in range(0, ff, fc):
        gate = jnp.dot(xb, wg_ref[:, c:c + fc], preferred_element_type=F32)
        up = jnp.dot(xb, wu_ref[:, c:c + fc], preferred_element_type=F32)
        h = (gate * jax.nn.sigmoid(gate) * up).astype(BF16)
        part = jnp.dot(h, wd_ref[c:c + fc, :], preferred_element_type=F32)
        acc = part if acc is None else acc + part
    y = _layer_norm(alpha * x + 0.5 * acc, g_ref[...], b2_ref[...])
    o_ref[...] = y
    if proj_splits:
        proj = jnp.dot(y.astype(BF16), win_ref[...], preferred_element_type=F32)
        for r, lo, hi in zip(proj_refs, proj_splits[:-1], proj_splits[1:]):
            r[...] = proj[:, lo:hi]
        for r, j in zip(proj_t_refs, proj_transposed):
            r[0] = proj[:, proj_splits[j]:proj_splits[j + 1]].T


def _layer_weight(a, layer, part=0, parts=1):
    rows = a.shape[1] // parts
    index = (layer, part, 0)
    return a, pl.BlockSpec((None, rows, a.shape[2]), lambda *_: index, pipeline_mode=pl.Buffered(1))


def _stage(x, ffn, ln_ffn, alpha, mix=None, w_in=None, proj_splits=(), proj_transposed=(), group_rows=None):
    m, d = x.shape
    tm = min(ROW_TILE, m, group_rows or m)
    assert m % tm == 0
    row = lambda w: pl.BlockSpec((tm, w), lambda i: (i, 0))
    args, specs = [x], [row(d)]
    if mix is not None:
        a, b, wa, wb, gm, bm = mix
        args += [a, b, wa[0], wb[0], gm, bm]
        specs += [row(a.shape[1]), row(b.shape[1]), wa[1], wb[1], _const_spec(gm.shape), _const_spec(bm.shape)]
    args += [w[0] for w in ffn] + [*ln_ffn]
    specs += [w[1] for w in ffn] + [_const_spec(p.shape) for p in ln_ffn]
    out_shape = [jax.ShapeDtypeStruct((m, d), F32)]
    out_specs = [row(d)]
    if proj_splits:
        args.append(w_in[0])
        specs.append(w_in[1])
        for lo, hi in zip(proj_splits[:-1], proj_splits[1:]):
            out_shape.append(jax.ShapeDtypeStruct((m, hi - lo), F32))
            out_specs.append(row(hi - lo))
        for j in proj_transposed:
            width = proj_splits[j + 1] - proj_splits[j]
            tiles = group_rows // tm
            assert group_rows % tm == 0 and m % group_rows == 0
            out_shape.append(jax.ShapeDtypeStruct((m // group_rows, width, group_rows), F32))
            out_specs.append(pl.BlockSpec((1, width, tm), lambda i, tiles=tiles: (i // tiles, 0, i % tiles)))
    kern = functools.partial(_stage_kernel, alpha=alpha, has_mix=mix is not None,
                             proj_splits=tuple(proj_splits), proj_transposed=tuple(proj_transposed))
    outs = pl.pallas_call(
        kern, grid=(m // tm,), in_specs=specs, out_specs=out_specs, out_shape=out_shape,
        compiler_params=_cparams(("parallel",)), name="stage")(*args)
    return outs


def _sb_prompt_kernel(bias_ref, q_ref, k_ref, v_ref, o_ref, *, tq, tk, scale):
    hp = pl.program_id(1)
    qi = pl.program_id(2)
    q = q_ref[0] * (scale * LOG2_E)
    lane = lax.broadcasted_iota(jnp.int32, q.shape, 1)
    kr = lax.broadcasted_iota(jnp.int32, (tk, tk), 0)
    kc = lax.broadcasted_iota(jnp.int32, (tk, tk), 1)
    neg_after = jnp.where(kr > kc, -1.0, 0.0).astype(BF16)
    causal = kc < kr
    heads = range(LANES // HEAD_DIM)
    qh = [jnp.where((lane >= h * HEAD_DIM) & (lane < (h + 1) * HEAD_DIM), q, 0.0).astype(BF16) for h in heads]
    bias = [bias_ref[2 * hp + h] * LOG2_E for h in heads]
    band = tq // tk
    qblk = [[qh[h][j * tk:(j + 1) * tk] for h in heads] for j in range(band)]

    def tile_step(kb, qs, carry, masked=False):
        start = pl.multiple_of(kb * tk, tk)
        ks = k_ref[0, pl.ds(start, tk), :].astype(BF16)
        vs = v_ref[0, pl.ds(start, tk), :].astype(BF16)
        out = []
        for h in heads:
            acc, run = carry[2 * h], carry[2 * h + 1]
            z = lax.dot_general(qs[h], ks, (((1,), (1,)), ((), ())), preferred_element_type=F32) + bias[h]
            sp = jnp.maximum(jnp.log2(1.0 + jnp.exp2(jnp.minimum(z, SB_EXP2_CAP))), z)
            spm = jnp.where(causal, sp, 0.0) if masked else sp
            later = jnp.dot(spm.astype(BF16), neg_after, preferred_element_type=F32)
            w = jnp.exp2(z - sp + later + run)
            if masked:
                w = jnp.where(causal, w, 0.0)
            out.append(acc + jnp.dot(w.astype(BF16), vs, preferred_element_type=F32))
            out.append(run - jnp.sum(spm, axis=1, keepdims=True))
        return tuple(out)

    blocks = []
    for j in range(band):
        cr = (jnp.zeros((tk, LANES), F32), jnp.zeros((tk, 1), F32)) * len(heads)
        cr = tile_step(qi * band + j, qblk[j], cr, masked=True)
        for jj in reversed(range(j)):
            cr = tile_step(qi * band + jj, qblk[j], cr)
        blocks.append(cr)
    carry = tuple(jnp.concatenate([blk[n] for blk in blocks], axis=0) for n in range(2 * len(heads)))

    def band_step(i, cr):
        for m in range(band):
            cr = tile_step((qi - 1 - i) * band + (band - 1 - m), qh, cr)
        return cr

    carry = lax.fori_loop(0, qi, band_step, carry)
    o_ref[0] = jnp.where(lane < HEAD_DIM, carry[0], carry[2])


def _sb_prompt(q, k, v, bias):
    b, t, hd = q.shape
    tq = min(SB_TILE_Q, t)
    tk = min(SB_TILE_K, t)
    assert t % tq == 0 and tq % tk == 0 and hd % LANES == 0
    kern = functools.partial(_sb_prompt_kernel, tq=tq, tk=tk, scale=HEAD_DIM ** -0.5)
    qspec = pl.BlockSpec((1, tq, LANES), lambda bi, hp, qi: (bi, qi, hp))
    kvspec = pl.BlockSpec((1, t, LANES), lambda bi, hp, qi: (bi, 0, hp))
    return pl.pallas_call(
        kern, grid=(b, hd // LANES, t // tq),
        in_specs=[pl.BlockSpec(memory_space=pltpu.SMEM), qspec, kvspec, kvspec],
        out_specs=qspec, out_shape=jax.ShapeDtypeStruct((b, t, hd), F32),
        compiler_params=_cparams(("parallel", "parallel", "arbitrary")), name="sb_prompt")(bias, q, k, v)


def _sb_sample_body(q_ref, bias_ref, after_ref, prior_ref, k_refs, v_refs, o_ref, qt_ref, zw_ref, *, n_heads,
                    scale):
    pages = len(k_refs)
    page = k_refs[0].shape[2]
    q = q_ref[...] * scale
    sub = lax.broadcasted_iota(jnp.int32, q.shape, 0)
    ones = jnp.ones((n_heads, page), BF16)
    for h in range(n_heads):
        qt_ref[h] = _dot_tn(jnp.where(sub == h, q, 0.0), ones)
    for u in range(pages):
        for h in range(n_heads):
            zw_ref[u * n_heads + h:u * n_heads + h + 1, :] = jnp.sum(k_refs[u][h] * qt_ref[h], axis=0,
                                                                     keepdims=True)
    z = zw_ref[...] + bias_ref[...]
    sp = _softplus(z)
    later = _dot_sel_rhs(-sp, after_ref[...])
    tot = jnp.broadcast_to(jnp.sum(sp, axis=1, keepdims=True), sp.shape)
    passed = _dot_sel_lhs(prior_ref[...], tot)
    zw_ref[...] = jnp.exp(z - sp + later - passed)
    ones = jnp.ones((SUBLANES, page), BF16)
    rows = []
    for h in range(n_heads):
        acc = v_refs[0][h] * zw_ref[h:h + 1, :]
        for u in range(1, pages):
            acc = acc + v_refs[u][h] * zw_ref[u * n_heads + h:u * n_heads + h + 1, :]
        hi, lo = _split_bf16(acc)
        rows.append((_dot_nt(ones, hi) + _dot_nt(ones, lo))[0:1])
    o_ref[...] = jnp.concatenate(rows, axis=0)


def _sb_sample_kernel(pt_ref, q_ref, bias_ref, after_ref, prior_ref, *refs, pages, n_heads, scale):
    o_ref, qt_ref, zw_ref = refs[2 * pages:]
    _sb_sample_body(q_ref, bias_ref, after_ref, prior_ref, refs[:pages], refs[pages:2 * pages], o_ref, qt_ref,
                    zw_ref, n_heads=n_heads, scale=scale)


def _sb_sample_consts(bias, n_pages, n_heads, page):
    rows = n_pages * n_heads
    bias_b = jnp.broadcast_to(jnp.tile(bias.astype(F32), n_pages)[:, None], (rows, page))
    key = jnp.arange(page)
    after = (key[:, None] > key[None, :]).astype(BF16)
    r = jnp.arange(rows)
    prior = ((r[:, None] % n_heads == r[None, :] % n_heads)
             & (r[None, :] // n_heads < r[:, None] // n_heads)).astype(BF16)
    return bias_b, after, prior


def _sb_sample(q, cache_k, cache_v, page_table, bias):
    nb, n_heads, _ = q.shape
    pages = page_table.shape[1]
    page = cache_k.shape[3]
    rows = pages * n_heads
    bias_b, after, prior = _sb_sample_consts(bias, pages, n_heads, page)

    def page_spec(u):
        return pl.BlockSpec((None, n_heads, HEAD_DIM, page), lambda b, pt: (pt[b, pages - 1 - u], 0, 0, 0))

    qspec = pl.BlockSpec((None, n_heads, HEAD_DIM), lambda b, pt: (b, 0, 0))
    const = lambda a: pl.BlockSpec(a.shape, lambda b, pt: (0, 0))
    grid_spec = pltpu.PrefetchScalarGridSpec(
        num_scalar_prefetch=1, grid=(nb,),
        in_specs=[qspec, const(bias_b), const(after), const(prior)] + [page_spec(u) for u in range(pages)] * 2,
        out_specs=qspec,
        scratch_shapes=[pltpu.VMEM((n_heads, HEAD_DIM, page), F32), pltpu.VMEM((rows, page), F32)])
    kern = functools.partial(_sb_sample_kernel, pages=pages, n_heads=n_heads, scale=HEAD_DIM ** -0.5)
    return pl.pallas_call(
        kern, grid_spec=grid_spec, out_shape=jax.ShapeDtypeStruct((nb, n_heads, HEAD_DIM), F32),
        compiler_params=_cparams(("parallel",)), name="sb_sample")(
            page_table, q, bias_b, after, prior, *([cache_k] * pages), *([cache_v] * pages))


def _sb_fused_kernel(pt_ref, bias_ref, q_ref, k_ref, v_ref, sq_ref, sbias_ref, after_ref, prior_ref, *refs,
                     pages, n_heads, tq, tk, scale):
    o_ref, so_ref, qt_ref, zw_ref = refs[2 * pages:]
    _sb_sample_body(sq_ref, sbias_ref, after_ref, prior_ref, refs[:pages], refs[pages:2 * pages], so_ref, qt_ref,
                    zw_ref, n_heads=n_heads, scale=scale)
    _sb_prompt_kernel(bias_ref, q_ref, k_ref, v_ref, o_ref, tq=tq, tk=tk, scale=scale)


def _sb_fused(qp, kp, vp, bias, qs, cache_k, cache_v, page_table):
    b, t, hd = qp.shape
    nb, n_heads, _ = qs.shape
    pages = page_table.shape[1]
    page = cache_k.shape[3]
    tq, tk = min(SB_TILE_Q, t), min(SB_TILE_K, t)
    n_hp, n_q = hd // LANES, t // tq
    assert b * n_hp * n_q == nb and t % tq == 0 and tq % tk == 0
    rows = pages * n_heads
    bias_b, after, prior = _sb_sample_consts(bias, pages, n_heads, page)
    seq = lambda bi, hp, qi: (bi * n_hp + hp) * n_q + qi

    def page_spec(u):
        return pl.BlockSpec((None, n_heads, HEAD_DIM, page),
                            lambda bi, hp, qi, pt: (pt[seq(bi, hp, qi), pages - 1 - u], 0, 0, 0))

    qspec = pl.BlockSpec((1, tq, LANES), lambda bi, hp, qi, pt: (bi, qi, hp))
    kvspec = pl.BlockSpec((1, t, LANES), lambda bi, hp, qi, pt: (bi, 0, hp))
    sqspec = pl.BlockSpec((None, n_heads, HEAD_DIM), lambda bi, hp, qi, pt: (seq(bi, hp, qi), 0, 0))
    const = lambda a: pl.BlockSpec(a.shape, lambda bi, hp, qi, pt: (0, 0))
    grid_spec = pltpu.PrefetchScalarGridSpec(
        num_scalar_prefetch=1, grid=(b, n_hp, n_q),
        in_specs=[pl.BlockSpec(memory_space=pltpu.SMEM), qspec, kvspec, kvspec, sqspec, const(bias_b),
                  const(after), const(prior)] + [page_spec(u) for u in range(pages)] * 2,
        out_specs=[qspec, sqspec],
        scratch_shapes=[pltpu.VMEM((n_heads, HEAD_DIM, page), F32), pltpu.VMEM((rows, page), F32)])
    kern = functools.partial(_sb_fused_kernel, pages=pages, n_heads=n_heads, tq=tq, tk=tk, scale=HEAD_DIM ** -0.5)
    return pl.pallas_call(
        kern, grid_spec=grid_spec,
        out_shape=[jax.ShapeDtypeStruct((b, t, hd), F32), jax.ShapeDtypeStruct((nb, n_heads, HEAD_DIM), F32)],
        compiler_params=_cparams(("parallel", "parallel", "arbitrary")), name="sb_fused")(
            page_table, bias, qp, kp, vp, qs, bias_b, after, prior, *([cache_k] * pages), *([cache_v] * pages))


def _sb_attention(prompt, sample, page_table):
    qp, kp, vp, bias = prompt
    qs, cache_k, cache_v, _ = sample
    b, t, hd = qp.shape
    steps = b * (hd // LANES) * (t // min(SB_TILE_Q, t))
    if steps == qs.shape[0]:
        return _sb_fused(qp, kp, vp, bias, qs, cache_k, cache_v, page_table)
    return _sb_prompt(qp, kp, vp, bias), _sb_sample(qs, cache_k, cache_v, page_table, bias)


RWKV_PREP_PARAMS = ("mu", "w0", "a0", "k_k", "k_a", "w_w2", "w_a2", "w_g2", "ones")


def _rwkv_prep_math(pb, prev, mu_ref, w0_ref, a0_ref, kk_ref, ka_ref, ww2_ref, wa2_ref, wg2_ref, ones_ref):
    pm = pb + mu_ref[...] * (prev - pb)
    d = w0_ref.shape[1]
    r = pm[:, 0:d]
    kr = pm[:, d:2 * d]
    vr = pm[:, 2 * d:3 * d]
    wad = pm[:, 3 * d:3 * d + LANES]
    gd = pm[:, 3 * d + LANES:]
    z_w = w0_ref[...] + _dot(jnp.tanh(wad), ww2_ref[...])
    log_decay = -math.exp(-0.5) * jax.nn.sigmoid(z_w)
    iclr = jax.nn.sigmoid(a0_ref[...] + _dot(wad, wa2_ref[...]))
    gate = _dot(jax.nn.sigmoid(gd), wg2_ref[...])
    kkr = kr * kk_ref[...]
    ss = _dot(kkr * kkr, ones_ref[...])
    kk = kkr * lax.rsqrt(jnp.maximum(ss, 1e-24))
    kf = kr * (1.0 + (iclr - 1.0) * ka_ref[...])
    return r, kf, vr, log_decay, kk, kk * iclr, gate


def _rwkv_prep_kernel(pb_ref, prev_ref, *refs):
    n = len(RWKV_PREP_PARAMS)
    outs = _rwkv_prep_math(pb_ref[...], prev_ref[...], *refs[:n])
    for o_ref, val in zip(refs[n:], outs):
        o_ref[...] = val


def _rwkv_prep(pb, prev, p):
    rows, c = pb.shape
    d = p["w0"].shape[1]
    consts = [p[n] for n in RWKV_PREP_PARAMS]
    return pl.pallas_call(
        _rwkv_prep_kernel, out_shape=[jax.ShapeDtypeStruct((rows, d), F32)] * 7,
        compiler_params=pltpu.CompilerParams(vmem_limit_bytes=VMEM_LIMIT_BYTES), name="rwkv_prep")(
            pb, prev, *consts)


def _rwkv_chunk_kernel(pb_ref, sh_ref, s0_ref, *refs, chunk, n_heads, n_seq):
    n_prep = len(RWKV_PREP_PARAMS)
    prep_refs = refs[:n_prep]
    rk_ref, gng_ref, gnb_ref, tri_ref, avg_ref, o_ref, sout_ref, st_ref, carry_ref = refs[n_prep:]
    ones_ref = prep_refs[-1]
    L = chunk
    c = pl.program_id(1)

    @pl.when(c == 0)
    def _():
        st_ref[...] = s0_ref[...]
        carry_ref[...] = sh_ref[...]

    pb = pb_ref[...].reshape(n_seq * L, pb_ref.shape[2])
    row = lax.broadcasted_iota(jnp.int32, pb.shape, 0)
    prev = pltpu.roll(pb, 1, axis=0)
    for s in range(n_seq):
        prev = jnp.where(row == s * L, carry_ref[s], prev)
        carry_ref[s] = pb[(s + 1) * L - 1:(s + 1) * L, :]
    r, kf, v, lw, kk, bb, gate = _rwkv_prep_math(pb, prev, *prep_refs)
    cs = _dot_sel_lhs(tri_ref[...], lw)
    ends = [cs[(s + 1) * L - 1:(s + 1) * L, :] for s in range(n_seq)]
    cs_end = jnp.concatenate([jnp.broadcast_to(e, (L, e.shape[1])) for e in ends], axis=0)
    p_in = jnp.exp(cs)
    p_inv = jnp.exp(-cs)
    p_end = jnp.exp(cs_end - cs)
    rt = r * p_in
    kt = kk * jnp.exp(cs - lw)
    bt = bb * p_inv
    kft = kf * p_inv
    kh = kf * p_end
    bh = bb * p_end
    p_last = [jnp.exp(e) for e in ends]

    row = lax.broadcasted_iota(jnp.int32, (2 * L, 2 * L), 0)
    col = lax.broadcasted_iota(jnp.int32, (2 * L, 2 * L), 1)
    t_idx = row & (L - 1)
    s_idx = col & (L - 1)
    tril = s_idx < t_idx + row // L
    row1 = lax.broadcasted_iota(jnp.int32, (L, L), 0)
    col1 = lax.broadcasted_iota(jnp.int32, (L, L), 1)
    same_blk = (row1 // RWKV_SOLVE_BLOCK) == (col1 // RWKV_SOLVE_BLOCK)

    pairs = [(s, h) for s in range(n_seq) for h in range(n_heads)]
    hs = range(len(pairs))
    rs = [slice(s * L, (s + 1) * L) for s, _ in pairs]
    sl = [slice(h * HEAD_DIM, (h + 1) * HEAD_DIM) for _, h in pairs]
    s0 = [st_ref[s, h] for s, h in pairs]
    vh = [v[rs[h], sl[h]] for h in hs]
    x = [jnp.concatenate([kt[rs[h], sl[h]], rt[rs[h], sl[h]]], axis=0) for h in hs]
    yk = [jnp.concatenate([bt[rs[h], sl[h]], kft[rs[h], sl[h]]], axis=0) for h in hs]
    gm = [jnp.where(tril, _dot_nt(x[h], yk[h]), 0.0) for h in hs]
    xm = [_dot_nt(x[h], s0[h]) for h in hs]
    rhs = [-(xm[h][:L] + _dot(gm[h][:L, L:], vh[h])) for h in hs]
    ad = [jnp.where(same_blk, gm[h][:L, :L], 0.0) for h in hs]
    z = [jnp.concatenate([gm[h][:L, :L] - ad[h], rhs[h]], axis=1) for h in hs]
    z = [z[h] - _dot(ad[h], z[h]) for h in hs]
    apow = ad
    span = 2
    while span < RWKV_SOLVE_BLOCK:
        apow = [_dot(apow[h], apow[h]) for h in hs]
        z = [z[h] + _dot(apow[h], z[h]) for h in hs]
        span *= 2
    w2 = [_dot(z[h][:, :L], z[h]) for h in hs]
    x1 = [z[h][:, L:] - w2[h][:, L:] for h in hs]
    u = [x1[h] + _dot(w2[h][:, :L], x1[h]) for h in hs]
    uv = [jnp.concatenate([u[h], vh[h]], axis=0) for h in hs]
    ys = [xm[h][L:] + _dot(gm[h][L:], uv[h]) for h in hs]
    for (s, hd), h in zip(pairs, hs):
        bk = jnp.concatenate([bh[rs[h], sl[h]], kh[rs[h], sl[h]]], axis=0)
        st_ref[s, hd] = s0[h] * p_last[s][:, sl[h]] + _dot_tn(uv[h], bk)
    y = jnp.concatenate([jnp.concatenate(ys[s * n_heads:(s + 1) * n_heads], axis=1) for s in range(n_seq)],
                        axis=0)
    mean = _dot(y, avg_ref[...])
    d = y - mean
    var = _dot(d * d, avg_ref[...])
    yn = d * lax.rsqrt(var + GN_EPS) * gng_ref[...] + gnb_ref[...]
    bonus = _dot(r * kf * rk_ref[...], ones_ref[...]) * v
    o_ref[...] = ((yn + bonus) * gate).reshape(o_ref.shape)

    @pl.when(c == pl.num_programs(1) - 1)
    def _():
        sout_ref[...] = st_ref[...]


def _rwkv_chunked(pb, shift, s0, p):
    b, t, c = pb.shape
    d = p["w0"].shape[1]
    n_heads = d // HEAD_DIM
    L = RWKV_CHUNK
    n_seq = math.gcd(RWKV_SEQS_PER_STEP, b)
    assert t % L == 0 and L // RWKV_SOLVE_BLOCK == 4
    blk = lambda w: pl.BlockSpec((n_seq, L, w), lambda bi, ci: (bi, ci, 0))
    sh_spec = pl.BlockSpec((n_seq, 1, c), lambda bi, ci: (bi, 0, 0))
    sspec = pl.BlockSpec((n_seq, n_heads, HEAD_DIM, HEAD_DIM), lambda bi, ci: (bi, 0, 0, 0))
    tri = jnp.kron(jnp.eye(n_seq, dtype=F32), jnp.tril(jnp.ones((L, L), F32))).astype(BF16)
    consts = [p[n] for n in RWKV_PREP_PARAMS] + [p["r_k"], p["gn_g"], p["gn_b"], tri, p["avg"]]
    kern = functools.partial(_rwkv_chunk_kernel, chunk=L, n_heads=n_heads, n_seq=n_seq)
    return pl.pallas_call(
        kern, grid=(b // n_seq, t // L),
        in_specs=[blk(c), sh_spec, sspec] + [_const_spec(a.shape) for a in consts],
        out_specs=[blk(d), sspec],
        out_shape=[jax.ShapeDtypeStruct((b, t, d), F32),
                   jax.ShapeDtypeStruct((b, n_heads, HEAD_DIM, HEAD_DIM), F32)],
        scratch_shapes=[pltpu.VMEM((n_seq, n_heads, HEAD_DIM, HEAD_DIM), F32), pltpu.VMEM((n_seq, 1, c), F32)],
        compiler_params=_cparams(("parallel", "arbitrary")), name="rwkv_chunk")(pb, shift, s0, *consts)


def _rwkv_step_kernel(s_ref, r_ref, k_ref, v_ref, lw_ref, kk_ref, bb_ref, g_ref, rk_ref, gng_ref, gnb_ref,
                      so_ref, o_ref, y_ref):
    r = r_ref[0]
    k = k_ref[0]
    v = v_ref[0]
    w = jnp.exp(lw_ref[0])
    kk = kk_ref[0]
    bb = bb_ref[0]
    for i in range(HEAD_DIM):
        s = s_ref[0, i]
        sa = jnp.sum(s * kk, axis=0, keepdims=True)
        s2 = s * w - sa * bb + v[i:i + 1, :] * k
        so_ref[0, i] = s2
        y_ref[i:i + 1, :] = jnp.sum(s2 * r, axis=0, keepdims=True)
    y = y_ref[...]
    mu = jnp.mean(y, axis=0, keepdims=True)
    d = y - mu
    var = jnp.mean(d * d, axis=0, keepdims=True)
    yn = d * lax.rsqrt(var + GN_EPS) * gng_ref[0] + gnb_ref[0]
    bonus = jnp.sum(r * k * rk_ref[0], axis=0, keepdims=True) * v
    o_ref[0] = (yn + bonus) * g_ref[0]


def _rwkv_step(prep, state, p):
    nb, d = prep[0].shape
    n_heads = d // HEAD_DIM
    to_heads = lambda a: a.reshape(nb, n_heads, HEAD_DIM).transpose(1, 2, 0)
    vecs = [to_heads(a) for a in prep]
    params = [jnp.broadcast_to(p[n].reshape(n_heads, HEAD_DIM, 1), (n_heads, HEAD_DIM, nb))
              for n in ("r_k", "gn_g", "gn_b")]
    st = state.transpose(1, 2, 3, 0)
    sspec = pl.BlockSpec((1, HEAD_DIM, HEAD_DIM, nb), lambda h: (h, 0, 0, 0))
    vspec = pl.BlockSpec((1, HEAD_DIM, nb), lambda h: (h, 0, 0))
    st_new, o = pl.pallas_call(
        _rwkv_step_kernel, grid=(n_heads,),
        in_specs=[sspec] + [vspec] * 10, out_specs=[sspec, vspec],
        out_shape=[jax.ShapeDtypeStruct(st.shape, F32), jax.ShapeDtypeStruct((n_heads, HEAD_DIM, nb), F32)],
        scratch_shapes=[pltpu.VMEM((HEAD_DIM, nb), F32)],
        compiler_params=_cparams(("parallel",)), name="rwkv_step")(st, *vecs, *params)
    return o.transpose(2, 0, 1).reshape(nb, d), st_new.transpose(3, 0, 1, 2)


def _s5_disc_kernel(lr_ref, li_ref, ldt_ref, ar_ref, ai_ref, cr_ref, ci_ref):
    lr = lr_ref[...]
    li = li_ref[...]
    dt = jnp.exp(ldt_ref[...])
    mag = jnp.exp(lr * dt)
    ar = mag * jnp.cos(li * dt)
    ai = mag * jnp.sin(li * dt)
    den = lr * lr + li * li
    ar_ref[...] = ar
    ai_ref[...] = ai
    cr_ref[...] = ((ar - 1.0) * lr + ai * li) / den
    ci_ref[...] = (ai * lr - (ar - 1.0) * li) / den


def _s5_discretize(lam_re, lam_im, log_dt):
    g, p = lam_re.shape
    ldt = jnp.broadcast_to(log_dt.reshape(g, 1), (g, p))
    return pl.pallas_call(_s5_disc_kernel, out_shape=[jax.ShapeDtypeStruct((g, p), F32)] * 4,
                          name="s5_disc")(lam_re, lam_im, ldt)


def _s5_mix_tail(x, y_state, dsk_ref, wo_ref, wgt_ref, g_ref, b_ref, alpha):
    y = y_state + dsk_ref[...] * x
    zg = _gelu_tanh(y).astype(BF16)
    out = (jnp.dot(zg, wo_ref[...], preferred_element_type=F32)
           * jax.nn.sigmoid(jnp.dot(zg, wgt_ref[...], preferred_element_type=F32)))
    return _layer_norm(alpha * x + out, g_ref[...], b_ref[...])


def _s5_scan_kernel(x_ref, wb_ref, wcr_ref, wci_ref, ar_ref, ai_ref, dsk_ref, wo_ref, wgt_ref, g_ref, b_ref,
                    s0r_ref, s0i_ref, perm_ref, permt_ref, o_ref, sr_out, si_out, bur, bui, sr_st, si_st,
                    *, tc, nb, alpha):
    ti = pl.program_id(0)

    @pl.when(ti == 0)
    def _():
        sr_st[...] = s0r_ref[...]
        si_st[...] = s0i_ref[...]

    x = x_ref[...].reshape(nb * tc, x_ref.shape[2])
    xb = jnp.dot(perm_ref[...], x.astype(BF16), preferred_element_type=F32).astype(BF16)
    n_blk = wb_ref.shape[0]
    half = wb_ref.shape[2] // 2
    for j in range(n_blk):
        pj = jnp.dot(xb[:, j * LANES:(j + 1) * LANES], wb_ref[j], preferred_element_type=F32)
        bur[:, j * half:(j + 1) * half] = pj[:, :half]
        bui[:, j * half:(j + 1) * half] = pj[:, half:]
    n_state = bur.shape[1]
    for s in range(n_state // S5_SLAB):
        cols = slice(s * S5_SLAB, (s + 1) * S5_SLAB)
        ar = jnp.broadcast_to(ar_ref[:, cols], (nb, S5_SLAB))
        ai = jnp.broadcast_to(ai_ref[:, cols], (nb, S5_SLAB))

        sr, si = sr_st[:, cols], si_st[:, cols]
        for t in range(tc):
            rows = slice(t * nb, (t + 1) * nb)
            sr, si = ar * sr - ai * si + bur[rows, cols], ar * si + ai * sr + bui[rows, cols]
            bur[rows, cols] = sr
            bui[rows, cols] = si
        sr_st[:, cols] = sr
        si_st[:, cols] = si
    ys = []
    for j in range(n_blk):
        ys.append(_dot(bur[:, j * half:(j + 1) * half], wcr_ref[j])
                  + _dot(bui[:, j * half:(j + 1) * half], wci_ref[j]))
    y_tb = jnp.concatenate(ys, axis=1)
    y_state = _dot_sel_lhs(permt_ref[...], y_tb)
    out = _s5_mix_tail(x, y_state, dsk_ref, wo_ref, wgt_ref, g_ref, b_ref, alpha)
    o_ref[...] = out.reshape(o_ref.shape)

    @pl.when(ti == pl.num_programs(0) - 1)
    def _():
        sr_out[...] = sr_st[...]
        si_out[...] = si_st[...]


def _s5_scan(x, s0r, s0i, sp, ln, alpha):
    nb, t, d = x.shape
    tc = min(S5_TIME_CHUNK, t)
    assert t % tc == 0 and nb == SUBLANES
    n_state = s0r.shape[1]
    r = jnp.arange(nb * tc)
    perm = ((r[:, None] // nb == r[None, :] % tc) & (r[:, None] % nb == r[None, :] // tc)).astype(BF16)
    consts = [sp["wb"], sp["wcr"], sp["wci"], sp["ar"], sp["ai"], sp["d_skip"], sp["w_out"], sp["w_gate"],
              ln[0], ln[1], s0r, s0i, perm, perm.T]
    blk = pl.BlockSpec((nb, tc, d), lambda i: (0, i, 0))
    st_spec = pl.BlockSpec((nb, n_state), lambda i: (0, 0))
    kern = functools.partial(_s5_scan_kernel, tc=tc, nb=nb, alpha=alpha)
    return pl.pallas_call(
        kern, grid=(t // tc,),
        in_specs=[blk] + [_const_spec(a.shape) for a in consts],
        out_specs=[blk, st_spec, st_spec],
        out_shape=[jax.ShapeDtypeStruct((nb, t, d), F32), jax.ShapeDtypeStruct((nb, n_state), F32),
                   jax.ShapeDtypeStruct((nb, n_state), F32)],
        scratch_shapes=[pltpu.VMEM((tc * nb, n_state), F32), pltpu.VMEM((tc * nb, n_state), F32),
                        pltpu.VMEM((nb, n_state), F32), pltpu.VMEM((nb, n_state), F32)],
        compiler_params=_cparams(("arbitrary",)), name="s5_scan")(x, *consts)


def _s5_step_kernel(x_ref, wb_ref, wcr_ref, wci_ref, ar_ref, ai_ref, dsk_ref, wo_ref, wgt_ref, g_ref, b_ref,
                    s0r_ref, s0i_ref, o_ref, sr_out, si_out, *, alpha):
    x = x_ref[...]
    xb = x.astype(BF16)
    n_blk = wb_ref.shape[0]
    half = wb_ref.shape[2] // 2
    ys = []
    for j in range(n_blk):
        cols = slice(j * half, (j + 1) * half)
        pj = jnp.dot(xb[:, j * LANES:(j + 1) * LANES], wb_ref[j], preferred_element_type=F32)
        ar = ar_ref[:, cols]
        ai = ai_ref[:, cols]
        sr = s0r_ref[:, cols]
        si = s0i_ref[:, cols]
        nsr = ar * sr - ai * si + pj[:, :half]
        nsi = ar * si + ai * sr + pj[:, half:]
        sr_out[:, cols] = nsr
        si_out[:, cols] = nsi
        ys.append(_dot(nsr, wcr_ref[j]) + _dot(nsi, wci_ref[j]))
    y_state = jnp.concatenate(ys, axis=1)
    o_ref[...] = _s5_mix_tail(x, y_state, dsk_ref, wo_ref, wgt_ref, g_ref, b_ref, alpha)


def _s5_step(x, s0r, s0i, sp, ln, alpha):
    m, d = x.shape
    n_state = s0r.shape[1]
    args = [x, sp["wb"], sp["wcr"], sp["wci"], sp["ar"], sp["ai"], sp["d_skip"], sp["w_out"], sp["w_gate"],
            ln[0], ln[1], s0r, s0i]
    kern = functools.partial(_s5_step_kernel, alpha=alpha)
    return pl.pallas_call(
        kern,
        out_shape=[jax.ShapeDtypeStruct((m, d), F32), jax.ShapeDtypeStruct((m, n_state), F32),
                   jax.ShapeDtypeStruct((m, n_state), F32)],
        compiler_params=pltpu.CompilerParams(vmem_limit_bytes=VMEM_LIMIT_BYTES), name="s5_step")(*args)


def _s5_weights(prm, i):
    ar, ai, cr, ci = _s5_discretize(prm["lam_re"][i], prm["lam_im"][i], prm["log_dt"][i])
    b_re, b_im = prm["b_re"][i], prm["b_im"][i]
    bbr = cr[..., None] * b_re - ci[..., None] * b_im
    bbi = cr[..., None] * b_im + ci[..., None] * b_re
    n_groups, n_state, n_ch = bbr.shape
    gpb = LANES // n_ch
    n_blk = n_groups // gpb
    eye = jnp.eye(gpb, dtype=F32)

    def blk_in(bb):
        w = jnp.einsum("jgpc,gh->jgchp", bb.reshape(n_blk, gpb, n_state, n_ch), eye)
        return w.reshape(n_blk, LANES, gpb * n_state)

    def blk_out(cc):
        w = jnp.einsum("jgcp,gh->jhpgc", cc.reshape(n_blk, gpb, n_ch, n_state), eye)
        return w.reshape(n_blk, gpb * n_state, LANES)

    return dict(
        wb=jnp.concatenate([blk_in(bbr), blk_in(bbi)], axis=-1).astype(BF16),
        wcr=blk_out(prm["c_re"][i]).astype(BF16), wci=(-blk_out(prm["c_im"][i])).astype(BF16),
        ar=ar.reshape(1, -1), ai=ai.reshape(1, -1), d_skip=prm["d_skip"][i].reshape(1, -1),
        w_out=prm["w_glu_out"][i].astype(BF16), w_gate=prm["w_glu_gate"][i].astype(BF16))


def _rwkv_weights(prm, i, d_rwkv):
    rank_d = prm["w_w2"].shape[1]
    rank_a = prm["w_a2"].shape[1]
    assert rank_d + rank_a == LANES
    zeros_d = jnp.zeros((rank_d, d_rwkv), F32)
    zeros_a = jnp.zeros((rank_a, d_rwkv), F32)
    head = jnp.arange(d_rwkv) // HEAD_DIM
    ones = (head[:, None] == head[None, :]).astype(F32)
    row = lambda a: a.reshape(1, -1).astype(F32)
    return dict(
        mu=row(prm["mu_shift"][i]), w0=row(prm["w0"][i]), a0=row(prm["a0"][i]), k_k=row(prm["k_k"][i]),
        k_a=row(prm["k_a"][i]), r_k=row(prm["r_k"][i]), gn_g=row(prm["gn_g"][i]), gn_b=row(prm["gn_b"][i]),
        w_w2=jnp.concatenate([prm["w_w2"][i], zeros_a], axis=0).astype(BF16),
        w_a2=jnp.concatenate([zeros_d, prm["w_a2"][i]], axis=0).astype(BF16),
        w_g2=prm["w_g2"][i].astype(BF16), ones=ones.astype(BF16), avg=(ones / HEAD_DIM).astype(BF16))


def _trunk(x, cache_k, cache_v, wkv0, shift0, s50, prm, w):
    bsz, t, d = x.shape
    depth = prm["ln_g"].shape[0]
    alpha = (2.0 * depth) ** 0.25
    d_sb = prm["sb_bias"].shape[1] * HEAD_DIM
    d_rwkv = prm["w0"].shape[1]
    splits = (0, d_sb, 2 * d_sb, 3 * d_sb, prm["w_in_even"].shape[2])
    ln = lambda layer, j: (prm["ln_g"][layer, j].reshape(1, d), prm["ln_b"][layer, j].reshape(1, d))
    prompt = cache_k is None
    rows = x.reshape(bsz * t, d)
    out_k, out_v, out_wkv, out_shift, out_s5 = [], [], [], [], []
    for layer in range(depth):
        i = layer // 2
        ffn1 = tuple(_layer_weight(w[n], layer) for n in ("ffn1_wg", "ffn1_wu", "ffn1_wd"))
        ffn2 = tuple(_layer_weight(w[n], layer) for n in ("ffn2_wg", "ffn2_wu", "ffn2_wd"))
        if layer % 2 == 0:
            group_rows = t if prompt else bsz
            x1, q, k, v, pb, k_fm, v_fm = _stage(rows, ffn1, ln(layer, 0), alpha,
                                                 w_in=_layer_weight(w["w_in_even"], i),
                                                 proj_splits=splits, proj_transposed=(1, 2),
                                                 group_rows=group_rows)
            rp = _rwkv_weights(prm, i, d_rwkv)
            pb3 = pb.reshape(bsz, t, -1)
            if prompt:
                o_sb = yield (q.reshape(bsz, t, d_sb), k.reshape(bsz, t, d_sb), v.reshape(bsz, t, d_sb),
                              prm["sb_bias"][i])
                o_rwkv, wkv = _rwkv_chunked(pb3, shift0[i][:, None, :], wkv0[i], rp)
                o_rwkv = o_rwkv.reshape(bsz * t, d_rwkv)
            else:
                assert t == 1
                o_sb = yield (q.reshape(bsz, d_sb // HEAD_DIM, HEAD_DIM), cache_k[i], cache_v[i], prm["sb_bias"][i])
                o_rwkv, wkv = _rwkv_step(_rwkv_prep(pb, shift0[i], rp), wkv0[i], rp)
            o_sb = o_sb.reshape(bsz * t, d_sb)
            for fm, out in ((k_fm, out_k), (v_fm, out_v)):
                fm = fm.reshape(-1, d_sb // HEAD_DIM, HEAD_DIM, group_rows)
                if prompt:
                    out.append(fm.transpose(0, 3, 1, 2))
                else:
                    out.append(fm[0].transpose(2, 0, 1)[:, None])
            out_wkv.append(wkv)
            out_shift.append(pb3[:, -1])
            assert d_sb == d_rwkv
            mix = (o_sb, o_rwkv, _layer_weight(w["w_out_even"], i, 0, 2),
                   _layer_weight(w["w_out_even"], i, 1, 2), *ln(layer, 1))
            (rows,) = _stage(x1, ffn2, ln(layer, 2), alpha, mix=mix)
        else:
            sp = _s5_weights(prm, i)
            s0r = s50[i][..., 0].reshape(bsz, -1)
            s0i = s50[i][..., 1].reshape(bsz, -1)
            (x1,) = _stage(rows, ffn1, ln(layer, 0), alpha)
            if t > 1:
                x2, sr, si = _s5_scan(x1.reshape(bsz, t, d), s0r, s0i, sp, ln(layer, 1), alpha)
                x2 = x2.reshape(bsz * t, d)
            else:
                x2, sr, si = _s5_step(x1, s0r, s0i, sp, ln(layer, 1), alpha)
            n_groups = prm["lam_re"].shape[1]
            out_s5.append(jnp.stack([sr.reshape(bsz, n_groups, -1), si.reshape(bsz, n_groups, -1)], axis=-1))
            (rows,) = _stage(x2, ffn2, ln(layer, 2), alpha)
    y = rows.reshape(bsz, t, d)
    return (y, jnp.stack(out_k), jnp.stack(out_v), jnp.stack(out_wkv), jnp.stack(out_shift),
            jnp.stack(out_s5))


def kernel(x_prompt, x_sample, cache_k_sb, cache_v_sb, page_table, state_wkv, state_shift, state_s5, ln_g, ln_b, ffn1_wg, ffn1_wu, ffn1_wd, ffn2_wg, ffn2_wu, ffn2_wd, w_in_even, w_out_even, sb_bias, mu_shift, w0, w_w2, a0, w_a2, w_g2, k_k, k_a, r_k, gn_g, gn_b, lam_re, lam_im, log_dt, b_re, b_im, c_re, c_im, d_skip, w_glu_out, w_glu_gate):
    prm = dict(ln_g=ln_g, ln_b=ln_b, sb_bias=sb_bias, mu_shift=mu_shift, w0=w0, w_w2=w_w2, a0=a0, w_a2=w_a2,
               w_g2=w_g2, k_k=k_k, k_a=k_a, r_k=r_k.reshape(r_k.shape[0], -1), gn_g=gn_g, gn_b=gn_b,
               lam_re=lam_re, lam_im=lam_im, log_dt=log_dt, b_re=b_re, b_im=b_im, c_re=c_re, c_im=c_im,
               d_skip=d_skip, w_glu_out=w_glu_out, w_glu_gate=w_glu_gate, w_in_even=w_in_even)
    w = dict(ffn1_wg=ffn1_wg, ffn1_wu=ffn1_wu, ffn1_wd=ffn1_wd, ffn2_wg=ffn2_wg, ffn2_wu=ffn2_wu,
             ffn2_wd=ffn2_wd, w_in_even=w_in_even, w_out_even=w_out_even)
    w = {name: a.astype(BF16) for name, a in w.items()}
    nb = x_prompt.shape[0]
    n_even, n_odd = state_wkv.shape[0], state_s5.shape[0]
    wkv0 = jnp.zeros((n_even, nb) + state_wkv.shape[2:], F32)
    shift0 = jnp.zeros((n_even, nb, state_shift.shape[2]), F32)
    s50 = jnp.zeros((n_odd, nb) + state_s5.shape[2:], F32)
    cache_kt = cache_k_sb.transpose(0, 1, 3, 4, 2)
    cache_vt = cache_v_sb.transpose(0, 1, 3, 4, 2)
    trunk_p = _trunk(x_prompt, None, None, wkv0, shift0, s50, prm, w)
    trunk_s = _trunk(x_sample, cache_kt, cache_vt, state_wkv, state_shift, state_s5, prm, w)
    def advance(trunk, value):
        try:
            return trunk.send(value), None
        except StopIteration as done:
            return None, done.value

    (req_p, out_p), (req_s, out_s) = advance(trunk_p, None), advance(trunk_s, None)
    while out_p is None:
        o_p, o_s = _sb_attention(req_p, req_s, page_table)
        (req_p, out_p), (req_s, out_s) = advance(trunk_p, o_p), advance(trunk_s, o_s)
    y_p, p_k, p_v, p_wkv, p_shift, p_s5 = out_p
    y_s, s_k, s_v, s_wkv, s_shift, s_s5 = out_s
    return (y_p, y_s, p_k, p_v, p_wkv, p_shift, p_s5, s_k, s_v, s_wkv, s_shift, s_s5)
```

```python
import functools
import math

import jax
import jax.numpy as jnp
from jax import lax
from jax.experimental import pallas as pl
from jax.experimental.pallas import tpu as pltpu

F32 = jnp.float32
BF16 = jnp.bfloat16

HEAD_DIM = 64
LOG2_E = 1.0 / math.log(2.0)
LN_EPS = 1e-5
GN_EPS = 64e-5
S5_GROUP = 16
S5_STATE = 64
LANES = 128
SUBLANES = 8
VMEM_LIMIT_BYTES = 56 * 1024 * 1024

ROW_TILE = 512
FF_CHUNKS = 2
SB_TILE_Q = 512
SB_TILE_K = 256
SB_EXP2_CAP = 100.0
RWKV_CHUNK = 64
RWKV_SOLVE_BLOCK = 16
RWKV_SEQS_PER_STEP = 4
S5_TIME_CHUNK = 32
S5_SLAB = 1024


def _cparams(sem):
    return pltpu.CompilerParams(dimension_semantics=sem, vmem_limit_bytes=VMEM_LIMIT_BYTES)


def _const_spec(shape):
    nd = len(shape)
    return pl.BlockSpec(shape, lambda *_: (0,) * nd, pipeline_mode=pl.Buffered(1))


def _dot(a, b):
    return jnp.dot(a.astype(BF16), b.astype(BF16), preferred_element_type=F32)


def _dot_nt(a, b):
    return lax.dot_general(a.astype(BF16), b.astype(BF16), (((1,), (1,)), ((), ())),
                           preferred_element_type=F32)


def _dot_tn(a, b):
    return lax.dot_general(a.astype(BF16), b.astype(BF16), (((0,), (0,)), ((), ())),
                           preferred_element_type=F32)


def _split_bf16(a):
    hi = a.astype(BF16)
    lo = (a - hi.astype(F32)).astype(BF16)
    return hi, lo


def _dot_sel_rhs(a, sel):
    hi, lo = _split_bf16(a)
    return (jnp.dot(hi, sel, preferred_element_type=F32) + jnp.dot(lo, sel, preferred_element_type=F32))


def _dot_sel_lhs(sel, a):
    hi, lo = _split_bf16(a)
    return (jnp.dot(sel, hi, preferred_element_type=F32) + jnp.dot(sel, lo, preferred_element_type=F32))


def _layer_norm(y, g, b):
    mu = jnp.mean(y, axis=-1, keepdims=True)
    d = y - mu
    var = jnp.mean(d * d, axis=-1, keepdims=True)
    return d * lax.rsqrt(var + LN_EPS) * g + b


def _softplus(z):
    return jnp.maximum(z, 0.0) + jnp.log1p(jnp.exp(-jnp.abs(z)))


def _gelu_tanh(y):
    c = math.sqrt(2.0 / math.pi)
    return 0.5 * y * (1.0 + jnp.tanh(c * (y + 0.044715 * (y * y * y))))


def _stage_kernel(*refs, alpha, has_mix, proj_splits, proj_transposed):
    it = iter(refs)
    x_ref = next(it)
    if has_mix:
        a_ref, b_ref, wa_ref, wb_ref, gm_ref, bm_ref = (next(it) for _ in range(6))
    wg_ref, wu_ref, wd_ref, g_ref, b2_ref = (next(it) for _ in range(5))
    if proj_splits:
        win_ref = next(it)
    o_ref = next(it)
    proj_refs = [next(it) for _ in proj_splits[1:]] if proj_splits else []
    proj_t_refs = [next(it) for _ in proj_transposed]

    x = x_ref[...]
    if has_mix:
        mix = _dot(a_ref[...], wa_ref[...]) + _dot(b_ref[...], wb_ref[...])
        x = _layer_norm(alpha * x + mix, gm_ref[...], bm_ref[...])
    xb = x.astype(BF16)
    ff = wg_ref.shape[1]
    fc = ff // FF_CHUNKS
    acc = None
    for c in range(0, ff, fc):
        gate = jnp.dot(xb, wg_ref[:, c:c + fc], preferred_element_type=F32)
        up = jnp.dot(xb, wu_ref[:, c:c + fc], preferred_element_type=F32)
        h = (gate * jax.nn.sigmoid(gate) * up).astype(BF16)
        part = jnp.dot(h, wd_ref[c:c + fc, :], preferred_element_type=F32)
        acc = part if acc is None else acc + part
    y = _layer_norm(alpha * x + 0.5 * acc, g_ref[...], b2_ref[...])
    o_ref[...] = y
    if proj_splits:
        proj = jnp.dot(y.astype(BF16), win_ref[...], preferred_element_type=F32)
        for r, lo, hi in zip(proj_refs, proj_splits[:-1], proj_splits[1:]):
            r[...] = proj[:, lo:hi]
        for r, j in zip(proj_t_refs, proj_transposed):
            r[0] = proj[:, proj_splits[j]:proj_splits[j + 1]].T


def _layer_weight(a, layer, part=0, parts=1):
    rows = a.shape[1] // parts
    index = (layer, part, 0)
    return a, pl.BlockSpec((None, rows, a.shape[2]), lambda *_: index, pipeline_mode=pl.Buffered(1))


def _stage(x, ffn, ln_ffn, alpha, mix=None, w_in=None, proj_splits=(), proj_transposed=(), group_rows=None):
    m, d = x.shape
    tm = min(ROW_TILE, m, group_rows or m)
    assert m % tm == 0
    row = lambda w: pl.BlockSpec((tm, w), lambda i: (i, 0))
    args, specs = [x], [row(d)]
    if mix is not None:
        a, b, wa, wb, gm, bm = mix
        args += [a, b, wa[0], wb[0], gm, bm]
        specs += [row(a.shape[1]), row(b.shape[1]), wa[1], wb[1], _const_spec(gm.shape), _const_spec(bm.shape)]
    args += [w[0] for w in ffn] + [*ln_ffn]
    specs += [w[1] for w in ffn] + [_const_spec(p.shape) for p in ln_ffn]
    out_shape = [jax.ShapeDtypeStruct((m, d), F32)]
    out_specs = [row(d)]
    if proj_splits:
        args.append(w_in[0])
        specs.append(w_in[1])
        for lo, hi in zip(proj_splits[:-1], proj_splits[1:]):
            out_shape.append(jax.ShapeDtypeStruct((m, hi - lo), F32))
            out_specs.append(row(hi - lo))
        for j in proj_transposed:
            width = proj_splits[j + 1] - proj_splits[j]
            tiles = group_rows // tm
            assert group_rows % tm == 0 and m % group_rows == 0
            out_shape.append(jax.ShapeDtypeStruct((m // group_rows, width, group_rows), F32))
            out_specs.append(pl.BlockSpec((1, width, tm), lambda i, tiles=tiles: (i // tiles, 0, i % tiles)))
    kern = functools.partial(_stage_kernel, alpha=alpha, has_mix=mix is not None,
                             proj_splits=tuple(proj_splits), proj_transposed=tuple(proj_transposed))
    outs = pl.pallas_call(
        kern, grid=(m // tm,), in_specs=specs, out_specs=out_specs, out_shape=out_shape,
        compiler_params=_cparams(("parallel",)), name="stage")(*args)
    return outs


def _sb_prompt_kernel(bias_ref, q_ref, k_ref, v_ref, o_ref, *, tq, tk, scale, interleave=iter(())):
    hp = pl.program_id(1)
    qi = pl.program_id(2)
    q = q_ref[0] * (scale * LOG2_E)
    lane = lax.broadcasted_iota(jnp.int32, q.shape, 1)
    kr = lax.broadcasted_iota(jnp.int32, (tk, tk), 0)
    kc = lax.broadcasted_iota(jnp.int32, (tk, tk), 1)
    neg_after = jnp.where(kr > kc, -1.0, 0.0).astype(BF16)
    causal = kc < kr
    heads = range(LANES // HEAD_DIM)
    qh = [jnp.where((lane >= h * HEAD_DIM) & (lane < (h + 1) * HEAD_DIM), q, 0.0).astype(BF16) for h in heads]
    bias = [bias_ref[2 * hp + h] * LOG2_E for h in heads]
    band = tq // tk
    qblk = [[qh[h][j * tk:(j + 1) * tk] for h in heads] for j in range(band)]

    def tile_step(kb, qs, carry, masked=False, work=iter(())):
        start = pl.multiple_of(kb * tk, tk)
        ks = k_ref[0, pl.ds(start, tk), :].astype(BF16)
        vs = v_ref[0, pl.ds(start, tk), :].astype(BF16)
        out = []
        for h in heads:
            next(work, None)
            acc, run = carry[2 * h], carry[2 * h + 1]
            z = lax.dot_general(qs[h], ks, (((1,), (1,)), ((), ())), preferred_element_type=F32) + bias[h]
            sp = jnp.maximum(jnp.log2(1.0 + jnp.exp2(jnp.minimum(z, SB_EXP2_CAP))), z)
            spm = jnp.where(causal, sp, 0.0) if masked else sp
            later = jnp.dot(spm.astype(BF16), neg_after, preferred_element_type=F32)
            next(work, None)
            w = jnp.exp2(z - sp + later + run)
            if masked:
                w = jnp.where(causal, w, 0.0)
            out.append(acc + jnp.dot(w.astype(BF16), vs, preferred_element_type=F32))
            out.append(run - jnp.sum(spm, axis=1, keepdims=True))
        return tuple(out)

    blocks = []
    for j in range(band):
        cr = (jnp.zeros((tk, LANES), F32), jnp.zeros((tk, 1), F32)) * len(heads)
        cr = tile_step(qi * band + j, qblk[j], cr, masked=True, work=interleave)
        for jj in reversed(range(j)):
            cr = tile_step(qi * band + jj, qblk[j], cr, work=interleave)
        blocks.append(cr)
    carry = tuple(jnp.concatenate([blk[n] for blk in blocks], axis=0) for n in range(2 * len(heads)))
    for _ in interleave:
        pass

    def band_step(i, cr):
        for m in range(band):
            cr = tile_step((qi - 1 - i) * band + (band - 1 - m), qh, cr)
        return cr

    carry = lax.fori_loop(0, qi, band_step, carry)
    o_ref[0] = jnp.where(lane < HEAD_DIM, carry[0], carry[2])


def _sb_prompt(q, k, v, bias):
    b, t, hd = q.shape
    tq = min(SB_TILE_Q, t)
    tk = min(SB_TILE_K, t)
    assert t % tq == 0 and tq % tk == 0 and hd % LANES == 0
    kern = functools.partial(_sb_prompt_kernel, tq=tq, tk=tk, scale=HEAD_DIM ** -0.5)
    qspec = pl.BlockSpec((1, tq, LANES), lambda bi, hp, qi: (bi, qi, hp))
    kvspec = pl.BlockSpec((1, t, LANES), lambda bi, hp, qi: (bi, 0, hp))
    return pl.pallas_call(
        kern, grid=(b, hd // LANES, t // tq),
        in_specs=[pl.BlockSpec(memory_space=pltpu.SMEM), qspec, kvspec, kvspec],
        out_specs=qspec, out_shape=jax.ShapeDtypeStruct((b, t, hd), F32),
        compiler_params=_cparams(("parallel", "parallel", "arbitrary")), name="sb_prompt")(bias, q, k, v)


def _sb_sample_body(q_ref, bias_ref, after_ref, prior_ref, k_refs, v_refs, o_ref, qt_ref, *, n_heads, scale):
    pages = len(k_refs)
    page = k_refs[0].shape[2]
    q = q_ref[...] * scale
    sub = lax.broadcasted_iota(jnp.int32, q.shape, 0)
    ones = jnp.ones((n_heads, page), BF16)
    for h in range(n_heads):
        qt_ref[h] = _dot_tn(jnp.where(sub == h, q, 0.0), ones)
    score_rows = []
    for u in range(pages):
        if u % 2 == 0:
            yield
        score_rows += [jnp.sum(k_refs[u][h] * qt_ref[h], axis=0, keepdims=True) for h in range(n_heads)]
    z = jnp.concatenate(score_rows, axis=0) + bias_ref[...]
    yield
    sp = _softplus(z)
    later = _dot_sel_rhs(-sp, after_ref[...])
    tot = jnp.broadcast_to(jnp.sum(sp, axis=1, keepdims=True), sp.shape)
    passed = _dot_sel_lhs(prior_ref[...], tot)
    w = jnp.exp(z - sp + later - passed)
    yield
    ones = jnp.ones((SUBLANES, page), BF16)
    rows = []
    for h in range(n_heads):
        if h:
            yield
        acc = v_refs[0][h] * w[h:h + 1, :]
        for u in range(1, pages):
            acc = acc + v_refs[u][h] * w[u * n_heads + h:u * n_heads + h + 1, :]
        hi, lo = _split_bf16(acc)
        rows.append((_dot_nt(ones, hi) + _dot_nt(ones, lo))[0:1])
    o_ref[...] = jnp.concatenate(rows, axis=0)


def _sb_sample_kernel(pt_ref, q_ref, bias_ref, after_ref, prior_ref, *refs, pages, n_heads, scale):
    o_ref, qt_ref = refs[2 * pages:]
    for _ in _sb_sample_body(q_ref, bias_ref, after_ref, prior_ref, refs[:pages], refs[pages:2 * pages], o_ref,
                             qt_ref, n_heads=n_heads, scale=scale):
        pass


def _sb_sample_consts(bias, n_pages, n_heads, page):
    rows = n_pages * n_heads
    bias_b = jnp.broadcast_to(jnp.tile(bias.astype(F32), n_pages)[:, None], (rows, page))
    key = jnp.arange(page)
    after = (key[:, None] > key[None, :]).astype(BF16)
    r = jnp.arange(rows)
    prior = ((r[:, None] % n_heads == r[None, :] % n_heads)
             & (r[None, :] // n_heads < r[:, None] // n_heads)).astype(BF16)
    return bias_b, after, prior


def _sb_sample(q, cache_k, cache_v, page_table, bias):
    nb, n_heads, _ = q.shape
    pages = page_table.shape[1]
    page = cache_k.shape[3]
    bias_b, after, prior = _sb_sample_consts(bias, pages, n_heads, page)

    def page_spec(u):
        return pl.BlockSpec((None, n_heads, HEAD_DIM, page), lambda b, pt: (pt[b, pages - 1 - u], 0, 0, 0))

    qspec = pl.BlockSpec((None, n_heads, HEAD_DIM), lambda b, pt: (b, 0, 0))
    const = lambda a: pl.BlockSpec(a.shape, lambda b, pt: (0, 0))
    grid_spec = pltpu.PrefetchScalarGridSpec(
        num_scalar_prefetch=1, grid=(nb,),
        in_specs=[qspec, const(bias_b), const(after), const(prior)] + [page_spec(u) for u in range(pages)] * 2,
        out_specs=qspec,
        scratch_shapes=[pltpu.VMEM((n_heads, HEAD_DIM, page), F32)])
    kern = functools.partial(_sb_sample_kernel, pages=pages, n_heads=n_heads, scale=HEAD_DIM ** -0.5)
    return pl.pallas_call(
        kern, grid_spec=grid_spec, out_shape=jax.ShapeDtypeStruct((nb, n_heads, HEAD_DIM), F32),
        compiler_params=_cparams(("parallel",)), name="sb_sample")(
            page_table, q, bias_b, after, prior, *([cache_k] * pages), *([cache_v] * pages))


def _sb_fused_kernel(pt_ref, bias_ref, q_ref, k_ref, v_ref, sq_ref, sbias_ref, after_ref, prior_ref, *refs,
                     pages, n_heads, tq, tk, scale):
    o_ref, so_ref, qt_ref = refs[2 * pages:]
    sample = _sb_sample_body(sq_ref, sbias_ref, after_ref, prior_ref, refs[:pages], refs[pages:2 * pages], so_ref,
                             qt_ref, n_heads=n_heads, scale=scale)
    _sb_prompt_kernel(bias_ref, q_ref, k_ref, v_ref, o_ref, tq=tq, tk=tk, scale=scale, interleave=sample)


def _sb_fused(qp, kp, vp, bias, qs, cache_k, cache_v, page_table):
    b, t, hd = qp.shape
    nb, n_heads, _ = qs.shape
    pages = page_table.shape[1]
    page = cache_k.shape[3]
    tq, tk = min(SB_TILE_Q, t), min(SB_TILE_K, t)
    n_hp, n_q = hd // LANES, t // tq
    assert b * n_hp * n_q == nb and t % tq == 0 and tq % tk == 0
    bias_b, after, prior = _sb_sample_consts(bias, pages, n_heads, page)
    seq = lambda bi, hp, qi: (bi * n_hp + hp) * n_q + qi

    def page_spec(u):
        return pl.BlockSpec((None, n_heads, HEAD_DIM, page),
                            lambda bi, hp, qi, pt: (pt[seq(bi, hp, qi), pages - 1 - u], 0, 0, 0))

    qspec = pl.BlockSpec((1, tq, LANES), lambda bi, hp, qi, pt: (bi, qi, hp))
    kvspec = pl.BlockSpec((1, t, LANES), lambda bi, hp, qi, pt: (bi, 0, hp))
    sqspec = pl.BlockSpec((None, n_heads, HEAD_DIM), lambda bi, hp, qi, pt: (seq(bi, hp, qi), 0, 0))
    const = lambda a: pl.BlockSpec(a.shape, lambda bi, hp, qi, pt: (0, 0))
    grid_spec = pltpu.PrefetchScalarGridSpec(
        num_scalar_prefetch=1, grid=(b, n_hp, n_q),
        in_specs=[pl.BlockSpec(memory_space=pltpu.SMEM), qspec, kvspec, kvspec, sqspec, const(bias_b),
                  const(after), const(prior)] + [page_spec(u) for u in range(pages)] * 2,
        out_specs=[qspec, sqspec],
        scratch_shapes=[pltpu.VMEM((n_heads, HEAD_DIM, page), F32)])
    kern = functools.partial(_sb_fused_kernel, pages=pages, n_heads=n_heads, tq=tq, tk=tk, scale=HEAD_DIM ** -0.5)
    return pl.pallas_call(
        kern, grid_spec=grid_spec,
        out_shape=[jax.ShapeDtypeStruct((b, t, hd), F32), jax.ShapeDtypeStruct((nb, n_heads, HEAD_DIM), F32)],
        compiler_params=_cparams(("parallel", "parallel", "arbitrary")), name="sb_fused")(
            page_table, bias, qp, kp, vp, qs, bias_b, after, prior, *([cache_k] * pages), *([cache_v] * pages))


def _sb_attention(prompt, sample, page_table):
    qp, kp, vp, bias = prompt
    qs, cache_k, cache_v, _ = sample
    b, t, hd = qp.shape
    steps = b * (hd // LANES) * (t // min(SB_TILE_Q, t))
    if steps == qs.shape[0]:
        return _sb_fused(qp, kp, vp, bias, qs, cache_k, cache_v, page_table)
    return _sb_prompt(qp, kp, vp, bias), _sb_sample(qs, cache_k, cache_v, page_table, bias)


RWKV_PREP_PARAMS = ("mu", "w0", "a0", "k_k", "k_a", "w_w2", "w_a2", "w_g2", "ones")


def _rwkv_prep_math(pb, prev, mu_ref, w0_ref, a0_ref, kk_ref, ka_ref, ww2_ref, wa2_ref, wg2_ref, ones_ref):
    pm = pb + mu_ref[...] * (prev - pb)
    d = w0_ref.shape[1]
    r = pm[:, 0:d]
    kr = pm[:, d:2 * d]
    vr = pm[:, 2 * d:3 * d]
    wad = pm[:, 3 * d:3 * d + LANES]
    gd = pm[:, 3 * d + LANES:]
    z_w = w0_ref[...] + _dot(jnp.tanh(wad), ww2_ref[...])
    log_decay = -math.exp(-0.5) * jax.nn.sigmoid(z_w)
    iclr = jax.nn.sigmoid(a0_ref[...] + _dot(wad, wa2_ref[...]))
    gate = _dot(jax.nn.sigmoid(gd), wg2_ref[...])
    kkr = kr * kk_ref[...]
    ss = _dot(kkr * kkr, ones_ref[...])
    kk = kkr * lax.rsqrt(jnp.maximum(ss, 1e-24))
    kf = kr * (1.0 + (iclr - 1.0) * ka_ref[...])
    return r, kf, vr, log_decay, kk, kk * iclr, gate


def _rwkv_prep_kernel(pb_ref, prev_ref, *refs):
    n = len(RWKV_PREP_PARAMS)
    outs = _rwkv_prep_math(pb_ref[...], prev_ref[...], *refs[:n])
    for o_ref, val in zip(refs[n:], outs):
        o_ref[...] = val


def _rwkv_prep(pb, prev, p):
    rows, c = pb.shape
    d = p["w0"].shape[1]
    consts = [p[n] for n in RWKV_PREP_PARAMS]
    return pl.pallas_call(
        _rwkv_prep_kernel, out_shape=[jax.ShapeDtypeStruct((rows, d), F32)] * 7,
        compiler_params=pltpu.CompilerParams(vmem_limit_bytes=VMEM_LIMIT_BYTES), name="rwkv_prep")(
            pb, prev, *consts)


def _rwkv_chunk_kernel(pb_ref, sh_ref, s0_ref, *refs, chunk, n_heads, n_seq):
    n_prep = len(RWKV_PREP_PARAMS)
    prep_refs = refs[:n_prep]
    rk_ref, gng_ref, gnb_ref, tri_ref, avg_ref, o_ref, sout_ref, st_ref, carry_ref = refs[n_prep:]
    ones_ref = prep_refs[-1]
    L = chunk
    c = pl.program_id(1)

    @pl.when(c == 0)
    def _():
        st_ref[...] = s0_ref[...]
        carry_ref[...] = sh_ref[...]

    pb = pb_ref[...].reshape(n_seq * L, pb_ref.shape[2])
    row = lax.broadcasted_iota(jnp.int32, pb.shape, 0)
    prev = pltpu.roll(pb, 1, axis=0)
    for s in range(n_seq):
        prev = jnp.where(row == s * L, carry_ref[s], prev)
        carry_ref[s] = pb[(s + 1) * L - 1:(s + 1) * L, :]
    r, kf, v, lw, kk, bb, gate = _rwkv_prep_math(pb, prev, *prep_refs)
    cs = _dot_sel_lhs(tri_ref[...], lw)
    ends = [cs[(s + 1) * L - 1:(s + 1) * L, :] for s in range(n_seq)]
    cs_end = jnp.concatenate([jnp.broadcast_to(e, (L, e.shape[1])) for e in ends], axis=0)
    p_in = jnp.exp(cs)
    p_inv = jnp.exp(-cs)
    p_end = jnp.exp(cs_end - cs)
    rt = r * p_in
    kt = kk * jnp.exp(cs - lw)
    bt = bb * p_inv
    kft = kf * p_inv
    kh = kf * p_end
    bh = bb * p_end
    p_last = [jnp.exp(e) for e in ends]

    row = lax.broadcasted_iota(jnp.int32, (2 * L, 2 * L), 0)
    col = lax.broadcasted_iota(jnp.int32, (2 * L, 2 * L), 1)
    t_idx = row & (L - 1)
    s_idx = col & (L - 1)
    tril = s_idx < t_idx + row // L
    row1 = lax.broadcasted_iota(jnp.int32, (L, L), 0)
    col1 = lax.broadcasted_iota(jnp.int32, (L, L), 1)
    same_blk = (row1 // RWKV_SOLVE_BLOCK) == (col1 // RWKV_SOLVE_BLOCK)

    pairs = [(s, h) for s in range(n_seq) for h in range(n_heads)]
    hs = range(len(pairs))
    rs = [slice(s * L, (s + 1) * L) for s, _ in pairs]
    sl = [slice(h * HEAD_DIM, (h + 1) * HEAD_DIM) for _, h in pairs]
    s0 = [st_ref[s, h] for s, h in pairs]
    vh = [v[rs[h], sl[h]] for h in hs]
    x = [jnp.concatenate([kt[rs[h], sl[h]], rt[rs[h], sl[h]]], axis=0) for h in hs]
    yk = [jnp.concatenate([bt[rs[h], sl[h]], kft[rs[h], sl[h]]], axis=0) for h in hs]
    gm = [jnp.where(tril, _dot_nt(x[h], yk[h]), 0.0) for h in hs]
    xm = [_dot_nt(x[h], s0[h]) for h in hs]
    rhs = [-(xm[h][:L] + _dot(gm[h][:L, L:], vh[h])) for h in hs]
    ad = [jnp.where(same_blk, gm[h][:L, :L], 0.0) for h in hs]
    z = [jnp.concatenate([gm[h][:L, :L] - ad[h], rhs[h]], axis=1) for h in hs]
    z = [z[h] - _dot(ad[h], z[h]) for h in hs]
    apow = ad
    span = 2
    while span < RWKV_SOLVE_BLOCK:
        apow = [_dot(apow[h], apow[h]) for h in hs]
        z = [z[h] + _dot(apow[h], z[h]) for h in hs]
        span *= 2
    w2 = [_dot(z[h][:, :L], z[h]) for h in hs]
    x1 = [z[h][:, L:] - w2[h][:, L:] for h in hs]
    u = [x1[h] + _dot(w2[h][:, :L], x1[h]) for h in hs]
    uv = [jnp.concatenate([u[h], vh[h]], axis=0) for h in hs]
    ys = [xm[h][L:] + _dot(gm[h][L:], uv[h]) for h in hs]
    for (s, hd), h in zip(pairs, hs):
        bk = jnp.concatenate([bh[rs[h], sl[h]], kh[rs[h], sl[h]]], axis=0)
        st_ref[s, hd] = s0[h] * p_last[s][:, sl[h]] + _dot_tn(uv[h], bk)
    y = jnp.concatenate([jnp.concatenate(ys[s * n_heads:(s + 1) * n_heads], axis=1) for s in range(n_seq)],
                        axis=0)
    mean = _dot(y, avg_ref[...])
    d = y - mean
    var = _dot(d * d, avg_ref[...])
    yn = d * lax.rsqrt(var + GN_EPS) * gng_ref[...] + gnb_ref[...]
    bonus = _dot(r * kf * rk_ref[...], ones_ref[...]) * v
    o_ref[...] = ((yn + bonus) * gate).reshape(o_ref.shape)

    @pl.when(c == pl.num_programs(1) - 1)
    def _():
        sout_ref[...] = st_ref[...]


def _rwkv_chunked(pb, shift, s0, p):
    b, t, c = pb.shape
    d = p["w0"].shape[1]
    n_heads = d // HEAD_DIM
    L = RWKV_CHUNK
    n_seq = math.gcd(RWKV_SEQS_PER_STEP, b)
    assert t % L == 0 and L // RWKV_SOLVE_BLOCK == 4
    blk = lambda w: pl.BlockSpec((n_seq, L, w), lambda bi, ci: (bi, ci, 0))
    sh_spec = pl.BlockSpec((n_seq, 1, c), lambda bi, ci: (bi, 0, 0))
    sspec = pl.BlockSpec((n_seq, n_heads, HEAD_DIM, HEAD_DIM), lambda bi, ci: (bi, 0, 0, 0))
    tri = jnp.kron(jnp.eye(n_seq, dtype=F32), jnp.tril(jnp.ones((L, L), F32))).astype(BF16)
    consts = [p[n] for n in RWKV_PREP_PARAMS] + [p["r_k"], p["gn_g"], p["gn_b"], tri, p["avg"]]
    kern = functools.partial(_rwkv_chunk_kernel, chunk=L, n_heads=n_heads, n_seq=n_seq)
    return pl.pallas_call(
        kern, grid=(b // n_seq, t // L),
        in_specs=[blk(c), sh_spec, sspec] + [_const_spec(a.shape) for a in consts],
        out_specs=[blk(d), sspec],
        out_shape=[jax.ShapeDtypeStruct((b, t, d), F32),
                   jax.ShapeDtypeStruct((b, n_heads, HEAD_DIM, HEAD_DIM), F32)],
        scratch_shapes=[pltpu.VMEM((n_seq, n_heads, HEAD_DIM, HEAD_DIM), F32), pltpu.VMEM((n_seq, 1, c), F32)],
        compiler_params=_cparams(("parallel", "arbitrary")), name="rwkv_chunk")(pb, shift, s0, *consts)


def _rwkv_step_kernel(s_ref, r_ref, k_ref, v_ref, lw_ref, kk_ref, bb_ref, g_ref, rk_ref, gng_ref, gnb_ref,
                      so_ref, o_ref, y_ref):
    r = r_ref[0]
    k = k_ref[0]
    v = v_ref[0]
    w = jnp.exp(lw_ref[0])
    kk = kk_ref[0]
    bb = bb_ref[0]
    for i in range(HEAD_DIM):
        s = s_ref[0, i]
        sa = jnp.sum(s * kk, axis=0, keepdims=True)
        s2 = s * w - sa * bb + v[i:i + 1, :] * k
        so_ref[0, i] = s2
        y_ref[i:i + 1, :] = jnp.sum(s2 * r, axis=0, keepdims=True)
    y = y_ref[...]
    mu = jnp.mean(y, axis=0, keepdims=True)
    d = y - mu
    var = jnp.mean(d * d, axis=0, keepdims=True)
    yn = d * lax.rsqrt(var + GN_EPS) * gng_ref[0] + gnb_ref[0]
    bonus = jnp.sum(r * k * rk_ref[0], axis=0, keepdims=True) * v
    o_ref[0] = (yn + bonus) * g_ref[0]


def _rwkv_step(prep, state, p):
    nb, d = prep[0].shape
    n_heads = d // HEAD_DIM
    to_heads = lambda a: a.reshape(nb, n_heads, HEAD_DIM).transpose(1, 2, 0)
    vecs = [to_heads(a) for a in prep]
    params = [jnp.broadcast_to(p[n].reshape(n_heads, HEAD_DIM, 1), (n_heads, HEAD_DIM, nb))
              for n in ("r_k", "gn_g", "gn_b")]
    st = state.transpose(1, 2, 3, 0)
    sspec = pl.BlockSpec((1, HEAD_DIM, HEAD_DIM, nb), lambda h: (h, 0, 0, 0))
    vspec = pl.BlockSpec((1, HEAD_DIM, nb), lambda h: (h, 0, 0))
    st_new, o = pl.pallas_call(
        _rwkv_step_kernel, grid=(n_heads,),
        in_specs=[sspec] + [vspec] * 10, out_specs=[sspec, vspec],
        out_shape=[jax.ShapeDtypeStruct(st.shape, F32), jax.ShapeDtypeStruct((n_heads, HEAD_DIM, nb), F32)],
        scratch_shapes=[pltpu.VMEM((HEAD_DIM, nb), F32)],
        compiler_params=_cparams(("parallel",)), name="rwkv_step")(st, *vecs, *params)
    return o.transpose(2, 0, 1).reshape(nb, d), st_new.transpose(3, 0, 1, 2)


def _s5_disc_kernel(lr_ref, li_ref, ldt_ref, ar_ref, ai_ref, cr_ref, ci_ref):
    lr = lr_ref[...]
    li = li_ref[...]
    dt = jnp.exp(ldt_ref[...])
    mag = jnp.exp(lr * dt)
    ar = mag * jnp.cos(li * dt)
    ai = mag * jnp.sin(li * dt)
    den = lr * lr + li * li
    ar_ref[...] = ar
    ai_ref[...] = ai
    cr_ref[...] = ((ar - 1.0) * lr + ai * li) / den
    ci_ref[...] = (ai * lr - (ar - 1.0) * li) / den


def _s5_discretize(lam_re, lam_im, log_dt):
    g, p = lam_re.shape
    ldt = jnp.broadcast_to(log_dt.reshape(g, 1), (g, p))
    return pl.pallas_call(_s5_disc_kernel, out_shape=[jax.ShapeDtypeStruct((g, p), F32)] * 4,
                          name="s5_disc")(lam_re, lam_im, ldt)


def _s5_mix_tail(x, y_state, dsk_ref, wo_ref, wgt_ref, g_ref, b_ref, alpha):
    y = y_state + dsk_ref[...] * x
    zg = _gelu_tanh(y).astype(BF16)
    out = (jnp.dot(zg, wo_ref[...], preferred_element_type=F32)
           * jax.nn.sigmoid(jnp.dot(zg, wgt_ref[...], preferred_element_type=F32)))
    return _layer_norm(alpha * x + out, g_ref[...], b_ref[...])


def _s5_scan_kernel(x_ref, wb_ref, wcr_ref, wci_ref, ar_ref, ai_ref, dsk_ref, wo_ref, wgt_ref, g_ref, b_ref,
                    s0r_ref, s0i_ref, perm_ref, permt_ref, o_ref, sr_out, si_out, bur, bui, sr_st, si_st,
                    *, tc, nb, alpha):
    ti = pl.program_id(0)

    @pl.when(ti == 0)
    def _():
        sr_st[...] = s0r_ref[...]
        si_st[...] = s0i_ref[...]

    x = x_ref[...].reshape(nb * tc, x_ref.shape[2])
    xb = jnp.dot(perm_ref[...], x.astype(BF16), preferred_element_type=F32).astype(BF16)
    n_blk = wb_ref.shape[0]
    half = wb_ref.shape[2] // 2
    for j in range(n_blk):
        pj = jnp.dot(xb[:, j * LANES:(j + 1) * LANES], wb_ref[j], preferred_element_type=F32)
        bur[:, j * half:(j + 1) * half] = pj[:, :half]
        bui[:, j * half:(j + 1) * half] = pj[:, half:]
    n_state = bur.shape[1]
    for s in range(n_state // S5_SLAB):
        cols = slice(s * S5_SLAB, (s + 1) * S5_SLAB)
        ar = jnp.broadcast_to(ar_ref[:, cols], (nb, S5_SLAB))
        ai = jnp.broadcast_to(ai_ref[:, cols], (nb, S5_SLAB))

        sr, si = sr_st[:, cols], si_st[:, cols]
        for t in range(tc):
            rows = slice(t * nb, (t + 1) * nb)
            sr, si = ar * sr - ai * si + bur[rows, cols], ar * si + ai * sr + bui[rows, cols]
            bur[rows, cols] = sr
            bui[rows, cols] = si
        sr_st[:, cols] = sr
        si_st[:, cols] = si
    ys = []
    for j in range(n_blk):
        ys.append(_dot(bur[:, j * half:(j + 1) * half], wcr_ref[j])
                  + _dot(bui[:, j * half:(j + 1) * half], wci_ref[j]))
    y_tb = jnp.concatenate(ys, axis=1)
    y_state = _dot_sel_lhs(permt_ref[...], y_tb)
    out = _s5_mix_tail(x, y_state, dsk_ref, wo_ref, wgt_ref, g_ref, b_ref, alpha)
    o_ref[...] = out.reshape(o_ref.shape)

    @pl.when(ti == pl.num_programs(0) - 1)
    def _():
        sr_out[...] = sr_st[...]
        si_out[...] = si_st[...]


def _s5_scan(x, s0r, s0i, sp, ln, alpha):
    nb, t, d = x.shape
    tc = min(S5_TIME_CHUNK, t)
    assert t % tc == 0 and nb == SUBLANES
    n_state = s0r.shape[1]
    r = jnp.arange(nb * tc)
    perm = ((r[:, None] // nb == r[None, :] % tc) & (r[:, None] % nb == r[None, :] // tc)).astype(BF16)
    consts = [sp["wb"], sp["wcr"], sp["wci"], sp["ar"], sp["ai"], sp["d_skip"], sp["w_out"], sp["w_gate"],
              ln[0], ln[1], s0r, s0i, perm, perm.T]
    blk = pl.BlockSpec((nb, tc, d), lambda i: (0, i, 0))
    st_spec = pl.BlockSpec((nb, n_state), lambda i: (0, 0))
    kern = functools.partial(_s5_scan_kernel, tc=tc, nb=nb, alpha=alpha)
    return pl.pallas_call(
        kern, grid=(t // tc,),
        in_specs=[blk] + [_const_spec(a.shape) for a in consts],
        out_specs=[blk, st_spec, st_spec],
        out_shape=[jax.ShapeDtypeStruct((nb, t, d), F32), jax.ShapeDtypeStruct((nb, n_state), F32),
                   jax.ShapeDtypeStruct((nb, n_state), F32)],
        scratch_shapes=[pltpu.VMEM((tc * nb, n_state), F32), pltpu.VMEM((tc * nb, n_state), F32),
                        pltpu.VMEM((nb, n_state), F32), pltpu.VMEM((nb, n_state), F32)],
        compiler_params=_cparams(("arbitrary",)), name="s5_scan")(x, *consts)


def _s5_step_kernel(x_ref, wb_ref, wcr_ref, wci_ref, ar_ref, ai_ref, dsk_ref, wo_ref, wgt_ref, g_ref, b_ref,
                    s0r_ref, s0i_ref, o_ref, sr_out, si_out, *, alpha):
    x = x_ref[...]
    xb = x.astype(BF16)
    n_blk = wb_ref.shape[0]
    half = wb_ref.shape[2] // 2
    ys = []
    for j in range(n_blk):
        cols = slice(j * half, (j + 1) * half)
        pj = jnp.dot(xb[:, j * LANES:(j + 1) * LANES], wb_ref[j], preferred_element_type=F32)
        ar = ar_ref[:, cols]
        ai = ai_ref[:, cols]
        sr = s0r_ref[:, cols]
        si = s0i_ref[:, cols]
        nsr = ar * sr - ai * si + pj[:, :half]
        nsi = ar * si + ai * sr + pj[:, half:]
        sr_out[:, cols] = nsr
        si_out[:, cols] = nsi
        ys.append(_dot(nsr, wcr_ref[j]) + _dot(nsi, wci_ref[j]))
    y_state = jnp.concatenate(ys, axis=1)
    o_ref[...] = _s5_mix_tail(x, y_state, dsk_ref, wo_ref, wgt_ref, g_ref, b_ref, alpha)


def _s5_step(x, s0r, s0i, sp, ln, alpha):
    m, d = x.shape
    n_state = s0r.shape[1]
    args = [x, sp["wb"], sp["wcr"], sp["wci"], sp["ar"], sp["ai"], sp["d_skip"], sp["w_out"], sp["w_gate"],
            ln[0], ln[1], s0r, s0i]
    kern = functools.partial(_s5_step_kernel, alpha=alpha)
    return pl.pallas_call(
        kern,
        out_shape=[jax.ShapeDtypeStruct((m, d), F32), jax.ShapeDtypeStruct((m, n_state), F32),
                   jax.ShapeDtypeStruct((m, n_state), F32)],
        compiler_params=pltpu.CompilerParams(vmem_limit_bytes=VMEM_LIMIT_BYTES), name="s5_step")(*args)


def _s5_weights(prm, i):
    ar, ai, cr, ci = _s5_discretize(prm["lam_re"][i], prm["lam_im"][i], prm["log_dt"][i])
    b_re, b_im = prm["b_re"][i], prm["b_im"][i]
    bbr = cr[..., None] * b_re - ci[..., None] * b_im
    bbi = cr[..., None] * b_im + ci[..., None] * b_re
    n_groups, n_state, n_ch = bbr.shape
    gpb = LANES // n_ch
    n_blk = n_groups // gpb
    eye = jnp.eye(gpb, dtype=F32)

    def blk_in(bb):
        w = jnp.einsum("jgpc,gh->jgchp", bb.reshape(n_blk, gpb, n_state, n_ch), eye)
        return w.reshape(n_blk, LANES, gpb * n_state)

    def blk_out(cc):
        w = jnp.einsum("jgcp,gh->jhpgc", cc.reshape(n_blk, gpb, n_ch, n_state), eye)
        return w.reshape(n_blk, gpb * n_state, LANES)

    return dict(
        wb=jnp.concatenate([blk_in(bbr), blk_in(bbi)], axis=-1).astype(BF16),
        wcr=blk_out(prm["c_re"][i]).astype(BF16), wci=(-blk_out(prm["c_im"][i])).astype(BF16),
        ar=ar.reshape(1, -1), ai=ai.reshape(1, -1), d_skip=prm["d_skip"][i].reshape(1, -1),
        w_out=prm["w_glu_out"][i].astype(BF16), w_gate=prm["w_glu_gate"][i].astype(BF16))


def _rwkv_weights(prm, i, d_rwkv):
    rank_d = prm["w_w2"].shape[1]
    rank_a = prm["w_a2"].shape[1]
    assert rank_d + rank_a == LANES
    zeros_d = jnp.zeros((rank_d, d_rwkv), F32)
    zeros_a = jnp.zeros((rank_a, d_rwkv), F32)
    head = jnp.arange(d_rwkv) // HEAD_DIM
    ones = (head[:, None] == head[None, :]).astype(F32)
    row = lambda a: a.reshape(1, -1).astype(F32)
    return dict(
        mu=row(prm["mu_shift"][i]), w0=row(prm["w0"][i]), a0=row(prm["a0"][i]), k_k=row(prm["k_k"][i]),
        k_a=row(prm["k_a"][i]), r_k=row(prm["r_k"][i]), gn_g=row(prm["gn_g"][i]), gn_b=row(prm["gn_b"][i]),
        w_w2=jnp.concatenate([prm["w_w2"][i], zeros_a], axis=0).astype(BF16),
        w_a2=jnp.concatenate([zeros_d, prm["w_a2"][i]], axis=0).astype(BF16),
        w_g2=prm["w_g2"][i].astype(BF16), ones=ones.astype(BF16), avg=(ones / HEAD_DIM).astype(BF16))


def _trunk(x, cache_k, cache_v, wkv0, shift0, s50, prm, w):
    bsz, t, d = x.shape
    depth = prm["ln_g"].shape[0]
    alpha = (2.0 * depth) ** 0.25
    d_sb = prm["sb_bias"].shape[1] * HEAD_DIM
    d_rwkv = prm["w0"].shape[1]
    splits = (0, d_sb, 2 * d_sb, 3 * d_sb, prm["w_in_even"].shape[2])
    ln = lambda layer, j: (prm["ln_g"][layer, j].reshape(1, d), prm["ln_b"][layer, j].reshape(1, d))
    prompt = cache_k is None
    rows = x.reshape(bsz * t, d)
    out_k, out_v, out_wkv, out_shift, out_s5 = [], [], [], [], []
    for layer in range(depth):
        i = layer // 2
        ffn1 = tuple(_layer_weight(w[n], layer) for n in ("ffn1_wg", "ffn1_wu", "ffn1_wd"))
        ffn2 = tuple(_layer_weight(w[n], layer) for n in ("ffn2_wg", "ffn2_wu", "ffn2_wd"))
        if layer % 2 == 0:
            group_rows = t if prompt else bsz
            x1, q, k, v, pb, k_fm, v_fm = _stage(rows, ffn1, ln(layer, 0), alpha,
                                                 w_in=_layer_weight(w["w_in_even"], i),
                                                 proj_splits=splits, proj_transposed=(1, 2),
                                                 group_rows=group_rows)
            rp = _rwkv_weights(prm, i, d_rwkv)
            pb3 = pb.reshape(bsz, t, -1)
            if prompt:
                o_sb = yield (q.reshape(bsz, t, d_sb), k.reshape(bsz, t, d_sb), v.reshape(bsz, t, d_sb),
                              prm["sb_bias"][i])
                o_rwkv, wkv = _rwkv_chunked(pb3, shift0[i][:, None, :], wkv0[i], rp)
                o_rwkv = o_rwkv.reshape(bsz * t, d_rwkv)
            else:
                assert t == 1
                o_sb = yield (q.reshape(bsz, d_sb // HEAD_DIM, HEAD_DIM), cache_k[i], cache_v[i], prm["sb_bias"][i])
                o_rwkv, wkv = _rwkv_step(_rwkv_prep(pb, shift0[i], rp), wkv0[i], rp)
            o_sb = o_sb.reshape(bsz * t, d_sb)
            for fm, out in ((k_fm, out_k), (v_fm, out_v)):
                fm = fm.reshape(-1, d_sb // HEAD_DIM, HEAD_DIM, group_rows)
                if prompt:
                    out.append(fm.transpose(0, 3, 1, 2))
                else:
                    out.append(fm[0].transpose(2, 0, 1)[:, None])
            out_wkv.append(wkv)
            out_shift.append(pb3[:, -1])
            assert d_sb == d_rwkv
            mix = (o_sb, o_rwkv, _layer_weight(w["w_out_even"], i, 0, 2),
                   _layer_weight(w["w_out_even"], i, 1, 2), *ln(layer, 1))
            (rows,) = _stage(x1, ffn2, ln(layer, 2), alpha, mix=mix)
        else:
            sp = _s5_weights(prm, i)
            s0r = s50[i][..., 0].reshape(bsz, -1)
            s0i = s50[i][..., 1].reshape(bsz, -1)
            (x1,) = _stage(rows, ffn1, ln(layer, 0), alpha)
            if t > 1:
                x2, sr, si = _s5_scan(x1.reshape(bsz, t, d), s0r, s0i, sp, ln(layer, 1), alpha)
                x2 = x2.reshape(bsz * t, d)
            else:
                x2, sr, si = _s5_step(x1, s0r, s0i, sp, ln(layer, 1), alpha)
            n_groups = prm["lam_re"].shape[1]
            out_s5.append(jnp.stack([sr.reshape(bsz, n_groups, -1), si.reshape(bsz, n_groups, -1)], axis=-1))
            (rows,) = _stage(x2, ffn2, ln(layer, 2), alpha)
    y = rows.reshape(bsz, t, d)
    return (y, jnp.stack(out_k), jnp.stack(out_v), jnp.stack(out_wkv), jnp.stack(out_shift),
            jnp.stack(out_s5))


def kernel(x_prompt, x_sample, cache_k_sb, cache_v_sb, page_table, state_wkv, state_shift, state_s5, ln_g, ln_b, ffn1_wg, ffn1_wu, ffn1_wd, ffn2_wg, ffn2_wu, ffn2_wd, w_in_even, w_out_even, sb_bias, mu_shift, w0, w_w2, a0, w_a2, w_g2, k_k, k_a, r_k, gn_g, gn_b, lam_re, lam_im, log_dt, b_re, b_im, c_re, c_im, d_skip, w_glu_out, w_glu_gate):
    prm = dict(ln_g=ln_g, ln_b=ln_b, sb_bias=sb_bias, mu_shift=mu_shift, w0=w0, w_w2=w_w2, a0=a0, w_a2=w_a2,
               w_g2=w_g2, k_k=k_k, k_a=k_a, r_k=r_k.reshape(r_k.shape[0], -1), gn_g=gn_g, gn_b=gn_b,
               lam_re=lam_re, lam_im=lam_im, log_dt=log_dt, b_re=b_re, b_im=b_im, c_re=c_re, c_im=c_im,
               d_skip=d_skip, w_glu_out=w_glu_out, w_glu_gate=w_glu_gate, w_in_even=w_in_even)
    w = dict(ffn1_wg=ffn1_wg, ffn1_wu=ffn1_wu, ffn1_wd=ffn1_wd, ffn2_wg=ffn2_wg, ffn2_wu=ffn2_wu,
             ffn2_wd=ffn2_wd, w_in_even=w_in_even, w_out_even=w_out_even)
    w = {name: a.astype(BF16) for name, a in w.items()}
    nb = x_prompt.shape[0]
    n_even, n_odd = state_wkv.shape[0], state_s5.shape[0]
    wkv0 = jnp.zeros((n_even, nb) + state_wkv.shape[2:], F32)
    shift0 = jnp.zeros((n_even, nb, state_shift.shape[2]), F32)
    s50 = jnp.zeros((n_odd, nb) + state_s5.shape[2:], F32)
    cache_kt = cache_k_sb.transpose(0, 1, 3, 4, 2)
    cache_vt = cache_v_sb.transpose(0, 1, 3, 4, 2)
    trunk_p = _trunk(x_prompt, None, None, wkv0, shift0, s50, prm, w)
    trunk_s = _trunk(x_sample, cache_kt, cache_vt, state_wkv, state_shift, state_s5, prm, w)
    def advance(trunk, value):
        try:
            return trunk.send(value), None
        except StopIteration as done:
            return None, done.value

    (req_p, out_p), (req_s, out_s) = advance(trunk_p, None), advance(trunk_s, None)
    while out_p is None:
        o_p, o_s = _sb_attention(req_p, req_s, page_table)
        (req_p, out_p), (req_s, out_s) = advance(trunk_p, o_p), advance(trunk_s, o_s)
    y_p, p_k, p_v, p_wkv, p_shift, p_s5 = out_p
    y_s, s_k, s_v, s_wkv, s_shift, s_s5 = out_s
    return (y_p, y_s, p_k, p_v, p_wkv, p_shift, p_s5, s_k, s_v, s_wkv, s_shift, s_s5)
```

```python
import functools
import math

import jax
import jax.numpy as jnp
from jax import lax
from jax.experimental import pallas as pl
from jax.experimental.pallas import tpu as pltpu

F32 = jnp.float32
BF16 = jnp.bfloat16

HEAD_DIM = 64
LOG2_E = 1.0 / math.log(2.0)
LN_EPS = 1e-5
GN_EPS = 64e-5
S5_GROUP = 16
S5_STATE = 64
LANES = 128
SUBLANES = 8
VMEM_LIMIT_BYTES = 56 * 1024 * 1024

ROW_TILE = 512
FF_CHUNKS = 2
SB_TILE_Q = 512
SB_TILE_K = 256
SB_EXP2_CAP = 100.0
RWKV_CHUNK = 64
RWKV_SOLVE_BLOCK = 16
RWKV_SEQS_PER_STEP = 4
S5_TIME_CHUNK = 32
S5_SLAB = 1024


def _cparams(sem):
    return pltpu.CompilerParams(dimension_semantics=sem, vmem_limit_bytes=VMEM_LIMIT_BYTES)


def _const_spec(shape):
    nd = len(shape)
    return pl.BlockSpec(shape, lambda *_: (0,) * nd, pipeline_mode=pl.Buffered(1))


def _dot(a, b):
    return jnp.dot(a.astype(BF16), b.astype(BF16), preferred_element_type=F32)


def _dot_nt(a, b):
    return lax.dot_general(a.astype(BF16), b.astype(BF16), (((1,), (1,)), ((), ())),
                           preferred_element_type=F32)


def _dot_tn(a, b):
    return lax.dot_general(a.astype(BF16), b.astype(BF16), (((0,), (0,)), ((), ())),
                           preferred_element_type=F32)


def _split_bf16(a):
    hi = a.astype(BF16)
    lo = (a - hi.astype(F32)).astype(BF16)
    return hi, lo


def _dot_sel_rhs(a, sel):
    hi, lo = _split_bf16(a)
    return (jnp.dot(hi, sel, preferred_element_type=F32) + jnp.dot(lo, sel, preferred_element_type=F32))


def _dot_sel_lhs(sel, a):
    hi, lo = _split_bf16(a)
    return (jnp.dot(sel, hi, preferred_element_type=F32) + jnp.dot(sel, lo, preferred_element_type=F32))


def _layer_norm(y, g, b):
    mu = jnp.mean(y, axis=-1, keepdims=True)
    d = y - mu
    var = jnp.mean(d * d, axis=-1, keepdims=True)
    return d * lax.rsqrt(var + LN_EPS) * g + b


def _softplus(z):
    return jnp.maximum(z, 0.0) + jnp.log1p(jnp.exp(-jnp.abs(z)))


def _gelu_tanh(y):
    c = math.sqrt(2.0 / math.pi)
    return 0.5 * y * (1.0 + jnp.tanh(c * (y + 0.044715 * (y * y * y))))


def _stage_kernel(*refs, alpha, has_mix, proj_splits, proj_transposed):
    it = iter(refs)
    x_ref = next(it)
    if has_mix:
        a_ref, b_ref, wa_ref, wb_ref, gm_ref, bm_ref = (next(it) for _ in range(6))
    wg_ref, wu_ref, wd_ref, g_ref, b2_ref = (next(it) for _ in range(5))
    if proj_splits:
        win_ref = next(it)
    o_ref = next(it)
    proj_refs = [next(it) for _ in proj_splits[1:]] if proj_splits else []
    proj_t_refs = [next(it) for _ in proj_transposed]

    x = x_ref[...]
    if has_mix:
        mix = _dot(a_ref[...], wa_ref[...]) + _dot(b_ref[...], wb_ref[...])
        x = _layer_norm(alpha * x + mix, gm_ref[...], bm_ref[...])
    xb = x.astype(BF16)
    ff = wg_ref.shape[1]
    fc = ff // FF_CHUNKS
    acc = None
    for c in range(0, ff, fc):
        gate = jnp.dot(xb, wg_ref[:, c:c + fc], preferred_element_type=F32)
        up = jnp.dot(xb, wu_ref[:, c:c + fc], preferred_element_type=F32)
        h = (gate * jax.nn.sigmoid(gate) * up).astype(BF16)
        part = jnp.dot(h, wd_ref[c:c + fc, :], preferred_element_type=F32)
        acc = part if acc is None else acc + part
    y = _layer_norm(alpha * x + 0.5 * acc, g_ref[...], b2_ref[...])
    o_ref[...] = y
    if proj_splits:
        proj = jnp.dot(y.astype(BF16), win_ref[...], preferred_element_type=F32)
        for r, lo, hi in zip(proj_refs, proj_splits[:-1], proj_splits[1:]):
            r[...] = proj[:, lo:hi].astype(r.dtype)
        for r, j in zip(proj_t_refs, proj_transposed):
            r[0] = proj[:, proj_splits[j]:proj_splits[j + 1]].T


def _layer_weight(a, layer, part=0, parts=1):
    rows = a.shape[1] // parts
    index = (layer, part, 0)
    return a, pl.BlockSpec((None, rows, a.shape[2]), lambda *_: index, pipeline_mode=pl.Buffered(1))


def _stage(x, ffn, ln_ffn, alpha, mix=None, w_in=None, proj_splits=(), proj_transposed=(), group_rows=None):
    m, d = x.shape
    tm = min(ROW_TILE, m, group_rows or m)
    assert m % tm == 0
    row = lambda w: pl.BlockSpec((tm, w), lambda i: (i, 0))
    args, specs = [x], [row(d)]
    if mix is not None:
        a, b, wa, wb, gm, bm = mix
        args += [a, b, wa[0], wb[0], gm, bm]
        specs += [row(a.shape[1]), row(b.shape[1]), wa[1], wb[1], _const_spec(gm.shape), _const_spec(bm.shape)]
    args += [w[0] for w in ffn] + [*ln_ffn]
    specs += [w[1] for w in ffn] + [_const_spec(p.shape) for p in ln_ffn]
    out_shape = [jax.ShapeDtypeStruct((m, d), F32)]
    out_specs = [row(d)]
    if proj_splits:
        args.append(w_in[0])
        specs.append(w_in[1])
        for lo, hi in zip(proj_splits[:-1], proj_splits[1:]):
            narrow = len(out_shape) - 1 in proj_transposed
            out_shape.append(jax.ShapeDtypeStruct((m, hi - lo), BF16 if narrow else F32))
            out_specs.append(row(hi - lo))
        for j in proj_transposed:
            width = proj_splits[j + 1] - proj_splits[j]
            tiles = group_rows // tm
            assert group_rows % tm == 0 and m % group_rows == 0
            out_shape.append(jax.ShapeDtypeStruct((m // group_rows, width, group_rows), F32))
            out_specs.append(pl.BlockSpec((1, width, tm), lambda i, tiles=tiles: (i // tiles, 0, i % tiles)))
    kern = functools.partial(_stage_kernel, alpha=alpha, has_mix=mix is not None,
                             proj_splits=tuple(proj_splits), proj_transposed=tuple(proj_transposed))
    outs = pl.pallas_call(
        kern, grid=(m // tm,), in_specs=specs, out_specs=out_specs, out_shape=out_shape,
        compiler_params=_cparams(("parallel",)), name="stage")(*args)
    return outs


def _sb_prompt_kernel(bias_ref, q_ref, k_ref, v_ref, o_ref, *, tq, tk, scale, interleave=iter(())):
    hp = pl.program_id(1)
    qi = pl.program_id(2)
    q = q_ref[0] * (scale * LOG2_E)
    lane = lax.broadcasted_iota(jnp.int32, q.shape, 1)
    kr = lax.broadcasted_iota(jnp.int32, (tk, tk), 0)
    kc = lax.broadcasted_iota(jnp.int32, (tk, tk), 1)
    neg_after = jnp.where(kr > kc, -1.0, 0.0).astype(BF16)
    causal = kc < kr
    heads = range(LANES // HEAD_DIM)
    qh = [jnp.where((lane >= h * HEAD_DIM) & (lane < (h + 1) * HEAD_DIM), q, 0.0).astype(BF16) for h in heads]
    bias = [bias_ref[2 * hp + h] * LOG2_E for h in heads]
    band = tq // tk
    qblk = [[qh[h][j * tk:(j + 1) * tk] for h in heads] for j in range(band)]

    def tile_step(kb, qs, carry, masked=False, work=iter(())):
        start = pl.multiple_of(kb * tk, tk)
        ks = k_ref[0, pl.ds(start, tk), :].astype(BF16)
        vs = v_ref[0, pl.ds(start, tk), :].astype(BF16)
        out = []
        for h in heads:
            next(work, None)
            acc, run = carry[2 * h], carry[2 * h + 1]
            z = lax.dot_general(qs[h], ks, (((1,), (1,)), ((), ())), preferred_element_type=F32) + bias[h]
            sp = jnp.maximum(jnp.log2(1.0 + jnp.exp2(jnp.minimum(z, SB_EXP2_CAP))), z)
            spm = jnp.where(causal, sp, 0.0) if masked else sp
            later = jnp.dot(spm.astype(BF16), neg_after, preferred_element_type=F32)
            next(work, None)
            w = jnp.exp2(z - sp + later + run)
            if masked:
                w = jnp.where(causal, w, 0.0)
            out.append(acc + jnp.dot(w.astype(BF16), vs, preferred_element_type=F32))
            out.append(run - jnp.sum(spm, axis=1, keepdims=True))
        return tuple(out)

    blocks = []
    for j in range(band):
        cr = (jnp.zeros((tk, LANES), F32), jnp.zeros((tk, 1), F32)) * len(heads)
        cr = tile_step(qi * band + j, qblk[j], cr, masked=True, work=interleave)
        for jj in reversed(range(j)):
            cr = tile_step(qi * band + jj, qblk[j], cr, work=interleave)
        blocks.append(cr)
    carry = tuple(jnp.concatenate([blk[n] for blk in blocks], axis=0) for n in range(2 * len(heads)))
    for _ in interleave:
        pass

    def band_step(i, cr):
        for m in range(band):
            cr = tile_step((qi - 1 - i) * band + (band - 1 - m), qh, cr)
        return cr

    carry = lax.fori_loop(0, qi, band_step, carry)
    o_ref[0] = jnp.where(lane < HEAD_DIM, carry[0], carry[2])


def _sb_prompt(q, k, v, bias):
    b, t, hd = q.shape
    tq = min(SB_TILE_Q, t)
    tk = min(SB_TILE_K, t)
    assert t % tq == 0 and tq % tk == 0 and hd % LANES == 0
    kern = functools.partial(_sb_prompt_kernel, tq=tq, tk=tk, scale=HEAD_DIM ** -0.5)
    qspec = pl.BlockSpec((1, tq, LANES), lambda bi, hp, qi: (bi, qi, hp))
    kvspec = pl.BlockSpec((1, t, LANES), lambda bi, hp, qi: (bi, 0, hp))
    return pl.pallas_call(
        kern, grid=(b, hd // LANES, t // tq),
        in_specs=[pl.BlockSpec(memory_space=pltpu.SMEM), qspec, kvspec, kvspec],
        out_specs=qspec, out_shape=jax.ShapeDtypeStruct((b, t, hd), F32),
        compiler_params=_cparams(("parallel", "parallel", "arbitrary")), name="sb_prompt")(bias, q, k, v)


def _sb_sample_body(q_ref, bias_ref, after_ref, prior_ref, k_refs, v_refs, o_ref, qt_ref, *, n_heads, scale):
    pages = len(k_refs)
    page = k_refs[0].shape[2]
    q = q_ref[...] * scale
    sub = lax.broadcasted_iota(jnp.int32, q.shape, 0)
    ones = jnp.ones((n_heads, page), BF16)
    for h in range(n_heads):
        qt_ref[h] = _dot_tn(jnp.where(sub == h, q, 0.0), ones)
    score_rows = []
    for u in range(pages):
        if u % 2 == 0:
            yield
        score_rows += [jnp.sum(k_refs[u][h] * qt_ref[h], axis=0, keepdims=True) for h in range(n_heads)]
    z = jnp.concatenate(score_rows, axis=0) + bias_ref[...]
    yield
    sp = _softplus(z)
    later = _dot_sel_rhs(-sp, after_ref[...])
    tot = jnp.broadcast_to(jnp.sum(sp, axis=1, keepdims=True), sp.shape)
    passed = _dot_sel_lhs(prior_ref[...], tot)
    w = jnp.exp(z - sp + later - passed)
    yield
    ones = jnp.ones((SUBLANES, page), BF16)
    rows = []
    for h in range(n_heads):
        if h:
            yield
        acc = v_refs[0][h] * w[h:h + 1, :]
        for u in range(1, pages):
            acc = acc + v_refs[u][h] * w[u * n_heads + h:u * n_heads + h + 1, :]
        hi, lo = _split_bf16(acc)
        rows.append((_dot_nt(ones, hi) + _dot_nt(ones, lo))[0:1])
    o_ref[...] = jnp.concatenate(rows, axis=0)


def _sb_sample_kernel(pt_ref, q_ref, bias_ref, after_ref, prior_ref, *refs, pages, n_heads, scale):
    o_ref, qt_ref = refs[2 * pages:]
    for _ in _sb_sample_body(q_ref, bias_ref, after_ref, prior_ref, refs[:pages], refs[pages:2 * pages], o_ref,
                             qt_ref, n_heads=n_heads, scale=scale):
        pass


def _sb_sample_consts(bias, n_pages, n_heads, page):
    rows = n_pages * n_heads
    bias_b = jnp.broadcast_to(jnp.tile(bias.astype(F32), n_pages)[:, None], (rows, page))
    key = jnp.arange(page)
    after = (key[:, None] > key[None, :]).astype(BF16)
    r = jnp.arange(rows)
    prior = ((r[:, None] % n_heads == r[None, :] % n_heads)
             & (r[None, :] // n_heads < r[:, None] // n_heads)).astype(BF16)
    return bias_b, after, prior


def _sb_sample(q, cache_k, cache_v, page_table, bias):
    nb, n_heads, _ = q.shape
    pages = page_table.shape[1]
    page = cache_k.shape[3]
    bias_b, after, prior = _sb_sample_consts(bias, pages, n_heads, page)

    def page_spec(u):
        return pl.BlockSpec((None, n_heads, HEAD_DIM, page), lambda b, pt: (pt[b, pages - 1 - u], 0, 0, 0))

    qspec = pl.BlockSpec((None, n_heads, HEAD_DIM), lambda b, pt: (b, 0, 0))
    const = lambda a: pl.BlockSpec(a.shape, lambda b, pt: (0, 0))
    grid_spec = pltpu.PrefetchScalarGridSpec(
        num_scalar_prefetch=1, grid=(nb,),
        in_specs=[qspec, const(bias_b), const(after), const(prior)] + [page_spec(u) for u in range(pages)] * 2,
        out_specs=qspec,
        scratch_shapes=[pltpu.VMEM((n_heads, HEAD_DIM, page), F32)])
    kern = functools.partial(_sb_sample_kernel, pages=pages, n_heads=n_heads, scale=HEAD_DIM ** -0.5)
    return pl.pallas_call(
        kern, grid_spec=grid_spec, out_shape=jax.ShapeDtypeStruct((nb, n_heads, HEAD_DIM), F32),
        compiler_params=_cparams(("parallel",)), name="sb_sample")(
            page_table, q, bias_b, after, prior, *([cache_k] * pages), *([cache_v] * pages))


def _sb_fused_kernel(pt_ref, bias_ref, q_ref, k_ref, v_ref, sq_ref, sbias_ref, after_ref, prior_ref, *refs,
                     pages, n_heads, tq, tk, scale):
    o_ref, so_ref, qt_ref = refs[2 * pages:]
    sample = _sb_sample_body(sq_ref, sbias_ref, after_ref, prior_ref, refs[:pages], refs[pages:2 * pages], so_ref,
                             qt_ref, n_heads=n_heads, scale=scale)
    _sb_prompt_kernel(bias_ref, q_ref, k_ref, v_ref, o_ref, tq=tq, tk=tk, scale=scale, interleave=sample)


def _sb_fused(qp, kp, vp, bias, qs, cache_k, cache_v, page_table):
    b, t, hd = qp.shape
    nb, n_heads, _ = qs.shape
    pages = page_table.shape[1]
    page = cache_k.shape[3]
    tq, tk = min(SB_TILE_Q, t), min(SB_TILE_K, t)
    n_hp, n_q = hd // LANES, t // tq
    assert b * n_hp * n_q == nb and t % tq == 0 and tq % tk == 0
    bias_b, after, prior = _sb_sample_consts(bias, pages, n_heads, page)
    seq = lambda bi, hp, qi: (bi * n_hp + hp) * n_q + qi

    def page_spec(u):
        return pl.BlockSpec((None, n_heads, HEAD_DIM, page),
                            lambda bi, hp, qi, pt: (pt[seq(bi, hp, qi), pages - 1 - u], 0, 0, 0))

    qspec = pl.BlockSpec((1, tq, LANES), lambda bi, hp, qi, pt: (bi, qi, hp))
    kvspec = pl.BlockSpec((1, t, LANES), lambda bi, hp, qi, pt: (bi, 0, hp))
    sqspec = pl.BlockSpec((None, n_heads, HEAD_DIM), lambda bi, hp, qi, pt: (seq(bi, hp, qi), 0, 0))
    const = lambda a: pl.BlockSpec(a.shape, lambda bi, hp, qi, pt: (0, 0))
    grid_spec = pltpu.PrefetchScalarGridSpec(
        num_scalar_prefetch=1, grid=(b, n_hp, n_q),
        in_specs=[pl.BlockSpec(memory_space=pltpu.SMEM), qspec, kvspec, kvspec, sqspec, const(bias_b),
                  const(after), const(prior)] + [page_spec(u) for u in range(pages)] * 2,
        out_specs=[qspec, sqspec],
        scratch_shapes=[pltpu.VMEM((n_heads, HEAD_DIM, page), F32)])
    kern = functools.partial(_sb_fused_kernel, pages=pages, n_heads=n_heads, tq=tq, tk=tk, scale=HEAD_DIM ** -0.5)
    return pl.pallas_call(
        kern, grid_spec=grid_spec,
        out_shape=[jax.ShapeDtypeStruct((b, t, hd), F32), jax.ShapeDtypeStruct((nb, n_heads, HEAD_DIM), F32)],
        compiler_params=_cparams(("parallel", "parallel", "arbitrary")), name="sb_fused")(
            page_table, bias, qp, kp, vp, qs, bias_b, after, prior, *([cache_k] * pages), *([cache_v] * pages))


def _sb_attention(prompt, sample, page_table):
    qp, kp, vp, bias = prompt
    qs, cache_k, cache_v, _ = sample
    b, t, hd = qp.shape
    steps = b * (hd // LANES) * (t // min(SB_TILE_Q, t))
    if steps == qs.shape[0]:
        return _sb_fused(qp, kp, vp, bias, qs, cache_k, cache_v, page_table)
    return _sb_prompt(qp, kp, vp, bias), _sb_sample(qs, cache_k, cache_v, page_table, bias)


RWKV_PREP_PARAMS = ("mu", "w0", "a0", "k_k", "k_a", "w_w2", "w_a2", "w_g2", "ones")


def _rwkv_prep_math(pb, prev, mu_ref, w0_ref, a0_ref, kk_ref, ka_ref, ww2_ref, wa2_ref, wg2_ref, ones_ref):
    pm = pb + mu_ref[...] * (prev - pb)
    d = w0_ref.shape[1]
    r = pm[:, 0:d]
    kr = pm[:, d:2 * d]
    vr = pm[:, 2 * d:3 * d]
    wad = pm[:, 3 * d:3 * d + LANES]
    gd = pm[:, 3 * d + LANES:]
    z_w = w0_ref[...] + _dot(jnp.tanh(wad), ww2_ref[...])
    log_decay = -math.exp(-0.5) * jax.nn.sigmoid(z_w)
    iclr = jax.nn.sigmoid(a0_ref[...] + _dot(wad, wa2_ref[...]))
    gate = _dot(jax.nn.sigmoid(gd), wg2_ref[...])
    kkr = kr * kk_ref[...]
    ss = _dot(kkr * kkr, ones_ref[...])
    kk = kkr * lax.rsqrt(jnp.maximum(ss, 1e-24))
    kf = kr * (1.0 + (iclr - 1.0) * ka_ref[...])
    return r, kf, vr, log_decay, kk, kk * iclr, gate


def _rwkv_prep_kernel(pb_ref, prev_ref, *refs):
    n = len(RWKV_PREP_PARAMS)
    outs = _rwkv_prep_math(pb_ref[...], prev_ref[...], *refs[:n])
    for o_ref, val in zip(refs[n:], outs):
        o_ref[...] = val


def _rwkv_prep(pb, prev, p):
    rows, c = pb.shape
    d = p["w0"].shape[1]
    consts = [p[n] for n in RWKV_PREP_PARAMS]
    return pl.pallas_call(
        _rwkv_prep_kernel, out_shape=[jax.ShapeDtypeStruct((rows, d), F32)] * 7,
        compiler_params=pltpu.CompilerParams(vmem_limit_bytes=VMEM_LIMIT_BYTES), name="rwkv_prep")(
            pb, prev, *consts)


def _rwkv_chunk_kernel(pb_ref, sh_ref, s0_ref, *refs, chunk, n_heads, n_seq):
    n_prep = len(RWKV_PREP_PARAMS)
    prep_refs = refs[:n_prep]
    rk_ref, gng_ref, gnb_ref, tri_ref, avg_ref, o_ref, sout_ref, st_ref, carry_ref = refs[n_prep:]
    ones_ref = prep_refs[-1]
    L = chunk
    c = pl.program_id(1)

    @pl.when(c == 0)
    def _():
        st_ref[...] = s0_ref[...]
        carry_ref[...] = sh_ref[...]

    pb = pb_ref[...].reshape(n_seq * L, pb_ref.shape[2])
    row = lax.broadcasted_iota(jnp.int32, pb.shape, 0)
    prev = pltpu.roll(pb, 1, axis=0)
    for s in range(n_seq):
        prev = jnp.where(row == s * L, carry_ref[s], prev)
        carry_ref[s] = pb[(s + 1) * L - 1:(s + 1) * L, :]
    r, kf, v, lw, kk, bb, gate = _rwkv_prep_math(pb, prev, *prep_refs)
    cs = _dot_sel_lhs(tri_ref[...], lw)
    ends = [cs[(s + 1) * L - 1:(s + 1) * L, :] for s in range(n_seq)]
    cs_end = jnp.concatenate([jnp.broadcast_to(e, (L, e.shape[1])) for e in ends], axis=0)
    p_in = jnp.exp(cs)
    p_inv = jnp.exp(-cs)
    p_end = jnp.exp(cs_end - cs)
    rt = r * p_in
    kt = kk * jnp.exp(cs - lw)
    bt = bb * p_inv
    kft = kf * p_inv
    kh = kf * p_end
    bh = bb * p_end
    p_last = [jnp.exp(e) for e in ends]

    row = lax.broadcasted_iota(jnp.int32, (2 * L, 2 * L), 0)
    col = lax.broadcasted_iota(jnp.int32, (2 * L, 2 * L), 1)
    t_idx = row & (L - 1)
    s_idx = col & (L - 1)
    tril = s_idx < t_idx + row // L
    row1 = lax.broadcasted_iota(jnp.int32, (L, L), 0)
    col1 = lax.broadcasted_iota(jnp.int32, (L, L), 1)
    same_blk = (row1 // RWKV_SOLVE_BLOCK) == (col1 // RWKV_SOLVE_BLOCK)

    pairs = [(s, h) for s in range(n_seq) for h in range(n_heads)]
    hs = range(len(pairs))
    rs = [slice(s * L, (s + 1) * L) for s, _ in pairs]
    sl = [slice(h * HEAD_DIM, (h + 1) * HEAD_DIM) for _, h in pairs]
    s0 = [st_ref[s, h] for s, h in pairs]
    vh = [v[rs[h], sl[h]] for h in hs]
    x = [jnp.concatenate([kt[rs[h], sl[h]], rt[rs[h], sl[h]]], axis=0) for h in hs]
    yk = [jnp.concatenate([bt[rs[h], sl[h]], kft[rs[h], sl[h]]], axis=0) for h in hs]
    gm = [jnp.where(tril, _dot_nt(x[h], yk[h]), 0.0) for h in hs]
    xm = [_dot_nt(x[h], s0[h]) for h in hs]
    rhs = [-(xm[h][:L] + _dot(gm[h][:L, L:], vh[h])) for h in hs]
    ad = [jnp.where(same_blk, gm[h][:L, :L], 0.0) for h in hs]
    z = [jnp.concatenate([gm[h][:L, :L] - ad[h], rhs[h]], axis=1) for h in hs]
    z = [z[h] - _dot(ad[h], z[h]) for h in hs]
    apow = ad
    span = 2
    while span < RWKV_SOLVE_BLOCK:
        apow = [_dot(apow[h], apow[h]) for h in hs]
        z = [z[h] + _dot(apow[h], z[h]) for h in hs]
        span *= 2
    w2 = [_dot(z[h][:, :L], z[h]) for h in hs]
    x1 = [z[h][:, L:] - w2[h][:, L:] for h in hs]
    u = [x1[h] + _dot(w2[h][:, :L], x1[h]) for h in hs]
    uv = [jnp.concatenate([u[h], vh[h]], axis=0) for h in hs]
    ys = [xm[h][L:] + _dot(gm[h][L:], uv[h]) for h in hs]
    for (s, hd), h in zip(pairs, hs):
        bk = jnp.concatenate([bh[rs[h], sl[h]], kh[rs[h], sl[h]]], axis=0)
        st_ref[s, hd] = s0[h] * p_last[s][:, sl[h]] + _dot_tn(uv[h], bk)
    y = jnp.concatenate([jnp.concatenate(ys[s * n_heads:(s + 1) * n_heads], axis=1) for s in range(n_seq)],
                        axis=0)
    mean = _dot(y, avg_ref[...])
    d = y - mean
    var = _dot(d * d, avg_ref[...])
    yn = d * lax.rsqrt(var + GN_EPS) * gng_ref[...] + gnb_ref[...]
    bonus = _dot(r * kf * rk_ref[...], ones_ref[...]) * v
    o_ref[...] = ((yn + bonus) * gate).reshape(o_ref.shape)

    @pl.when(c == pl.num_programs(1) - 1)
    def _():
        sout_ref[...] = st_ref[...]


def _rwkv_chunked(pb, shift, s0, p):
    b, t, c = pb.shape
    d = p["w0"].shape[1]
    n_heads = d // HEAD_DIM
    L = RWKV_CHUNK
    n_seq = math.gcd(RWKV_SEQS_PER_STEP, b)
    assert t % L == 0 and L // RWKV_SOLVE_BLOCK == 4
    blk = lambda w: pl.BlockSpec((n_seq, L, w), lambda bi, ci: (bi, ci, 0))
    sh_spec = pl.BlockSpec((n_seq, 1, c), lambda bi, ci: (bi, 0, 0))
    sspec = pl.BlockSpec((n_seq, n_heads, HEAD_DIM, HEAD_DIM), lambda bi, ci: (bi, 0, 0, 0))
    tri = jnp.kron(jnp.eye(n_seq, dtype=F32), jnp.tril(jnp.ones((L, L), F32))).astype(BF16)
    consts = [p[n] for n in RWKV_PREP_PARAMS] + [p["r_k"], p["gn_g"], p["gn_b"], tri, p["avg"]]
    kern = functools.partial(_rwkv_chunk_kernel, chunk=L, n_heads=n_heads, n_seq=n_seq)
    return pl.pallas_call(
        kern, grid=(b // n_seq, t // L),
        in_specs=[blk(c), sh_spec, sspec] + [_const_spec(a.shape) for a in consts],
        out_specs=[blk(d), sspec],
        out_shape=[jax.ShapeDtypeStruct((b, t, d), F32),
                   jax.ShapeDtypeStruct((b, n_heads, HEAD_DIM, HEAD_DIM), F32)],
        scratch_shapes=[pltpu.VMEM((n_seq, n_heads, HEAD_DIM, HEAD_DIM), F32), pltpu.VMEM((n_seq, 1, c), F32)],
        compiler_params=_cparams(("parallel", "arbitrary")), name="rwkv_chunk")(pb, shift, s0, *consts)


def _rwkv_step_kernel(s_ref, r_ref, k_ref, v_ref, lw_ref, kk_ref, bb_ref, g_ref, rk_ref, gng_ref, gnb_ref,
                      so_ref, o_ref, y_ref):
    r = r_ref[0]
    k = k_ref[0]
    v = v_ref[0]
    w = jnp.exp(lw_ref[0])
    kk = kk_ref[0]
    bb = bb_ref[0]
    for i in range(HEAD_DIM):
        s = s_ref[0, i]
        sa = jnp.sum(s * kk, axis=0, keepdims=True)
        s2 = s * w - sa * bb + v[i:i + 1, :] * k
        so_ref[0, i] = s2
        y_ref[i:i + 1, :] = jnp.sum(s2 * r, axis=0, keepdims=True)
    y = y_ref[...]
    mu = jnp.mean(y, axis=0, keepdims=True)
    d = y - mu
    var = jnp.mean(d * d, axis=0, keepdims=True)
    yn = d * lax.rsqrt(var + GN_EPS) * gng_ref[0] + gnb_ref[0]
    bonus = jnp.sum(r * k * rk_ref[0], axis=0, keepdims=True) * v
    o_ref[0] = (yn + bonus) * g_ref[0]


def _rwkv_step(prep, state, p):
    nb, d = prep[0].shape
    n_heads = d // HEAD_DIM
    to_heads = lambda a: a.reshape(nb, n_heads, HEAD_DIM).transpose(1, 2, 0)
    vecs = [to_heads(a) for a in prep]
    params = [jnp.broadcast_to(p[n].reshape(n_heads, HEAD_DIM, 1), (n_heads, HEAD_DIM, nb))
              for n in ("r_k", "gn_g", "gn_b")]
    st = state.transpose(1, 2, 3, 0)
    sspec = pl.BlockSpec((1, HEAD_DIM, HEAD_DIM, nb), lambda h: (h, 0, 0, 0))
    vspec = pl.BlockSpec((1, HEAD_DIM, nb), lambda h: (h, 0, 0))
    st_new, o = pl.pallas_call(
        _rwkv_step_kernel, grid=(n_heads,),
        in_specs=[sspec] + [vspec] * 10, out_specs=[sspec, vspec],
        out_shape=[jax.ShapeDtypeStruct(st.shape, F32), jax.ShapeDtypeStruct((n_heads, HEAD_DIM, nb), F32)],
        scratch_shapes=[pltpu.VMEM((HEAD_DIM, nb), F32)],
        compiler_params=_cparams(("parallel",)), name="rwkv_step")(st, *vecs, *params)
    return o.transpose(2, 0, 1).reshape(nb, d), st_new.transpose(3, 0, 1, 2)


def _s5_disc_kernel(lr_ref, li_ref, ldt_ref, ar_ref, ai_ref, cr_ref, ci_ref):
    lr = lr_ref[...]
    li = li_ref[...]
    dt = jnp.exp(ldt_ref[...])
    mag = jnp.exp(lr * dt)
    ar = mag * jnp.cos(li * dt)
    ai = mag * jnp.sin(li * dt)
    den = lr * lr + li * li
    ar_ref[...] = ar
    ai_ref[...] = ai
    cr_ref[...] = ((ar - 1.0) * lr + ai * li) / den
    ci_ref[...] = (ai * lr - (ar - 1.0) * li) / den


def _s5_discretize(lam_re, lam_im, log_dt):
    g, p = lam_re.shape
    ldt = jnp.broadcast_to(log_dt.reshape(g, 1), (g, p))
    return pl.pallas_call(_s5_disc_kernel, out_shape=[jax.ShapeDtypeStruct((g, p), F32)] * 4,
                          name="s5_disc")(lam_re, lam_im, ldt)


def _s5_mix_tail(x, y_state, dsk_ref, wo_ref, wgt_ref, g_ref, b_ref, alpha):
    y = y_state + dsk_ref[...] * x
    zg = _gelu_tanh(y).astype(BF16)
    out = (jnp.dot(zg, wo_ref[...], preferred_element_type=F32)
           * jax.nn.sigmoid(jnp.dot(zg, wgt_ref[...], preferred_element_type=F32)))
    return _layer_norm(alpha * x + out, g_ref[...], b_ref[...])


def _s5_scan_kernel(x_ref, wb_ref, wcr_ref, wci_ref, ar_ref, ai_ref, dsk_ref, wo_ref, wgt_ref, g_ref, b_ref,
                    s0r_ref, s0i_ref, perm_ref, permt_ref, o_ref, sr_out, si_out, bur, bui, sr_st, si_st,
                    *, tc, nb, alpha):
    ti = pl.program_id(0)

    @pl.when(ti == 0)
    def _():
        sr_st[...] = s0r_ref[...]
        si_st[...] = s0i_ref[...]

    x = x_ref[...].reshape(nb * tc, x_ref.shape[2])
    xb = jnp.dot(perm_ref[...], x.astype(BF16), preferred_element_type=F32).astype(BF16)
    n_blk = wb_ref.shape[0]
    half = wb_ref.shape[2] // 2
    for j in range(n_blk):
        pj = jnp.dot(xb[:, j * LANES:(j + 1) * LANES], wb_ref[j], preferred_element_type=F32)
        bur[:, j * half:(j + 1) * half] = pj[:, :half]
        bui[:, j * half:(j + 1) * half] = pj[:, half:]
    n_state = bur.shape[1]
    for s in range(n_state // S5_SLAB):
        cols = slice(s * S5_SLAB, (s + 1) * S5_SLAB)
        ar = jnp.broadcast_to(ar_ref[:, cols], (nb, S5_SLAB))
        ai = jnp.broadcast_to(ai_ref[:, cols], (nb, S5_SLAB))

        sr, si = sr_st[:, cols], si_st[:, cols]
        for t in range(tc):
            rows = slice(t * nb, (t + 1) * nb)
            sr, si = ar * sr - ai * si + bur[rows, cols], ar * si + ai * sr + bui[rows, cols]
            bur[rows, cols] = sr
            bui[rows, cols] = si
        sr_st[:, cols] = sr
        si_st[:, cols] = si
    ys = []
    for j in range(n_blk):
        ys.append(_dot(bur[:, j * half:(j + 1) * half], wcr_ref[j])
                  + _dot(bui[:, j * half:(j + 1) * half], wci_ref[j]))
    y_tb = jnp.concatenate(ys, axis=1)
    y_state = _dot_sel_lhs(permt_ref[...], y_tb)
    out = _s5_mix_tail(x, y_state, dsk_ref, wo_ref, wgt_ref, g_ref, b_ref, alpha)
    o_ref[...] = out.reshape(o_ref.shape)

    @pl.when(ti == pl.num_programs(0) - 1)
    def _():
        sr_out[...] = sr_st[...]
        si_out[...] = si_st[...]


def _s5_scan(x, s0r, s0i, sp, ln, alpha):
    nb, t, d = x.shape
    tc = min(S5_TIME_CHUNK, t)
    assert t % tc == 0 and nb == SUBLANES
    n_state = s0r.shape[1]
    r = jnp.arange(nb * tc)
    perm = ((r[:, None] // nb == r[None, :] % tc) & (r[:, None] % nb == r[None, :] // tc)).astype(BF16)
    consts = [sp["wb"], sp["wcr"], sp["wci"], sp["ar"], sp["ai"], sp["d_skip"], sp["w_out"], sp["w_gate"],
              ln[0], ln[1], s0r, s0i, perm, perm.T]
    blk = pl.BlockSpec((nb, tc, d), lambda i: (0, i, 0))
    st_spec = pl.BlockSpec((nb, n_state), lambda i: (0, 0))
    kern = functools.partial(_s5_scan_kernel, tc=tc, nb=nb, alpha=alpha)
    return pl.pallas_call(
        kern, grid=(t // tc,),
        in_specs=[blk] + [_const_spec(a.shape) for a in consts],
        out_specs=[blk, st_spec, st_spec],
        out_shape=[jax.ShapeDtypeStruct((nb, t, d), F32), jax.ShapeDtypeStruct((nb, n_state), F32),
                   jax.ShapeDtypeStruct((nb, n_state), F32)],
        scratch_shapes=[pltpu.VMEM((tc * nb, n_state), F32), pltpu.VMEM((tc * nb, n_state), F32),
                        pltpu.VMEM((nb, n_state), F32), pltpu.VMEM((nb, n_state), F32)],
        compiler_params=_cparams(("arbitrary",)), name="s5_scan")(x, *consts)


def _s5_step_kernel(x_ref, wb_ref, wcr_ref, wci_ref, ar_ref, ai_ref, dsk_ref, wo_ref, wgt_ref, g_ref, b_ref,
                    s0r_ref, s0i_ref, o_ref, sr_out, si_out, *, alpha):
    x = x_ref[...]
    xb = x.astype(BF16)
    n_blk = wb_ref.shape[0]
    half = wb_ref.shape[2] // 2
    ys = []
    for j in range(n_blk):
        cols = slice(j * half, (j + 1) * half)
        pj = jnp.dot(xb[:, j * LANES:(j + 1) * LANES], wb_ref[j], preferred_element_type=F32)
        ar = ar_ref[:, cols]
        ai = ai_ref[:, cols]
        sr = s0r_ref[:, cols]
        si = s0i_ref[:, cols]
        nsr = ar * sr - ai * si + pj[:, :half]
        nsi = ar * si + ai * sr + pj[:, half:]
        sr_out[:, cols] = nsr
        si_out[:, cols] = nsi
        ys.append(_dot(nsr, wcr_ref[j]) + _dot(nsi, wci_ref[j]))
    y_state = jnp.concatenate(ys, axis=1)
    o_ref[...] = _s5_mix_tail(x, y_state, dsk_ref, wo_ref, wgt_ref, g_ref, b_ref, alpha)


def _s5_step(x, s0r, s0i, sp, ln, alpha):
    m, d = x.shape
    n_state = s0r.shape[1]
    args = [x, sp["wb"], sp["wcr"], sp["wci"], sp["ar"], sp["ai"], sp["d_skip"], sp["w_out"], sp["w_gate"],
            ln[0], ln[1], s0r, s0i]
    kern = functools.partial(_s5_step_kernel, alpha=alpha)
    return pl.pallas_call(
        kern,
        out_shape=[jax.ShapeDtypeStruct((m, d), F32), jax.ShapeDtypeStruct((m, n_state), F32),
                   jax.ShapeDtypeStruct((m, n_state), F32)],
        compiler_params=pltpu.CompilerParams(vmem_limit_bytes=VMEM_LIMIT_BYTES), name="s5_step")(*args)


def _s5_weights(prm, i):
    ar, ai, cr, ci = _s5_discretize(prm["lam_re"][i], prm["lam_im"][i], prm["log_dt"][i])
    b_re, b_im = prm["b_re"][i], prm["b_im"][i]
    bbr = cr[..., None] * b_re - ci[..., None] * b_im
    bbi = cr[..., None] * b_im + ci[..., None] * b_re
    n_groups, n_state, n_ch = bbr.shape
    gpb = LANES // n_ch
    n_blk = n_groups // gpb
    eye = jnp.eye(gpb, dtype=F32)

    def blk_in(bb):
        w = jnp.einsum("jgpc,gh->jgchp", bb.reshape(n_blk, gpb, n_state, n_ch), eye)
        return w.reshape(n_blk, LANES, gpb * n_state)

    def blk_out(cc):
        w = jnp.einsum("jgcp,gh->jhpgc", cc.reshape(n_blk, gpb, n_ch, n_state), eye)
        return w.reshape(n_blk, gpb * n_state, LANES)

    return dict(
        wb=jnp.concatenate([blk_in(bbr), blk_in(bbi)], axis=-1).astype(BF16),
        wcr=blk_out(prm["c_re"][i]).astype(BF16), wci=(-blk_out(prm["c_im"][i])).astype(BF16),
        ar=ar.reshape(1, -1), ai=ai.reshape(1, -1), d_skip=prm["d_skip"][i].reshape(1, -1),
        w_out=prm["w_glu_out"][i].astype(BF16), w_gate=prm["w_glu_gate"][i].astype(BF16))


def _rwkv_weights(prm, i, d_rwkv):
    rank_d = prm["w_w2"].shape[1]
    rank_a = prm["w_a2"].shape[1]
    assert rank_d + rank_a == LANES
    zeros_d = jnp.zeros((rank_d, d_rwkv), F32)
    zeros_a = jnp.zeros((rank_a, d_rwkv), F32)
    head = jnp.arange(d_rwkv) // HEAD_DIM
    ones = (head[:, None] == head[None, :]).astype(F32)
    row = lambda a: a.reshape(1, -1).astype(F32)
    return dict(
        mu=row(prm["mu_shift"][i]), w0=row(prm["w0"][i]), a0=row(prm["a0"][i]), k_k=row(prm["k_k"][i]),
        k_a=row(prm["k_a"][i]), r_k=row(prm["r_k"][i]), gn_g=row(prm["gn_g"][i]), gn_b=row(prm["gn_b"][i]),
        w_w2=jnp.concatenate([prm["w_w2"][i], zeros_a], axis=0).astype(BF16),
        w_a2=jnp.concatenate([zeros_d, prm["w_a2"][i]], axis=0).astype(BF16),
        w_g2=prm["w_g2"][i].astype(BF16), ones=ones.astype(BF16), avg=(ones / HEAD_DIM).astype(BF16))


def _trunk(x, cache_k, cache_v, wkv0, shift0, s50, prm, w):
    bsz, t, d = x.shape
    depth = prm["ln_g"].shape[0]
    alpha = (2.0 * depth) ** 0.25
    d_sb = prm["sb_bias"].shape[1] * HEAD_DIM
    d_rwkv = prm["w0"].shape[1]
    splits = (0, d_sb, 2 * d_sb, 3 * d_sb, prm["w_in_even"].shape[2])
    ln = lambda layer, j: (prm["ln_g"][layer, j].reshape(1, d), prm["ln_b"][layer, j].reshape(1, d))
    prompt = cache_k is None
    rows = x.reshape(bsz * t, d)
    out_k, out_v, out_wkv, out_shift, out_s5 = [], [], [], [], []
    for layer in range(depth):
        i = layer // 2
        ffn1 = tuple(_layer_weight(w[n], layer) for n in ("ffn1_wg", "ffn1_wu", "ffn1_wd"))
        ffn2 = tuple(_layer_weight(w[n], layer) for n in ("ffn2_wg", "ffn2_wu", "ffn2_wd"))
        if layer % 2 == 0:
            group_rows = t if prompt else bsz
            x1, q, k, v, pb, k_fm, v_fm = _stage(rows, ffn1, ln(layer, 0), alpha,
                                                 w_in=_layer_weight(w["w_in_even"], i),
                                                 proj_splits=splits, proj_transposed=(1, 2),
                                                 group_rows=group_rows)
            rp = _rwkv_weights(prm, i, d_rwkv)
            pb3 = pb.reshape(bsz, t, -1)
            if prompt:
                o_sb = yield (q.reshape(bsz, t, d_sb), k.reshape(bsz, t, d_sb), v.reshape(bsz, t, d_sb),
                              prm["sb_bias"][i])
                o_rwkv, wkv = _rwkv_chunked(pb3, shift0[i][:, None, :], wkv0[i], rp)
                o_rwkv = o_rwkv.reshape(bsz * t, d_rwkv)
            else:
                assert t == 1
                o_sb = yield (q.reshape(bsz, d_sb // HEAD_DIM, HEAD_DIM), cache_k[i], cache_v[i], prm["sb_bias"][i])
                o_rwkv, wkv = _rwkv_step(_rwkv_prep(pb, shift0[i], rp), wkv0[i], rp)
            o_sb = o_sb.reshape(bsz * t, d_sb)
            for fm, out in ((k_fm, out_k), (v_fm, out_v)):
                fm = fm.reshape(-1, d_sb // HEAD_DIM, HEAD_DIM, group_rows)
                if prompt:
                    out.append(fm.transpose(0, 3, 1, 2))
                else:
                    out.append(fm[0].transpose(2, 0, 1)[:, None])
            out_wkv.append(wkv)
            out_shift.append(pb3[:, -1])
            assert d_sb == d_rwkv
            mix = (o_sb, o_rwkv, _layer_weight(w["w_out_even"], i, 0, 2),
                   _layer_weight(w["w_out_even"], i, 1, 2), *ln(layer, 1))
            (rows,) = _stage(x1, ffn2, ln(layer, 2), alpha, mix=mix)
        else:
            sp = _s5_weights(prm, i)
            s0r = s50[i][..., 0].reshape(bsz, -1)
            s0i = s50[i][..., 1].reshape(bsz, -1)
            (x1,) = _stage(rows, ffn1, ln(layer, 0), alpha)
            if t > 1:
                x2, sr, si = _s5_scan(x1.reshape(bsz, t, d), s0r, s0i, sp, ln(layer, 1), alpha)
                x2 = x2.reshape(bsz * t, d)
            else:
                x2, sr, si = _s5_step(x1, s0r, s0i, sp, ln(layer, 1), alpha)
            n_groups = prm["lam_re"].shape[1]
            out_s5.append(jnp.stack([sr.reshape(bsz, n_groups, -1), si.reshape(bsz, n_groups, -1)], axis=-1))
            (rows,) = _stage(x2, ffn2, ln(layer, 2), alpha)
    y = rows.reshape(bsz, t, d)
    return (y, jnp.stack(out_k), jnp.stack(out_v), jnp.stack(out_wkv), jnp.stack(out_shift),
            jnp.stack(out_s5))


def kernel(x_prompt, x_sample, cache_k_sb, cache_v_sb, page_table, state_wkv, state_shift, state_s5, ln_g, ln_b, ffn1_wg, ffn1_wu, ffn1_wd, ffn2_wg, ffn2_wu, ffn2_wd, w_in_even, w_out_even, sb_bias, mu_shift, w0, w_w2, a0, w_a2, w_g2, k_k, k_a, r_k, gn_g, gn_b, lam_re, lam_im, log_dt, b_re, b_im, c_re, c_im, d_skip, w_glu_out, w_glu_gate):
    prm = dict(ln_g=ln_g, ln_b=ln_b, sb_bias=sb_bias, mu_shift=mu_shift, w0=w0, w_w2=w_w2, a0=a0, w_a2=w_a2,
               w_g2=w_g2, k_k=k_k, k_a=k_a, r_k=r_k.reshape(r_k.shape[0], -1), gn_g=gn_g, gn_b=gn_b,
               lam_re=lam_re, lam_im=lam_im, log_dt=log_dt, b_re=b_re, b_im=b_im, c_re=c_re, c_im=c_im,
               d_skip=d_skip, w_glu_out=w_glu_out, w_glu_gate=w_glu_gate, w_in_even=w_in_even)
    w = dict(ffn1_wg=ffn1_wg, ffn1_wu=ffn1_wu, ffn1_wd=ffn1_wd, ffn2_wg=ffn2_wg, ffn2_wu=ffn2_wu,
             ffn2_wd=ffn2_wd, w_in_even=w_in_even, w_out_even=w_out_even)
    w = {name: a.astype(BF16) for name, a in w.items()}
    nb = x_prompt.shape[0]
    n_even, n_odd = state_wkv.shape[0], state_s5.shape[0]
    wkv0 = jnp.zeros((n_even, nb) + state_wkv.shape[2:], F32)
    shift0 = jnp.zeros((n_even, nb, state_shift.shape[2]), F32)
    s50 = jnp.zeros((n_odd, nb) + state_s5.shape[2:], F32)
    cache_kt = cache_k_sb.transpose(0, 1, 3, 4, 2)
    cache_vt = cache_v_sb.transpose(0, 1, 3, 4, 2)
    trunk_p = _trunk(x_prompt, None, None, wkv0, shift0, s50, prm, w)
    trunk_s = _trunk(x_sample, cache_kt, cache_vt, state_wkv, state_shift, state_s5, prm, w)
    def advance(trunk, value):
        try:
            return trunk.send(value), None
        except StopIteration as done:
            return None, done.value

    (req_p, out_p), (req_s, out_s) = advance(trunk_p, None), advance(trunk_s, None)
    while out_p is None:
        o_p, o_s = _sb_attention(req_p, req_s, page_table)
        (req_p, out_p), (req_s, out_s) = advance(trunk_p, o_p), advance(trunk_s, o_s)
    y_p, p_k, p_v, p_wkv, p_shift, p_s5 = out_p
    y_s, s_k, s_v, s_wkv, s_shift, s_s5 = out_s
    return (y_p, y_s, p_k, p_v, p_wkv, p_shift, p_s5, s_k, s_v, s_wkv, s_shift, s_s5)
```
